```python
import math
import jax, jax.numpy as jnp
from jax import lax
import numpy as np

D_MODEL = 1024
BATCH = 4
SEQ = 8192
DEPTH = 1

DN_HEADS = 8
DN_HEAD_DIM = 128
DN_WIDTH = DN_HEADS * DN_HEAD_DIM
DN_CONV = 4
DN_CHUNK = 64
MOBA_HEADS = 8
MOBA_HEAD_DIM = 128
MOBA_WIDTH = MOBA_HEADS * MOBA_HEAD_DIM
MOBA_BLOCK = 256
MOBA_TOPK = 3
MOBA_QCHUNK = 16
REL_BUCKETS = 32
REL_MAX_DIST = 2048
N_BRANCH = 2
SPLITS = (3 * DN_WIDTH, DN_WIDTH, DN_HEADS, DN_HEADS, 3 * MOBA_WIDTH, MOBA_WIDTH, N_BRANCH * D_MODEL)
D_IN_PROJ = sum(SPLITS)
NORM_EPS = 1e-6
NEG_INF = -1e30

kernel_name = 'hybrid_deltanet_moba_block'


def rms_norm(x, w):
    xf = x.astype(jnp.float32)
    y = xf * lax.rsqrt(jnp.mean(xf * xf, axis=-1, keepdims=True) + NORM_EPS)
    return (y * w.astype(jnp.float32)).astype(x.dtype)


def l2_normalize(x):
    return x * lax.rsqrt(jnp.sum(x * x, axis=-1, keepdims=True) + NORM_EPS)


def causal_depthwise_conv(x, w):
    k_width, channels = w.shape
    return lax.conv_general_dilated(
        x, w[:, None, :].astype(x.dtype), window_strides=(1,), padding=[(k_width - 1, 0)],
        dimension_numbers=('NWC', 'WIO', 'NWC'), feature_group_count=channels)


def t5_bucket(dist):
    dist = jnp.maximum(dist, 0)
    max_exact = REL_BUCKETS // 2
    d = jnp.maximum(dist, 1).astype(jnp.float32)
    large = max_exact + (jnp.log(d / max_exact) / math.log(REL_MAX_DIST / max_exact)
                         * (REL_BUCKETS - max_exact)).astype(jnp.int32)
    large = jnp.minimum(large, REL_BUCKETS - 1)
    return jnp.where(dist < max_exact, dist, large)


def gated_delta_rule(q, k, v, g, beta):
    bsz, nh, s, dk = q.shape
    dv = v.shape[-1]
    c = DN_CHUNK
    n = s // c
    q = q * dk ** -0.5
    k_beta = k * beta[..., None]
    v_beta = v * beta[..., None]
    qc = q.reshape(bsz, nh, n, c, dk)
    kc = k.reshape(bsz, nh, n, c, dk)
    kbc = k_beta.reshape(bsz, nh, n, c, dk)
    vbc = v_beta.reshape(bsz, nh, n, c, dv)
    gc = jnp.cumsum(g.reshape(bsz, nh, n, c), axis=-1)
    tri_incl = jnp.tril(jnp.ones((c, c), dtype=bool))
    tri_strict = jnp.tril(jnp.ones((c, c), dtype=bool), -1)
    gdiff = gc[..., :, None] - gc[..., None, :]
    decay = jnp.where(tri_incl, jnp.exp(jnp.where(tri_incl, gdiff, 0.0)), 0.0)
    lower = jnp.where(tri_strict, jnp.einsum('bhnid,bhnjd->bhnij', kbc, kc) * decay, 0.0)
    eye = jnp.eye(c, dtype=jnp.float32)
    t_mat = lax.linalg.triangular_solve(eye + lower, jnp.broadcast_to(eye, lower.shape),
                                        left_side=True, lower=True, unit_diagonal=True)
    u = jnp.einsum('bhnij,bhnje->bhnie', t_mat, vbc)
    w = jnp.einsum('bhnij,bhnjd->bhnid', t_mat, kbc * jnp.exp(gc)[..., None])
    attn_intra = jnp.where(tri_incl, jnp.einsum('bhnid,bhnjd->bhnij', qc, kc) * decay, 0.0)
    q_dec = qc * jnp.exp(gc)[..., None]
    g_last = gc[..., -1]
    k_dec = kc * jnp.exp(g_last[..., None] - gc)[..., None]

    def step(state, xs):
        q_i, k_i, u_i, w_i, a_i, gl_i = xs
        v_new = u_i - jnp.einsum('bhcd,bhde->bhce', w_i, state)
        o = jnp.einsum('bhcd,bhde->bhce', q_i, state) + jnp.einsum('bhij,bhje->bhie', a_i, v_new)
        state = state * jnp.exp(gl_i)[..., None, None] + jnp.einsum('bhcd,bhce->bhde', k_i, v_new)
        return state, o

    xs = tuple(jnp.moveaxis(t, 2, 0) for t in (q_dec, k_dec, u, w, attn_intra, g_last))
    state0 = jnp.zeros((bsz, nh, dk, dv), dtype=jnp.float32)
    _, o = lax.scan(step, state0, xs)
    return jnp.moveaxis(o, 0, 2).reshape(bsz, nh, s, dv)


def moba_attention(q, k, v, rel_bias):
    bsz, nh, s, dh = q.shape
    blk = MOBA_BLOCK
    s_pad = -(-s // blk) * blk
    pad = ((0, 0), (0, 0), (0, s_pad - s), (0, 0))
    q, k, v = (jnp.pad(t.astype(jnp.float32), pad) for t in (q, k, v))
    q = q * dh ** -0.5
    nb = s_pad // blk
    k_blocks = k.reshape(bsz, nh, nb, blk, dh)
    v_blocks = v.reshape(bsz, nh, nb, blk, dh)
    q_blk = jnp.arange(s_pad, dtype=jnp.int32) // blk
    past = jnp.arange(nb, dtype=jnp.int32)[None, :] < q_blk[:, None]
    gate_scores = jnp.einsum('bhsd,bhnd->bhsn', q, k_blocks.mean(axis=3))
    gate_scores = jnp.where(past, gate_scores, NEG_INF)
    topk = min(MOBA_TOPK, nb)
    _, top_idx = lax.top_k(gate_scores, topk)
    sel_valid = jnp.arange(topk, dtype=jnp.int32)[None, :] < q_blk[:, None]
    bi = jnp.arange(bsz)[:, None, None, None]
    hi = jnp.arange(nh)[None, :, None, None]
    bias_hb = rel_bias.astype(jnp.float32).T
    offs = jnp.arange(blk, dtype=jnp.int32)
    cq = MOBA_QCHUNK

    def chunk(ci):
        start = ci * cq
        qc = lax.dynamic_slice_in_dim(q, start, cq, axis=2)
        idx = lax.dynamic_slice_in_dim(top_idx, start, cq, axis=2)
        valid = lax.dynamic_slice_in_dim(sel_valid, start, cq, axis=0)
        qpos = start + jnp.arange(cq, dtype=jnp.int32)
        k_sel = k_blocks[bi, hi, idx]
        v_sel = v_blocks[bi, hi, idx]
        s_past = jnp.einsum('bhcd,bhcrkd->bhcrk', qc, k_sel)
        kpos_past = idx[..., None] * blk + offs
        s_past = s_past + bias_hb[hi[..., None], t5_bucket(qpos[:, None, None] - kpos_past)]
        s_past = jnp.where(valid[:, :, None], s_past, NEG_INF).reshape(bsz, nh, cq, topk * blk)
        own_start = (start // blk) * blk
        k_own = lax.dynamic_slice_in_dim(k, own_start, blk, axis=2)
        v_own = lax.dynamic_slice_in_dim(v, own_start, blk, axis=2)
        dist_own = qpos[:, None] - (own_start + offs)[None, :]
        s_own = jnp.einsum('bhcd,bhkd->bhck', qc, k_own) + bias_hb[:, t5_bucket(dist_own)][None]
        s_own = jnp.where(dist_own >= 0, s_own, NEG_INF)
        p = jax.nn.softmax(jnp.concatenate([s_past, s_own], axis=-1), axis=-1)
        p_past = p[..., :topk * blk].reshape(bsz, nh, cq, topk, blk)
        p_own = p[..., topk * blk:]
        return (jnp.einsum('bhcrk,bhcrkd->bhcd', p_past, v_sel)
                + jnp.einsum('bhck,bhkd->bhcd', p_own, v_own))

    out = lax.map(chunk, jnp.arange(s_pad // cq))
    out = jnp.moveaxis(out, 0, 2).reshape(bsz, nh, s_pad, dh)
    return out[:, :, :s]


def hybrid_layer(x, norm_pre_w, w_in, b_gate, conv_w, dn_a_log, dn_dt_bias, dn_onorm_w,
                 rel_bias, w_branch, w_out, norm_post_w):
    bsz, s, d = x.shape
    h = rms_norm(x, norm_pre_w)
    proj = jnp.einsum('bsd,df->bsf', h, w_in)
    cuts = []
    acc = 0
    for size in SPLITS[:-1]:
        acc += size
        cuts.append(acc)
    qkv_a, z_a, beta_a, dec_a, qkv_b, z_b, gate_logits = jnp.split(proj, cuts, axis=-1)

    qkv_a = jax.nn.silu(causal_depthwise_conv(qkv_a.astype(jnp.float32), conv_w.astype(jnp.float32)))
    qa, ka, va = jnp.split(qkv_a, 3, axis=-1)
    heads_a = lambda t: t.reshape(bsz, s, DN_HEADS, DN_HEAD_DIM).transpose(0, 2, 1, 3)
    qa = l2_normalize(heads_a(qa))
    ka = l2_normalize(heads_a(ka))
    va = heads_a(va)
    beta = jax.nn.sigmoid(beta_a.astype(jnp.float32)).transpose(0, 2, 1)
    g = (-jnp.exp(dn_a_log.astype(jnp.float32))
         * jax.nn.softplus(dec_a.astype(jnp.float32) + dn_dt_bias.astype(jnp.float32))).transpose(0, 2, 1)
    oa = gated_delta_rule(qa, ka, va, g, beta).transpose(0, 2, 1, 3)
    oa = rms_norm(oa, dn_onorm_w).reshape(bsz, s, DN_WIDTH)
    ya = (oa * jax.nn.silu(z_a.astype(jnp.float32))).astype(x.dtype)

    qb, kb, vb = jnp.split(qkv_b, 3, axis=-1)
    heads_b = lambda t: t.reshape(bsz, s, MOBA_HEADS, MOBA_HEAD_DIM).transpose(0, 2, 1, 3)
    ob = moba_attention(heads_b(qb), heads_b(kb), heads_b(vb), rel_bias)
    ob = ob.transpose(0, 2, 1, 3).reshape(bsz, s, MOBA_WIDTH)
    yb = (ob * jax.nn.silu(z_b.astype(jnp.float32))).astype(x.dtype)

    gates = jax.nn.sigmoid((gate_logits + b_gate).astype(jnp.float32)).reshape(bsz, s, N_BRANCH, d)
    pa = jnp.einsum('bsf,fd->bsd', ya, w_branch[0]).astype(jnp.float32)
    pb = jnp.einsum('bsf,fd->bsd', yb, w_branch[1]).astype(jnp.float32)
    merged = (gates[:, :, 0] * pa + gates[:, :, 1] * pb).astype(x.dtype)
    out = jnp.einsum('bsd,de->bse', merged, w_out)
    return x + rms_norm(out, norm_post_w)


def setup_inputs(seed: int = 0) -> dict:
    key = jax.random.key(seed)
    ks = jax.random.split(key, 12)
    f32 = jnp.float32
    x = jax.random.normal(ks[0], (BATCH, SEQ, D_MODEL), f32)
    norm_pre_w = 1.0 + 0.1 * jax.random.normal(ks[1], (D_MODEL,), f32)
    w_in = jax.random.normal(ks[2], (D_MODEL, D_IN_PROJ), f32) * D_MODEL ** -0.5
    b_gate = 0.1 * jax.random.normal(ks[3], (N_BRANCH * D_MODEL,), f32)
    conv_w = jax.random.normal(ks[4], (DN_CONV, 3 * DN_WIDTH), f32) * DN_CONV ** -0.5
    dn_a_log = jnp.log(jax.random.uniform(ks[5], (DN_HEADS,), f32, minval=1.0, maxval=16.0))
    dt = jnp.exp(jax.random.uniform(ks[6], (DN_HEADS,), f32, minval=math.log(1e-3), maxval=math.log(1e-1)))
    dn_dt_bias = dt + jnp.log(-jnp.expm1(-dt))
    dn_onorm_w = 1.0 + 0.1 * jax.random.normal(ks[7], (DN_HEAD_DIM,), f32)
    rel_bias = 0.5 * jax.random.normal(ks[8], (REL_BUCKETS, MOBA_HEADS), f32)
    w_branch = jax.random.normal(ks[9], (N_BRANCH, DN_WIDTH, D_MODEL), f32) * DN_WIDTH ** -0.5
    w_out = jax.random.normal(ks[10], (D_MODEL, D_MODEL), f32) * D_MODEL ** -0.5
    norm_post_w = 1.0 + 0.1 * jax.random.normal(ks[11], (D_MODEL,), f32)
    return {'x': x, 'norm_pre_w': norm_pre_w, 'w_in': w_in, 'b_gate': b_gate, 'conv_w': conv_w,
            'dn_a_log': dn_a_log, 'dn_dt_bias': dn_dt_bias, 'dn_onorm_w': dn_onorm_w,
            'rel_bias': rel_bias, 'w_branch': w_branch, 'w_out': w_out, 'norm_post_w': norm_post_w}


def reference(x, norm_pre_w, w_in, b_gate, conv_w, dn_a_log, dn_dt_bias, dn_onorm_w,
              rel_bias, w_branch, w_out, norm_post_w):
    h = x
    for _ in range(DEPTH):
        h = hybrid_layer(h, norm_pre_w, w_in, b_gate, conv_w, dn_a_log, dn_dt_bias, dn_onorm_w,
                         rel_bias, w_branch, w_out, norm_post_w)
    return h
```

```python
import functools
import math

import jax
import jax.numpy as jnp
from jax import lax
from jax.experimental import pallas as pl
from jax.experimental.pallas import tpu as pltpu

F32 = jnp.float32
BF16 = jnp.bfloat16

LANES = 128
DN_CHUNK = 64
DN_GROUP = 256
DN_CONV_HALO = 8
MOBA_BLOCK = 256
MOBA_TOPK = 3
REL_BUCKETS = 32
REL_MAX_DIST = 2048
NORM_EPS = 1e-6
NEG_INF = -1e30
VMEM_LIMIT = 56 * 1024 * 1024


def _cparams(sem):
    return pltpu.CompilerParams(dimension_semantics=sem, vmem_limit_bytes=VMEM_LIMIT)


def _rmsnorm_kernel(x_ref, w_ref, o_ref):
    x = x_ref[...]
    y = x * lax.rsqrt(jnp.mean(x * x, axis=-1, keepdims=True) + NORM_EPS)
    o_ref[...] = (y * w_ref[...]).astype(o_ref.dtype)


def _rmsnorm(x2d, w, tm=1024):
    t, d = x2d.shape
    return pl.pallas_call(
        _rmsnorm_kernel,
        grid=(t // tm,),
        in_specs=[pl.BlockSpec((tm, d), lambda i: (i, 0)), pl.BlockSpec((1, d), lambda i: (0, 0))],
        out_specs=pl.BlockSpec((tm, d), lambda i: (i, 0)),
        out_shape=jax.ShapeDtypeStruct((t, d), BF16),
        compiler_params=_cparams(("parallel",)),
        name="rmsnorm_pre",
    )(x2d, w.reshape(1, d))


def _mm_kernel(h_ref, w_ref, o_ref):
    o_ref[...] = jnp.dot(h_ref[...], w_ref[...], preferred_element_type=F32).astype(o_ref.dtype)


def _matmul(h, w, out_dtype, tm, tn, name):
    t, d = h.shape
    n = w.shape[1]
    return pl.pallas_call(
        _mm_kernel,
        grid=(t // tm, n // tn),
        in_specs=[pl.BlockSpec((tm, d), lambda i, j: (i, 0)), pl.BlockSpec((d, tn), lambda i, j: (0, j))],
        out_specs=pl.BlockSpec((tm, tn), lambda i, j: (i, j)),
        out_shape=jax.ShapeDtypeStruct((t, n), out_dtype),
        compiler_params=_cparams(("parallel", "arbitrary")),
        name=name,
    )(h, w)


def _mm_t_kernel(wt_ref, h_ref, o_ref, *, n_scaled_tiles, scale):
    acc = lax.dot_general(wt_ref[...], h_ref[0], (((1,), (1,)), ((), ())), preferred_element_type=F32)
    s = jnp.where(pl.program_id(2) < n_scaled_tiles, scale, 1.0).astype(F32)
    res = (acc * s).astype(o_ref.dtype)
    for c in range(o_ref.shape[1]):
        o_ref[0, c] = res[:, c * MOBA_BLOCK:(c + 1) * MOBA_BLOCK]


def _matmul_t(wt, h3, n_scaled_rows, scale, tm=1024, tn=512):
    b, s, d = h3.shape
    n = wt.shape[0]
    kern = functools.partial(_mm_t_kernel, n_scaled_tiles=n_scaled_rows // tn, scale=scale)
    return pl.pallas_call(
        kern,
        grid=(b, s // tm, n // tn),
        in_specs=[pl.BlockSpec((tn, d), lambda bi, i, j: (j, 0)), pl.BlockSpec((1, tm, d), lambda bi, i, j: (bi, i, 0))],
        out_specs=pl.BlockSpec((1, tm // MOBA_BLOCK, tn, MOBA_BLOCK), lambda bi, i, j: (bi, i, j, 0)),
        out_shape=jax.ShapeDtypeStruct((b, s // MOBA_BLOCK, n, MOBA_BLOCK), BF16),
        compiler_params=_cparams(("parallel", "parallel", "arbitrary")),
        name="in_proj_t",
    )(wt, h3)


def _deltanet_kernel(q_ref, k_ref, v_ref, z_ref, sm_ref, cq_ref, ck_ref, cv_ref, alog_ref, dtb_ref, onw_ref,
                     o_ref, state_ref, pq_ref, pk_ref, pv_ref, vnew_ref, *, n_heads):
    h = pl.program_id(1)
    g = pl.program_id(2)
    G, C, D = DN_GROUP, DN_CHUNK, LANES

    @pl.when(g == 0)
    def _():
        state_ref[...] = jnp.zeros_like(state_ref)
        pq_ref[...] = jnp.zeros_like(pq_ref)
        pk_ref[...] = jnp.zeros_like(pk_ref)
        pv_ref[...] = jnp.zeros_like(pv_ref)

    def conv_silu(x_ref, prev_ref, w_ref):
        cur = x_ref[0].astype(F32)
        ext = jnp.concatenate([prev_ref[...], cur], axis=0)
        w = w_ref[...]
        kw = w.shape[0]
        base = DN_CONV_HALO - (kw - 1)
        y = ext[base:base + G] * w[0:1]
        for kk in range(1, kw):
            y = y + ext[base + kk:base + kk + G] * w[kk:kk + 1]
        prev_ref[...] = cur[G - DN_CONV_HALO:G]
        return y * jax.nn.sigmoid(y)

    def l2n(x):
        return x * lax.rsqrt(jnp.sum(x * x, axis=-1, keepdims=True) + NORM_EPS)

    q = l2n(conv_silu(q_ref, pq_ref, cq_ref)) * (D ** -0.5)
    k = l2n(conv_silu(k_ref, pk_ref, ck_ref))
    v = conv_silu(v_ref, pv_ref, cv_ref)

    cs = sm_ref[0]
    lane = lax.broadcasted_iota(jnp.int32, (G, LANES), 1)
    row = lax.broadcasted_iota(jnp.int32, (G, LANES), 0)
    beta_all = jax.nn.sigmoid(cs)
    xg = cs + dtb_ref[...]
    softplus = jnp.maximum(xg, 0.0) + jnp.log1p(jnp.exp(-jnp.abs(xg)))
    g_all = -jnp.exp(alog_ref[...]) * softplus
    pos = row & (C - 1)
    gc_all = g_all
    shift = 1
    while shift < C:
        gc_all = gc_all + jnp.where(pos >= shift, pltpu.roll(gc_all, shift, 0), 0.0)
        shift *= 2
    beta_col = jnp.sum(jnp.where(lane == h, beta_all, 0.0), axis=1, keepdims=True)
    gc_col = jnp.sum(jnp.where(lane == h + n_heads, gc_all, 0.0), axis=1, keepdims=True)

    ri = lax.broadcasted_iota(jnp.int32, (G, G), 0)
    ci = lax.broadcasted_iota(jnp.int32, (G, G), 1)
    gc_row = jnp.sum(jnp.where(ri == ci, gc_col, 0.0), axis=0, keepdims=True)
    gl_col = jnp.sum(jnp.where(ci == (ri | (C - 1)), gc_row, 0.0), axis=1, keepdims=True)
    lc = C.bit_length() - 1
    same = (ri >> lc) == (ci >> lc)
    tri_incl = same & (ri >= ci)
    tri_strict = same & (ri > ci)
    decay = jnp.where(tri_incl, jnp.exp(jnp.where(tri_incl, gc_col - gc_row, 0.0)), 0.0)

    kb = k * beta_col
    vb = v * beta_col
    k16 = k.astype(BF16)
    nt = (((1,), (1,)), ((), ()))
    kk = lax.dot_general(kb.astype(BF16), k16, nt, preferred_element_type=F32)
    lower = jnp.where(tri_strict, kk * decay, 0.0)
    qk = lax.dot_general(q.astype(BF16), k16, nt, preferred_element_type=F32)
    attn = jnp.where(tri_incl, qk * decay, 0.0).astype(BF16)

    def pair_mask(s):
        ls = s.bit_length() - 1
        return ((ri >> (ls + 1)) == (ci >> (ls + 1))) & (((ri >> ls) & 1) == 1) & (((ci >> ls) & 1) == 0)

    tinv = jnp.where(ri == ci, 1.0, 0.0) - jnp.where(pair_mask(1), lower, 0.0)
    s = 2
    while s < C:
        m16 = jnp.where(pair_mask(s), lower, 0.0).astype(BF16)
        t16 = tinv.astype(BF16)
        md = jnp.dot(m16, t16, preferred_element_type=F32)
        tinv = tinv - jnp.dot(t16, md.astype(BF16), preferred_element_type=F32)
        s *= 2

    egc = jnp.exp(gc_col)
    rhs = jnp.concatenate([vb, kb * egc], axis=1).astype(BF16)
    uw = jnp.dot(tinv.astype(BF16), rhs, preferred_element_type=F32)
    u = uw[:, :D]
    w = uw[:, D:]
    q_dec = q * egc
    k_dec = k * jnp.exp(gl_col - gc_col)
    k_dec_t = k_dec.T
    egl = jnp.exp(gl_col)
    ct = lax.broadcasted_iota(jnp.int32, (D, G), 1) >> lc

    vnew_ref[...] = jnp.zeros_like(vnew_ref)
    outs = []
    for c in range(G // C):
        r0 = c * C
        st = state_ref[...]
        wq = jnp.concatenate([w[r0:r0 + C], q_dec[r0:r0 + C]], axis=0).astype(BF16)
        ws_qs = jnp.dot(wq, st.astype(BF16), preferred_element_type=F32)
        v_new = u[r0:r0 + C] - ws_qs[:C]
        vnew_ref[r0:r0 + C, :] = v_new.astype(BF16)
        vn = vnew_ref[...]
        outs.append(ws_qs[C:] + jnp.dot(attn[r0:r0 + C], vn, preferred_element_type=F32))
        kt_c = jnp.where(ct == c, k_dec_t, 0.0).astype(BF16)
        e_c = jnp.concatenate([egl[r0:r0 + C]] * (D // C), axis=0)
        state_ref[...] = st * e_c + jnp.dot(kt_c, vn, preferred_element_type=F32)

    o = jnp.concatenate(outs, axis=0)
    o = o * lax.rsqrt(jnp.mean(o * o, axis=-1, keepdims=True) + NORM_EPS) * onw_ref[...]
    z = z_ref[0].astype(F32)
    o_ref[0] = (o * (z * jax.nn.sigmoid(z))).astype(o_ref.dtype)


def _deltanet(p3, small3, conv_w, a_log_row, dt_row, onorm_w, n_heads, col_q, col_k, col_v, col_z):
    b, s, _ = p3.shape
    G, D = DN_GROUP, LANES
    kw = conv_w.shape[0]
    blk = lambda off: pl.BlockSpec((1, G, D), lambda bi, hi, gi: (bi, gi, off + hi))
    cw = lambda off: pl.BlockSpec((kw, D), lambda bi, hi, gi: (0, off + hi))
    row_spec = pl.BlockSpec((1, LANES), lambda bi, hi, gi: (0, 0))
    return pl.pallas_call(
        functools.partial(_deltanet_kernel, n_heads=n_heads),
        grid=(b, n_heads, s // G),
        in_specs=[blk(col_q), blk(col_k), blk(col_v), blk(col_z),
                  pl.BlockSpec((1, G, LANES), lambda bi, hi, gi: (bi, gi, 0)),
                  cw(0), cw(n_heads), cw(2 * n_heads), row_spec, row_spec, row_spec],
        out_specs=pl.BlockSpec((1, G, D), lambda bi, hi, gi: (bi, gi, hi)),
        out_shape=jax.ShapeDtypeStruct((b, s, n_heads * D), BF16),
        scratch_shapes=[pltpu.VMEM((D, D), F32),
                        pltpu.VMEM((DN_CONV_HALO, D), F32), pltpu.VMEM((DN_CONV_HALO, D), F32),
                        pltpu.VMEM((DN_CONV_HALO, D), F32), pltpu.VMEM((G, D), BF16)],
        compiler_params=_cparams(("parallel", "parallel", "arbitrary")),
        name="gated_deltanet",
    )(p3, p3, p3, p3, small3, conv_w, conv_w, conv_w, a_log_row, dt_row, onorm_w)


def _t5_bucket_f32(dist):
    dist = jnp.maximum(dist, 0)
    max_exact = REL_BUCKETS // 2
    d = jnp.maximum(dist, 1).astype(F32)
    large = max_exact + (jnp.log(d / max_exact) / math.log(REL_MAX_DIST / max_exact)
                         * (REL_BUCKETS - max_exact)).astype(jnp.int32)
    large = jnp.minimum(large, REL_BUCKETS - 1)
    return jnp.where(dist < max_exact, dist, large)


def _n_near_tiles():
    d_far = REL_MAX_DIST
    return -(-(d_far + MOBA_BLOCK) // MOBA_BLOCK) + 1


def _moba_kernel(rb_ref, qt_ref, k_ref, vt_ref, z_ref, o_ref, kmean_ref, bias_ref, sel_ref, *, n_near):
    hh = pl.program_id(0)
    bb = pl.program_id(1)
    i = pl.program_id(2)
    BLK, D = MOBA_BLOCK, LANES
    nb = vt_ref.shape[1]

    @pl.when((bb == 0) & (i == 0))
    def _():
        kr = lax.broadcasted_iota(jnp.int32, (BLK, BLK), 0)
        qc = lax.broadcasted_iota(jnp.int32, (BLK, BLK), 1)
        for t in range(n_near + 1):
            dist = t * BLK + qc - kr
            bucket = _t5_bucket_f32(dist)
            if t == n_near:
                bucket = jnp.full((BLK, BLK), REL_BUCKETS - 1, jnp.int32)
            val = jnp.full((BLK, BLK), rb_ref[0, hh], F32)
            for bk in range(1, REL_BUCKETS):
                val = jnp.where(bucket == bk, rb_ref[bk, hh], val)
            if t == 0:
                val = jnp.where(dist >= 0, val, NEG_INF)
            bias_ref[t] = val

    @pl.when(i == 0)
    def _():
        def body(j, carry):
            kj = k_ref[0, pl.ds(pl.multiple_of(j * BLK, BLK), BLK), :].astype(F32)
            kmean_ref[pl.ds(j, 1), :] = jnp.mean(kj, axis=0, keepdims=True)
            return carry
        lax.fori_loop(0, nb, body, 0)

    qt = qt_ref[0, 0]
    gate = jnp.dot(kmean_ref[...].astype(BF16), qt, preferred_element_type=F32)
    jrow = lax.broadcasted_iota(jnp.int32, gate.shape, 0)
    past = jrow < i
    sc = jnp.where(past, gate, NEG_INF)
    rank = jnp.zeros(gate.shape, F32)
    for jj in range(nb):
        rj = sc[jj:jj + 1, :]
        beats = (rj > sc) | ((rj == sc) & (jj < jrow))
        rank = rank + jnp.where(beats, 1.0, 0.0)
    selected = past & (rank < float(MOBA_TOPK))
    sel_ref[...] = jnp.where(selected, 0.0, NEG_INF)

    def tile(j):
        off = pl.multiple_of(j * BLK, BLK)
        kj = k_ref[0, pl.ds(off, BLK), :]
        vjt = vt_ref[0, j]
        return jnp.dot(kj, qt, preferred_element_type=F32), vjt

    s0, v0t = tile(i)
    s0 = s0 + bias_ref[0]
    m0 = jnp.max(s0, axis=0, keepdims=True)
    p0 = jnp.exp(s0 - m0)
    l0 = jnp.sum(p0, axis=0, keepdims=True)
    acc0 = jnp.dot(v0t, p0.astype(BF16), preferred_element_type=F32)

    def body(j, carry):
        m, l, acc = carry
        sj, vjt = tile(j)
        t = jnp.minimum(i - j, n_near)
        sj = sj + bias_ref[t] + sel_ref[pl.ds(j, 1), :]
        m_new = jnp.maximum(m, jnp.max(sj, axis=0, keepdims=True))
        alpha = jnp.exp(m - m_new)
        p = jnp.exp(sj - m_new)
        l = l * alpha + jnp.sum(p, axis=0, keepdims=True)
        acc = acc * alpha + jnp.dot(vjt, p.astype(BF16), preferred_element_type=F32)
        return m_new, l, acc

    m, l, acc = lax.fori_loop(0, i, body, (m0, l0, acc0))
    out_t = acc / l
    out = out_t.T
    z = z_ref[0].astype(F32)
    o_ref[0] = (out * (z * jax.nn.sigmoid(z))).astype(o_ref.dtype)


def _moba(rel_bias, qvt, p3, n_heads, col_k, col_z):
    b, s, _ = p3.shape
    BLK, D = MOBA_BLOCK, LANES
    nb = s // BLK
    n_near = _n_near_tiles()
    return pl.pallas_call(
        functools.partial(_moba_kernel, n_near=n_near),
        grid=(n_heads, b, nb),
        in_specs=[pl.BlockSpec(memory_space=pltpu.SMEM),
                  pl.BlockSpec((1, 1, D, BLK), lambda hi, bi, i: (bi, i, hi, 0)),
                  pl.BlockSpec((1, s, D), lambda hi, bi, i: (bi, 0, col_k + hi)),
                  pl.BlockSpec((1, nb, D, BLK), lambda hi, bi, i: (bi, 0, n_heads + hi, 0)),
                  pl.BlockSpec((1, BLK, D), lambda hi, bi, i: (bi, i, col_z + hi))],
        out_specs=pl.BlockSpec((1, BLK, D), lambda hi, bi, i: (bi, i, hi)),
        out_shape=jax.ShapeDtypeStruct((b, s, n_heads * D), BF16),
        scratch_shapes=[pltpu.VMEM((nb, D), F32), pltpu.VMEM((n_near + 1, BLK, BLK), F32),
                        pltpu.VMEM((nb, BLK), F32)],
        compiler_params=_cparams(("arbitrary", "arbitrary", "arbitrary")),
        name="moba_attention",
    )(rel_bias, qvt, p3, qvt, p3)


def _merge_kernel(ya_ref, yb_ref, g0_ref, g1_ref, x_ref, w0_ref, w1_ref, wo_ref, b0_ref, b1_ref, nw_ref, o_ref):
    pa = jnp.dot(ya_ref[...], w0_ref[...], preferred_element_type=F32)
    pb = jnp.dot(yb_ref[...], w1_ref[...], preferred_element_type=F32)
    g0 = jax.nn.sigmoid(g0_ref[...].astype(F32) + b0_ref[...])
    g1 = jax.nn.sigmoid(g1_ref[...].astype(F32) + b1_ref[...])
    merged = (g0 * pa + g1 * pb).astype(BF16)
    out = jnp.dot(merged, wo_ref[...], preferred_element_type=F32)
    y = out * lax.rsqrt(jnp.mean(out * out, axis=-1, keepdims=True) + NORM_EPS) * nw_ref[...]
    o_ref[...] = x_ref[...] + y


def _merge(ya, yb, p2, x2d, w0, w1, wo, b_gate, norm_w, col_g0, col_g1, tm=512):
    t, d = x2d.shape
    row = lambda c: pl.BlockSpec((tm, d), lambda i: (i, c))
    full = pl.BlockSpec((d, d), lambda i: (0, 0))
    vec = lambda c: pl.BlockSpec((1, d), lambda i: (0, c))
    return pl.pallas_call(
        _merge_kernel,
        grid=(t // tm,),
        in_specs=[row(0), row(0), row(col_g0), row(col_g1), row(0), full, full, full, vec(0), vec(1), vec(0)],
        out_specs=row(0),
        out_shape=jax.ShapeDtypeStruct((t, d), F32),
        compiler_params=_cparams(("parallel",)),
        name="merge_out",
    )(ya, yb, p2, p2, x2d, w0, w1, wo, b_gate.reshape(1, -1), b_gate.reshape(1, -1), norm_w.reshape(1, d))


def kernel(x, norm_pre_w, w_in, b_gate, conv_w, dn_a_log, dn_dt_bias, dn_onorm_w, rel_bias, w_branch, w_out,
           norm_post_w):
    b, s, d = x.shape
    n_heads = dn_a_log.shape[0]
    hd = dn_onorm_w.shape[0]
    width = n_heads * hd
    assert hd == LANES and rel_bias.shape[1] == n_heads and width == d
    assert s % MOBA_BLOCK == 0 and s % DN_GROUP == 0 and 2 * n_heads <= LANES
    t = b * s

    c_za = 3 * width
    c_beta = c_za + width
    c_qb = c_beta + 2 * n_heads
    c_kb = c_qb + width
    c_vb = c_kb + width
    c_zb = c_vb + width
    c_gate = c_zb + width
    w16 = w_in.astype(BF16)
    w_main = jnp.concatenate([w16[:, :c_beta], w16[:, c_kb:c_vb], w16[:, c_zb:]], axis=1)
    w_small = jnp.pad(w16[:, c_beta:c_qb], ((0, 0), (0, LANES - 2 * n_heads)))
    w_t = jnp.concatenate([w16[:, c_qb:c_kb], w16[:, c_vb:c_zb]], axis=1).T
    nblk = width // LANES
    col_q, col_k, col_v, col_za = 0, nblk, 2 * nblk, 3 * nblk
    col_kb, col_zb = 4 * nblk, 5 * nblk

    x2d = x.reshape(t, d)
    hn = _rmsnorm(x2d, norm_pre_w)
    p2 = _matmul(hn, w_main, BF16, tm=1024, tn=1024, name="in_proj_main")
    small = _matmul(hn, w_small, F32, tm=1024, tn=LANES, name="in_proj_small")
    qvt = _matmul_t(w_t, hn.reshape(b, s, d), n_scaled_rows=width, scale=hd ** -0.5)
    p3 = p2.reshape(b, s, -1)

    pad = (0, LANES - 2 * n_heads)
    a_log_row = jnp.pad(jnp.concatenate([jnp.zeros_like(dn_a_log), dn_a_log]), pad).reshape(1, LANES).astype(F32)
    dt_row = jnp.pad(jnp.concatenate([jnp.zeros_like(dn_dt_bias), dn_dt_bias]), pad).reshape(1, LANES).astype(F32)
    ya = _deltanet(p3, small.reshape(b, s, LANES), conv_w.astype(F32), a_log_row, dt_row,
                   dn_onorm_w.reshape(1, hd).astype(F32), n_heads, col_q, col_k, col_v, col_za)
    yb = _moba(rel_bias.astype(F32), qvt, p3, n_heads, col_kb, col_zb)

    gate_blk = (6 * nblk * LANES) // d
    out = _merge(ya.reshape(t, d), yb.reshape(t, d), p2, x2d,
                 w_branch[0].astype(BF16), w_branch[1].astype(BF16), w_out.astype(BF16),
                 b_gate.astype(F32), norm_post_w.astype(F32), gate_blk, gate_blk + 1)
    return out.reshape(b, s, d)
```

```python
import functools
import math

import jax
import jax.numpy as jnp
from jax import lax
from jax.experimental import pallas as pl
from jax.experimental.pallas import tpu as pltpu

F32 = jnp.float32
BF16 = jnp.bfloat16

LANES = 128
DN_CHUNK = 64
DN_GROUP = 256
DN_CONV_HALO = 8
MOBA_BLOCK = 256
MOBA_TOPK = 3
REL_BUCKETS = 32
REL_MAX_DIST = 2048
NORM_EPS = 1e-6
NEG_INF = -1e30
VMEM_LIMIT = 56 * 1024 * 1024
NT_DIMS = (((1,), (1,)), ((), ()))


def _cparams(sem):
    return pltpu.CompilerParams(dimension_semantics=sem, vmem_limit_bytes=VMEM_LIMIT)


def _rmsnorm_kernel(x_ref, w_ref, o_ref):
    x = x_ref[...]
    y = x * lax.rsqrt(jnp.mean(x * x, axis=-1, keepdims=True) + NORM_EPS)
    o_ref[...] = (y * w_ref[...]).astype(o_ref.dtype)


def _rmsnorm(x2d, w, tm=1024):
    t, d = x2d.shape
    return pl.pallas_call(
        _rmsnorm_kernel,
        grid=(t // tm,),
        in_specs=[pl.BlockSpec((tm, d), lambda i: (i, 0)), pl.BlockSpec((1, d), lambda i: (0, 0))],
        out_specs=pl.BlockSpec((tm, d), lambda i: (i, 0)),
        out_shape=jax.ShapeDtypeStruct((t, d), BF16),
        compiler_params=_cparams(("parallel",)),
        name="rmsnorm_pre",
    )(x2d, w.reshape(1, d))


def _mm_kernel(h_ref, w_ref, o_ref, *, scaled_tile, scale):
    acc = jnp.dot(h_ref[...], w_ref[...], preferred_element_type=F32)
    if scaled_tile is not None:
        acc = acc * jnp.where(pl.program_id(1) == scaled_tile, scale, 1.0).astype(F32)
    o_ref[...] = acc.astype(o_ref.dtype)


def _matmul(h, w, out_dtype, tm, tn, name, scaled_tile=None, scale=1.0):
    t, d = h.shape
    n = w.shape[1]
    return pl.pallas_call(
        functools.partial(_mm_kernel, scaled_tile=scaled_tile, scale=scale),
        grid=(t // tm, n // tn),
        in_specs=[pl.BlockSpec((tm, d), lambda i, j: (i, 0)), pl.BlockSpec((d, tn), lambda i, j: (0, j))],
        out_specs=pl.BlockSpec((tm, tn), lambda i, j: (i, j)),
        out_shape=jax.ShapeDtypeStruct((t, n), out_dtype),
        compiler_params=_cparams(("parallel", "arbitrary")),
        name=name,
    )(h, w)


def _deltanet_kernel(q_ref, k_ref, v_ref, z_ref, sm_ref, cq_ref, ck_ref, cv_ref, alog_ref, dtb_ref, onw_ref,
                     o_ref, state_ref, pq_ref, pk_ref, pv_ref, vnew_ref, *, n_heads):
    h = pl.program_id(1)
    g = pl.program_id(2)
    G, C, D = DN_GROUP, DN_CHUNK, LANES

    @pl.when(g == 0)
    def _():
        state_ref[...] = jnp.zeros_like(state_ref)
        pq_ref[...] = jnp.zeros_like(pq_ref)
        pk_ref[...] = jnp.zeros_like(pk_ref)
        pv_ref[...] = jnp.zeros_like(pv_ref)

    def conv_silu(x_ref, prev_ref, w_ref):
        cur = x_ref[0].astype(F32)
        ext = jnp.concatenate([prev_ref[...], cur], axis=0)
        w = w_ref[...]
        kw = w.shape[0]
        base = DN_CONV_HALO - (kw - 1)
        y = ext[base:base + G] * w[0:1]
        for kk in range(1, kw):
            y = y + ext[base + kk:base + kk + G] * w[kk:kk + 1]
        prev_ref[...] = cur[G - DN_CONV_HALO:G]
        return y * jax.nn.sigmoid(y)

    def l2n(x):
        return x * lax.rsqrt(jnp.sum(x * x, axis=-1, keepdims=True) + NORM_EPS)

    q = l2n(conv_silu(q_ref, pq_ref, cq_ref)) * (D ** -0.5)
    k = l2n(conv_silu(k_ref, pk_ref, ck_ref))
    v = conv_silu(v_ref, pv_ref, cv_ref)

    cs = sm_ref[0]
    lane = lax.broadcasted_iota(jnp.int32, (G, LANES), 1)
    row = lax.broadcasted_iota(jnp.int32, (G, LANES), 0)
    beta_all = jax.nn.sigmoid(cs)
    xg = cs + dtb_ref[...]
    softplus = jnp.maximum(xg, 0.0) + jnp.log1p(jnp.exp(-jnp.abs(xg)))
    g_all = -jnp.exp(alog_ref[...]) * softplus
    pos = row & (C - 1)
    gc_all = g_all
    shift = 1
    while shift < C:
        gc_all = gc_all + jnp.where(pos >= shift, pltpu.roll(gc_all, shift, 0), 0.0)
        shift *= 2
    beta_col = jnp.sum(jnp.where(lane == h, beta_all, 0.0), axis=1, keepdims=True)
    gc_col = jnp.sum(jnp.where(lane == h + n_heads, gc_all, 0.0), axis=1, keepdims=True)

    ri = lax.broadcasted_iota(jnp.int32, (G, G), 0)
    ci = lax.broadcasted_iota(jnp.int32, (G, G), 1)
    gc_row = jnp.sum(jnp.where(ri == ci, gc_col, 0.0), axis=0, keepdims=True)
    gl_col = jnp.sum(jnp.where(ci == (ri | (C - 1)), gc_row, 0.0), axis=1, keepdims=True)
    lc = C.bit_length() - 1
    same = (ri >> lc) == (ci >> lc)
    tri_incl = same & (ri >= ci)
    tri_strict = same & (ri > ci)
    decay = jnp.where(tri_incl, jnp.exp(jnp.where(tri_incl, gc_col - gc_row, 0.0)), 0.0)

    kb = k * beta_col
    vb = v * beta_col
    k16 = k.astype(BF16)
    kk = lax.dot_general(kb.astype(BF16), k16, NT_DIMS, preferred_element_type=F32)
    lower = jnp.where(tri_strict, kk * decay, 0.0)
    qk = lax.dot_general(q.astype(BF16), k16, NT_DIMS, preferred_element_type=F32)
    attn = jnp.where(tri_incl, qk * decay, 0.0).astype(BF16)

    def pair_mask(s):
        ls = s.bit_length() - 1
        return ((ri >> (ls + 1)) == (ci >> (ls + 1))) & (((ri >> ls) & 1) == 1) & (((ci >> ls) & 1) == 0)

    tinv = jnp.where(ri == ci, 1.0, 0.0) - jnp.where(pair_mask(1), lower, 0.0)
    s = 2
    while s < C:
        m16 = jnp.where(pair_mask(s), lower, 0.0).astype(BF16)
        t16 = tinv.astype(BF16)
        md = jnp.dot(m16, t16, preferred_element_type=F32)
        tinv = tinv - jnp.dot(t16, md.astype(BF16), preferred_element_type=F32)
        s *= 2

    egc = jnp.exp(gc_col)
    rhs = jnp.concatenate([vb, kb * egc], axis=1).astype(BF16)
    uw = jnp.dot(tinv.astype(BF16), rhs, preferred_element_type=F32)
    u = uw[:, :D]
    w = uw[:, D:]
    q_dec = q * egc
    k_dec = k * jnp.exp(gl_col - gc_col)
    k_dec_t = k_dec.T
    egl = jnp.exp(gl_col)
    ct = lax.broadcasted_iota(jnp.int32, (D, G), 1) >> lc

    vnew_ref[...] = jnp.zeros_like(vnew_ref)
    outs = []
    for c in range(G // C):
        r0 = c * C
        st = state_ref[...]
        wq = jnp.concatenate([w[r0:r0 + C], q_dec[r0:r0 + C]], axis=0).astype(BF16)
        ws_qs = jnp.dot(wq, st.astype(BF16), preferred_element_type=F32)
        v_new = u[r0:r0 + C] - ws_qs[:C]
        vnew_ref[r0:r0 + C, :] = v_new.astype(BF16)
        vn = vnew_ref[...]
        outs.append(ws_qs[C:] + jnp.dot(attn[r0:r0 + C], vn, preferred_element_type=F32))
        kt_c = jnp.where(ct == c, k_dec_t, 0.0).astype(BF16)
        e_c = jnp.concatenate([egl[r0:r0 + C]] * (D // C), axis=0)
        state_ref[...] = st * e_c + jnp.dot(kt_c, vn, preferred_element_type=F32)

    o = jnp.concatenate(outs, axis=0)
    o = o * lax.rsqrt(jnp.mean(o * o, axis=-1, keepdims=True) + NORM_EPS) * onw_ref[...]
    z = z_ref[0].astype(F32)
    o_ref[0] = (o * (z * jax.nn.sigmoid(z))).astype(o_ref.dtype)


def _deltanet(p3, small3, conv_w, a_log_row, dt_row, onorm_w, n_heads, col_q, col_k, col_v, col_z):
    b, s, _ = p3.shape
    G, D = DN_GROUP, LANES
    kw = conv_w.shape[0]
    blk = lambda off: pl.BlockSpec((1, G, D), lambda bi, hi, gi: (bi, gi, off + hi))
    cw = lambda off: pl.BlockSpec((kw, D), lambda bi, hi, gi: (0, off + hi))
    row_spec = pl.BlockSpec((1, LANES), lambda bi, hi, gi: (0, 0))
    return pl.pallas_call(
        functools.partial(_deltanet_kernel, n_heads=n_heads),
        grid=(b, n_heads, s // G),
        in_specs=[blk(col_q), blk(col_k), blk(col_v), blk(col_z),
                  pl.BlockSpec((1, G, LANES), lambda bi, hi, gi: (bi, gi, 0)),
                  cw(0), cw(n_heads), cw(2 * n_heads), row_spec, row_spec, row_spec],
        out_specs=pl.BlockSpec((1, G, D), lambda bi, hi, gi: (bi, gi, hi)),
        out_shape=jax.ShapeDtypeStruct((b, s, n_heads * D), BF16),
        scratch_shapes=[pltpu.VMEM((D, D), F32),
                        pltpu.VMEM((DN_CONV_HALO, D), F32), pltpu.VMEM((DN_CONV_HALO, D), F32),
                        pltpu.VMEM((DN_CONV_HALO, D), F32), pltpu.VMEM((G, D), BF16)],
        compiler_params=_cparams(("parallel", "parallel", "arbitrary")),
        name="gated_deltanet",
    )(p3, p3, p3, p3, small3, conv_w, conv_w, conv_w, a_log_row, dt_row, onorm_w)


def _t5_bucket_f32(dist):
    dist = jnp.maximum(dist, 0)
    max_exact = REL_BUCKETS // 2
    d = jnp.maximum(dist, 1).astype(F32)
    large = max_exact + (jnp.log(d / max_exact) / math.log(REL_MAX_DIST / max_exact)
                         * (REL_BUCKETS - max_exact)).astype(jnp.int32)
    large = jnp.minimum(large, REL_BUCKETS - 1)
    return jnp.where(dist < max_exact, dist, large)


def _n_near_tiles():
    return -(-(REL_MAX_DIST + MOBA_BLOCK - 1) // MOBA_BLOCK)


def _moba_kernel(rb_ref, q_ref, k_ref, v_ref, z_ref, o_ref, kmean_ref, bias_ref, etab_ref, qa_ref,
                 *, n_near, nbp, group):
    hh = pl.program_id(0)
    bb = pl.program_id(1)
    i = pl.program_id(2)
    BLK, D, U = MOBA_BLOCK, LANES, group
    nb = k_ref.shape[1] // BLK
    col_shift, col_dead = nbp, nbp + 16
    e_own, e_dead = nb, nb + 1

    @pl.when((bb == 0) & (i == 0))
    def _():
        qr = lax.broadcasted_iota(jnp.int32, (BLK, BLK), 0)
        kc = lax.broadcasted_iota(jnp.int32, (BLK, BLK), 1)
        for t in range(n_near):
            dist = t * BLK + qr - kc
            bucket = _t5_bucket_f32(dist)
            val = jnp.full((BLK, BLK), rb_ref[0, hh], F32)
            for bk in range(1, REL_BUCKETS):
                val = jnp.where(bucket == bk, rb_ref[bk, hh], val)
            if t == 0:
                val = jnp.where(dist >= 0, val, NEG_INF)
            bias_ref[t] = val
        bias_ref[n_near] = jnp.full((BLK, BLK), rb_ref[REL_BUCKETS - 1, hh], F32)
        lane = lax.broadcasted_iota(jnp.int32, (16, D), 1)
        for j in range(nb + 2):
            hot = (lane == col_shift) | ((lane == j) & (j < nb)) | ((lane == col_dead) & (j == e_dead))
            etab_ref[j] = jnp.where(hot, 1.0, 0.0).astype(BF16)

    @pl.when(i == 0)
    def _():
        def body(j, carry):
            kj = k_ref[0, pl.ds(pl.multiple_of(j * BLK, BLK), BLK), :].astype(F32)
            kmean_ref[pl.ds(j, 1), :] = jnp.mean(kj, axis=0, keepdims=True)
            return carry
        lax.fori_loop(0, nb, body, 0)

    q = q_ref[0]
    gate = lax.dot_general(kmean_ref[...].astype(BF16), q, NT_DIMS, preferred_element_type=F32)
    jrow = lax.broadcasted_iota(jnp.int32, gate.shape, 0)
    past = jrow < i
    sc = jnp.where(past, gate, NEG_INF)
    rank = jnp.zeros(gate.shape, F32)
    for jj in range(nb):
        rj = sc[jj:jj + 1, :]
        beats = (rj > sc) | ((rj == sc) & (jj < jrow))
        rank = rank + jnp.where(beats, 1.0, 0.0)
    selected = past & (rank < float(MOBA_TOPK))
    parts = [jnp.where(selected, 0.0, NEG_INF)]
    if nbp > nb:
        parts.append(jnp.zeros((nbp - nb, BLK), F32))
    parts += [jnp.zeros((16, BLK), F32), jnp.full((16, BLK), NEG_INF, F32)]
    if D - col_dead - 16:
        parts.append(jnp.zeros((D - col_dead - 16, BLK), F32))
    extra = jnp.concatenate(parts, axis=0).T
    lane_q = lax.broadcasted_iota(jnp.int32, (BLK, D), 1)

    def set_query_operand(shift_col):
        qa_ref[:, D:2 * D] = jnp.where(lane_q == col_shift, shift_col, extra).astype(BF16)

    qa_ref[:, 0:D] = q
    set_query_operand(jnp.zeros((BLK, 1), F32))

    def group_scores(g):
        j0 = g * U
        row0 = pl.multiple_of(j0 * BLK, U * BLK)
        e_rows, b_cols = [], []
        for u in range(U):
            j = j0 + u
            e_idx = jnp.where(j == i, e_own, jnp.where(j > i, e_dead, j))
            e_rows += [etab_ref[e_idx]] * (BLK // 16)
            b_cols.append(bias_ref[jnp.clip(i - j, 0, n_near)])
        ka = jnp.concatenate([k_ref[0, pl.ds(row0, U * BLK), :], jnp.concatenate(e_rows, axis=0)], axis=1)
        s = lax.dot_general(qa_ref[...], ka, NT_DIMS, preferred_element_type=F32)
        return s + jnp.concatenate(b_cols, axis=1), row0

    def lane_chunks(s):
        return [s[:, c * D:(c + 1) * D] for c in range(s.shape[1] // D)]

    n_groups = i // U + 1

    def max_step(g, mm):
        s, _ = group_scores(g)
        for c in lane_chunks(s):
            mm = jnp.maximum(mm, c)
        return mm

    mm = lax.fori_loop(0, n_groups, max_step, jnp.full((BLK, D), jnp.finfo(F32).min, F32))
    m = jnp.max(mm, axis=1, keepdims=True)

    set_query_operand(-m)

    def acc_step(g, carry):
        acc, lsum = carry
        s, row0 = group_scores(g)
        p = jnp.exp(s)
        for c in lane_chunks(p):
            lsum = lsum + c
        vg = v_ref[0, pl.ds(row0, U * BLK), :]
        return acc + jnp.dot(p.astype(BF16), vg, preferred_element_type=F32), lsum

    zero = jnp.zeros((BLK, D), F32)
    acc, lsum = lax.fori_loop(0, n_groups, acc_step, (zero, zero))

    out = acc / jnp.sum(lsum, axis=1, keepdims=True)
    z = z_ref[0].astype(F32)
    o_ref[0] = (out * (z * jax.nn.sigmoid(z))).astype(o_ref.dtype)


def _moba(rel_bias, p3, n_heads, col_q, col_k, col_v, col_z, group=4):
    b, s, _ = p3.shape
    BLK, D = MOBA_BLOCK, LANES
    nb = s // BLK
    nbp = -(-nb // 16) * 16
    assert nbp + 32 <= D, "selection-mask columns must fit in the spare contraction columns"
    assert nb % group == 0, "key blocks are visited in whole groups"
    n_near = _n_near_tiles()
    seq = lambda off: pl.BlockSpec((1, s, D), lambda hi, bi, i: (bi, 0, off + hi))
    blk = lambda off: pl.BlockSpec((1, BLK, D), lambda hi, bi, i: (bi, i, off + hi))
    return pl.pallas_call(
        functools.partial(_moba_kernel, n_near=n_near, nbp=nbp, group=group),
        grid=(n_heads, b, nb),
        in_specs=[pl.BlockSpec(memory_space=pltpu.SMEM), blk(col_q), seq(col_k), seq(col_v), blk(col_z)],
        out_specs=blk(0),
        out_shape=jax.ShapeDtypeStruct((b, s, n_heads * D), BF16),
        scratch_shapes=[pltpu.VMEM((nb, D), F32), pltpu.VMEM((n_near + 1, BLK, BLK), F32),
                        pltpu.VMEM((nb + 2, 16, D), BF16), pltpu.VMEM((BLK, 2 * D), BF16)],
        compiler_params=_cparams(("arbitrary", "arbitrary", "arbitrary")),
        name="moba_attention",
    )(rel_bias, p3, p3, p3, p3)


def _merge_kernel(ya_ref, yb_ref, g0_ref, g1_ref, x_ref, w0_ref, w1_ref, wo_ref, b0_ref, b1_ref, nw_ref, o_ref):
    pa = jnp.dot(ya_ref[...], w0_ref[...], preferred_element_type=F32)
    pb = jnp.dot(yb_ref[...], w1_ref[...], preferred_element_type=F32)
    g0 = jax.nn.sigmoid(g0_ref[...].astype(F32) + b0_ref[...])
    g1 = jax.nn.sigmoid(g1_ref[...].astype(F32) + b1_ref[...])
    merged = (g0 * pa + g1 * pb).astype(BF16)
    out = jnp.dot(merged, wo_ref[...], preferred_element_type=F32)
    y = out * lax.rsqrt(jnp.mean(out * out, axis=-1, keepdims=True) + NORM_EPS) * nw_ref[...]
    o_ref[...] = x_ref[...] + y


def _merge(ya, yb, p2, x2d, w0, w1, wo, b_gate, norm_w, col_g0, col_g1, tm=512):
    t, d = x2d.shape
    row = lambda c: pl.BlockSpec((tm, d), lambda i: (i, c))
    full = pl.BlockSpec((d, d), lambda i: (0, 0))
    vec = lambda c: pl.BlockSpec((1, d), lambda i: (0, c))
    return pl.pallas_call(
        _merge_kernel,
        grid=(t // tm,),
        in_specs=[row(0), row(0), row(col_g0), row(col_g1), row(0), full, full, full, vec(0), vec(1), vec(0)],
        out_specs=row(0),
        out_shape=jax.ShapeDtypeStruct((t, d), F32),
        compiler_params=_cparams(("parallel",)),
        name="merge_out",
    )(ya, yb, p2, p2, x2d, w0, w1, wo, b_gate.reshape(1, -1), b_gate.reshape(1, -1), norm_w.reshape(1, d))


def kernel(x, norm_pre_w, w_in, b_gate, conv_w, dn_a_log, dn_dt_bias, dn_onorm_w, rel_bias, w_branch, w_out,
           norm_post_w):
    b, s, d = x.shape
    n_heads = dn_a_log.shape[0]
    hd = dn_onorm_w.shape[0]
    width = n_heads * hd
    assert hd == LANES and rel_bias.shape[1] == n_heads and width == d
    assert s % MOBA_BLOCK == 0 and s % DN_GROUP == 0 and 2 * n_heads <= LANES
    t = b * s

    c_beta = 4 * width
    c_qb = c_beta + 2 * n_heads
    w16 = w_in.astype(BF16)
    w_main = jnp.concatenate([w16[:, :c_beta], w16[:, c_qb:]], axis=1)
    w_small = jnp.pad(w16[:, c_beta:c_qb], ((0, 0), (0, LANES - 2 * n_heads)))
    nblk = width // LANES
    col = lambda k: k * nblk

    x2d = x.reshape(t, d)
    hn = _rmsnorm(x2d, norm_pre_w)
    p2 = _matmul(hn, w_main, BF16, tm=1024, tn=width, name="in_proj_main", scaled_tile=4, scale=hd ** -0.5)
    small = _matmul(hn, w_small, F32, tm=1024, tn=LANES, name="in_proj_small")
    p3 = p2.reshape(b, s, -1)

    pad = (0, LANES - 2 * n_heads)
    a_log_row = jnp.pad(jnp.concatenate([jnp.zeros_like(dn_a_log), dn_a_log]), pad).reshape(1, LANES).astype(F32)
    dt_row = jnp.pad(jnp.concatenate([jnp.zeros_like(dn_dt_bias), dn_dt_bias]), pad).reshape(1, LANES).astype(F32)
    ya = _deltanet(p3, small.reshape(b, s, LANES), conv_w.astype(F32), a_log_row, dt_row,
                   dn_onorm_w.reshape(1, hd).astype(F32), n_heads, col(0), col(1), col(2), col(3))
    yb = _moba(rel_bias.astype(F32), p3, n_heads, col(4), col(5), col(6), col(7))

    out = _merge(ya.reshape(t, d), yb.reshape(t, d), p2, x2d,
                 w_branch[0].astype(BF16), w_branch[1].astype(BF16), w_out.astype(BF16),
                 b_gate.astype(F32), norm_post_w.astype(F32), 8, 9)
    return out.reshape(b, s, d)
```

```python
import functools
import math

import jax
import jax.numpy as jnp
from jax import lax
from jax.experimental import pallas as pl
from jax.experimental.pallas import tpu as pltpu

F32 = jnp.float32
BF16 = jnp.bfloat16

LANES = 128
DN_CHUNK = 64
DN_GROUP = 128
DN_CONV_HALO = 8
MOBA_BLOCK = 256
MOBA_TOPK = 3
REL_BUCKETS = 32
REL_MAX_DIST = 2048
NORM_EPS = 1e-6
NEG_INF = -1e30
VMEM_LIMIT = 56 * 1024 * 1024
NT_DIMS = (((1,), (1,)), ((), ()))


def _cparams(sem):
    return pltpu.CompilerParams(dimension_semantics=sem, vmem_limit_bytes=VMEM_LIMIT)


def _rmsnorm_kernel(x_ref, w_ref, o_ref):
    x = x_ref[...]
    y = x * lax.rsqrt(jnp.mean(x * x, axis=-1, keepdims=True) + NORM_EPS)
    o_ref[...] = (y * w_ref[...]).astype(o_ref.dtype)


def _rmsnorm(x2d, w, tm=1024):
    t, d = x2d.shape
    return pl.pallas_call(
        _rmsnorm_kernel,
        grid=(t // tm,),
        in_specs=[pl.BlockSpec((tm, d), lambda i: (i, 0)), pl.BlockSpec((1, d), lambda i: (0, 0))],
        out_specs=pl.BlockSpec((tm, d), lambda i: (i, 0)),
        out_shape=jax.ShapeDtypeStruct((t, d), BF16),
        compiler_params=_cparams(("parallel",)),
        name="rmsnorm_pre",
    )(x2d, w.reshape(1, d))


def _mm_kernel(h_ref, w_ref, o_ref, *, scaled_tile, scale):
    acc = jnp.dot(h_ref[...], w_ref[...], preferred_element_type=F32)
    if scaled_tile is not None:
        acc = acc * jnp.where(pl.program_id(1) == scaled_tile, scale, 1.0).astype(F32)
    o_ref[...] = acc.astype(o_ref.dtype)


def _matmul(h, w, out_dtype, tm, tn, name, scaled_tile=None, scale=1.0):
    t, d = h.shape
    n = w.shape[1]
    return pl.pallas_call(
        functools.partial(_mm_kernel, scaled_tile=scaled_tile, scale=scale),
        grid=(t // tm, n // tn),
        in_specs=[pl.BlockSpec((tm, d), lambda i, j: (i, 0)), pl.BlockSpec((d, tn), lambda i, j: (0, j))],
        out_specs=pl.BlockSpec((tm, tn), lambda i, j: (i, j)),
        out_shape=jax.ShapeDtypeStruct((t, n), out_dtype),
        compiler_params=_cparams(("parallel", "arbitrary")),
        name=name,
    )(h, w)


def _deltanet_kernel(q_ref, k_ref, v_ref, z_ref, sm_ref, cq_ref, ck_ref, cv_ref, alog_ref, dtb_ref, onw_ref,
                     o_ref, state_ref, halo_ref, vnew_ref, gct_ref, *, n_heads, heads_per_step):
    hg = pl.program_id(1)
    g = pl.program_id(2)
    G, C, D, NH = DN_GROUP, DN_CHUNK, LANES, heads_per_step
    lc = C.bit_length() - 1

    @pl.when(g == 0)
    def _():
        state_ref[...] = jnp.zeros_like(state_ref)
        halo_ref[...] = jnp.zeros_like(halo_ref)

    def conv_silu(x_ref, slot, w_ref):
        cur = x_ref[0].astype(F32)
        ext = jnp.concatenate([halo_ref[slot], cur], axis=0)
        w = w_ref[...]
        kw = w.shape[0]
        base = DN_CONV_HALO - (kw - 1)
        y = ext[base:base + G] * w[0:1]
        for kk in range(1, kw):
            y = y + ext[base + kk:base + kk + G] * w[kk:kk + 1]
        halo_ref[slot] = cur[G - DN_CONV_HALO:G]
        return y * jax.nn.sigmoid(y)

    def l2n(x):
        return x * lax.rsqrt(jnp.sum(x * x, axis=-1, keepdims=True) + NORM_EPS)

    qf = conv_silu(q_ref, 0, cq_ref)
    kf = conv_silu(k_ref, 1, ck_ref)
    vf = conv_silu(v_ref, 2, cv_ref)

    cs = sm_ref[0]
    lane = lax.broadcasted_iota(jnp.int32, (G, LANES), 1)
    row = lax.broadcasted_iota(jnp.int32, (G, LANES), 0)
    beta_all = jax.nn.sigmoid(cs)
    xg = cs + dtb_ref[...]
    softplus = jnp.maximum(xg, 0.0) + jnp.log1p(jnp.exp(-jnp.abs(xg)))
    g_all = -jnp.exp(alog_ref[...]) * softplus
    pos = row & (C - 1)
    gc_all = g_all
    shift = 1
    while shift < C:
        gc_all = gc_all + jnp.where(pos >= shift, pltpu.roll(gc_all, shift, 0), 0.0)
        shift *= 2
    gl_all = jnp.concatenate([jnp.broadcast_to(gc_all[c * C + C - 1:c * C + C], (C, LANES))
                              for c in range(G // C)], axis=0)
    gct_ref[...] = gc_all.T

    ri = lax.broadcasted_iota(jnp.int32, (G, G), 0)
    ci = lax.broadcasted_iota(jnp.int32, (G, G), 1)
    same = (ri >> lc) == (ci >> lc)
    tri_incl = same & (ri >= ci)
    tri_strict = same & (ri > ci)
    eye = jnp.where(ri == ci, 1.0, 0.0)
    ct = lax.broadcasted_iota(jnp.int32, (D, G), 1) >> lc

    def pair_mask(s):
        ls = s.bit_length() - 1
        return ((ri >> (ls + 1)) == (ci >> (ls + 1))) & (((ri >> ls) & 1) == 1) & (((ci >> ls) & 1) == 0)

    pair_masks = {}
    s = 1
    while s < C:
        pair_masks[s] = pair_mask(s)
        s *= 2

    heads = range(NH)
    sls = [slice(hl * D, (hl + 1) * D) for hl in heads]
    hidx = [hg * NH + hl for hl in heads]

    def col_of(x, lane_idx):
        return jnp.sum(jnp.where(lane == lane_idx, x, 0.0), axis=1, keepdims=True)

    q = [l2n(qf[:, sl]) * (D ** -0.5) for sl in sls]
    k = [l2n(kf[:, sl]) for sl in sls]
    beta_col = [col_of(beta_all, h) for h in hidx]
    gc_col = [col_of(gc_all, h + n_heads) for h in hidx]
    gl_col = [col_of(gl_all, h + n_heads) for h in hidx]
    gc_row = [gct_ref[pl.ds(h + n_heads, 1), :] for h in hidx]
    decay = [jnp.where(tri_incl, jnp.exp(jnp.where(tri_incl, gc_col[a] - gc_row[a], 0.0)), 0.0) for a in heads]
    kb = [k[a] * beta_col[a] for a in heads]
    vb = [vf[:, sls[a]] * beta_col[a] for a in heads]
    k16 = [x.astype(BF16) for x in k]
    kk = [lax.dot_general(kb[a].astype(BF16), k16[a], NT_DIMS, preferred_element_type=F32) for a in heads]
    qk = [lax.dot_general(q[a].astype(BF16), k16[a], NT_DIMS, preferred_element_type=F32) for a in heads]
    lower = [jnp.where(tri_strict, kk[a] * decay[a], 0.0) for a in heads]
    attn = [jnp.where(tri_incl, qk[a] * decay[a], 0.0).astype(BF16) for a in heads]

    tinv = [eye - jnp.where(pair_masks[1], lower[a], 0.0) for a in heads]
    s = 2
    while s < C:
        t16 = [x.astype(BF16) for x in tinv]
        md = [jnp.dot(jnp.where(pair_masks[s], lower[a], 0.0).astype(BF16), t16[a], preferred_element_type=F32)
              for a in heads]
        tinv = [tinv[a] - jnp.dot(t16[a], md[a].astype(BF16), preferred_element_type=F32) for a in heads]
        s *= 2

    egc = [jnp.exp(x) for x in gc_col]
    uw = [jnp.dot(tinv[a].astype(BF16), jnp.concatenate([vb[a], kb[a] * egc[a]], axis=1).astype(BF16),
                  preferred_element_type=F32) for a in heads]
    q_dec = [q[a] * egc[a] for a in heads]
    k_dec_t = [(k[a] * jnp.exp(gl_col[a] - gc_col[a])).T for a in heads]
    egl = [jnp.exp(x) for x in gl_col]

    vnew_ref[...] = jnp.zeros_like(vnew_ref)
    outs = [[] for _ in heads]
    for c in range(G // C):
        r0 = c * C
        st = [state_ref[a] for a in heads]
        ws_qs = [jnp.dot(jnp.concatenate([uw[a][r0:r0 + C, D:], q_dec[a][r0:r0 + C]], axis=0).astype(BF16),
                         st[a].astype(BF16), preferred_element_type=F32) for a in heads]
        for a in heads:
            vnew_ref[a, r0:r0 + C, :] = (uw[a][r0:r0 + C, :D] - ws_qs[a][:C]).astype(BF16)
        vn = [vnew_ref[a] for a in heads]
        for a in heads:
            outs[a].append(ws_qs[a][C:] + jnp.dot(attn[a][r0:r0 + C], vn[a], preferred_element_type=F32))
        for a in heads:
            kt_c = jnp.where(ct == c, k_dec_t[a], 0.0).astype(BF16)
            e_c = jnp.concatenate([egl[a][r0:r0 + C]] * (D // C), axis=0)
            state_ref[a] = st[a] * e_c + jnp.dot(kt_c, vn[a], preferred_element_type=F32)

    for a in heads:
        o = jnp.concatenate(outs[a], axis=0)
        o = o * lax.rsqrt(jnp.mean(o * o, axis=-1, keepdims=True) + NORM_EPS) * onw_ref[...]
        z = z_ref[0, :, sls[a]].astype(F32)
        o_ref[0, :, sls[a]] = (o * (z * jax.nn.sigmoid(z))).astype(o_ref.dtype)


def _deltanet(p3, small3, conv_w, a_log_row, dt_row, onorm_w, n_heads, col_q, col_k, col_v, col_z,
              heads_per_step=8):
    b, s, _ = p3.shape
    G, D, NH = DN_GROUP, LANES, heads_per_step
    assert n_heads % NH == 0 and all(c % NH == 0 for c in (col_q, col_k, col_v, col_z))
    kw = conv_w.shape[0]
    blk = lambda off: pl.BlockSpec((1, G, NH * D), lambda bi, hi, gi: (bi, gi, off // NH + hi))
    cw = lambda off: pl.BlockSpec((kw, NH * D), lambda bi, hi, gi: (0, off // NH + hi))
    row_spec = pl.BlockSpec((1, LANES), lambda bi, hi, gi: (0, 0))
    return pl.pallas_call(
        functools.partial(_deltanet_kernel, n_heads=n_heads, heads_per_step=NH),
        grid=(b, n_heads // NH, s // G),
        in_specs=[blk(col_q), blk(col_k), blk(col_v), blk(col_z),
                  pl.BlockSpec((1, G, LANES), lambda bi, hi, gi: (bi, gi, 0)),
                  cw(0), cw(n_heads), cw(2 * n_heads), row_spec, row_spec, row_spec],
        out_specs=pl.BlockSpec((1, G, NH * D), lambda bi, hi, gi: (bi, gi, hi)),
        out_shape=jax.ShapeDtypeStruct((b, s, n_heads * D), BF16),
        scratch_shapes=[pltpu.VMEM((NH, D, D), F32), pltpu.VMEM((3, DN_CONV_HALO, NH * D), F32),
                        pltpu.VMEM((NH, G, D), BF16), pltpu.VMEM((LANES, G), F32)],
        compiler_params=_cparams(("parallel", "parallel", "arbitrary")),
        name="gated_deltanet",
    )(p3, p3, p3, p3, small3, conv_w, conv_w, conv_w, a_log_row, dt_row, onorm_w)


def _t5_bucket_f32(dist):
    dist = jnp.maximum(dist, 0)
    max_exact = REL_BUCKETS // 2
    d = jnp.maximum(dist, 1).astype(F32)
    large = max_exact + (jnp.log(d / max_exact) / math.log(REL_MAX_DIST / max_exact)
                         * (REL_BUCKETS - max_exact)).astype(jnp.int32)
    large = jnp.minimum(large, REL_BUCKETS - 1)
    return jnp.where(dist < max_exact, dist, large)


def _n_near_tiles():
    return -(-(REL_MAX_DIST + MOBA_BLOCK - 1) // MOBA_BLOCK)


def _moba_kernel(rb_ref, q_ref, k_ref, v_ref, z_ref, o_ref, kmean_ref, bias_ref, etab_ref, qa_ref,
                 *, n_near, nbp, group):
    hh = pl.program_id(0)
    bb = pl.program_id(1)
    i = pl.program_id(2)
    BLK, D, U = MOBA_BLOCK, LANES, group
    nb = k_ref.shape[1] // BLK
    col_shift, col_dead = nbp, nbp + 16
    e_own, e_dead = nb, nb + 1

    @pl.when((bb == 0) & (i == 0))
    def _():
        qr = lax.broadcasted_iota(jnp.int32, (BLK, BLK), 0)
        kc = lax.broadcasted_iota(jnp.int32, (BLK, BLK), 1)
        for t in range(n_near):
            dist = t * BLK + qr - kc
            bucket = _t5_bucket_f32(dist)
            val = jnp.full((BLK, BLK), rb_ref[0, hh], F32)
            for bk in range(1, REL_BUCKETS):
                val = jnp.where(bucket == bk, rb_ref[bk, hh], val)
            if t == 0:
                val = jnp.where(dist >= 0, val, NEG_INF)
            bias_ref[t] = val
        bias_ref[n_near] = jnp.full((BLK, BLK), rb_ref[REL_BUCKETS - 1, hh], F32)
        lane = lax.broadcasted_iota(jnp.int32, (16, D), 1)
        for j in range(nb + 2):
            hot = (lane == col_shift) | ((lane == j) & (j < nb)) | ((lane == col_dead) & (j == e_dead))
            etab_ref[j] = jnp.where(hot, 1.0, 0.0).astype(BF16)

    @pl.when(i == 0)
    def _():
        def body(j, carry):
            kj = k_ref[0, pl.ds(pl.multiple_of(j * BLK, BLK), BLK), :].astype(F32)
            kmean_ref[pl.ds(j, 1), :] = jnp.mean(kj, axis=0, keepdims=True)
            return carry
        lax.fori_loop(0, nb, body, 0)

    q = q_ref[0]
    gate = lax.dot_general(kmean_ref[...].astype(BF16), q, NT_DIMS, preferred_element_type=F32)
    jrow = lax.broadcasted_iota(jnp.int32, gate.shape, 0)
    past = jrow < i
    sc = jnp.where(past, gate, NEG_INF)
    rank = jnp.zeros(gate.shape, F32)
    for jj in range(nb):
        rj = sc[jj:jj + 1, :]
        beats = (rj > sc) | ((rj == sc) & (jj < jrow))
        rank = rank + jnp.where(beats, 1.0, 0.0)
    selected = past & (rank < float(MOBA_TOPK))
    parts = [jnp.where(selected, 0.0, NEG_INF)]
    if nbp > nb:
        parts.append(jnp.zeros((nbp - nb, BLK), F32))
    parts += [jnp.zeros((16, BLK), F32), jnp.full((16, BLK), NEG_INF, F32)]
    if D - col_dead - 16:
        parts.append(jnp.zeros((D - col_dead - 16, BLK), F32))
    extra = jnp.concatenate(parts, axis=0).T
    lane_q = lax.broadcasted_iota(jnp.int32, (BLK, D), 1)

    def set_query_operand(shift_col):
        qa_ref[:, D:2 * D] = jnp.where(lane_q == col_shift, shift_col, extra).astype(BF16)

    qa_ref[:, 0:D] = q
    set_query_operand(jnp.zeros((BLK, 1), F32))

    def group_scores(g):
        j0 = g * U
        row0 = pl.multiple_of(j0 * BLK, U * BLK)
        e_rows, b_cols = [], []
        for u in range(U):
            j = j0 + u
            e_idx = jnp.where(j == i, e_own, jnp.where(j > i, e_dead, j))
            e_rows += [etab_ref[e_idx]] * (BLK // 16)
            b_cols.append(bias_ref[jnp.clip(i - j, 0, n_near)])
        ka = jnp.concatenate([k_ref[0, pl.ds(row0, U * BLK), :], jnp.concatenate(e_rows, axis=0)], axis=1)
        s = lax.dot_general(qa_ref[...], ka, NT_DIMS, preferred_element_type=F32)
        return s + jnp.concatenate(b_cols, axis=1), row0

    def lane_chunks(s):
        return [s[:, c * D:(c + 1) * D] for c in range(s.shape[1] // D)]

    n_groups = i // U + 1

    def max_step(g, mm):
        s, _ = group_scores(g)
        for c in lane_chunks(s):
            mm = jnp.maximum(mm, c)
        return mm

    mm = lax.fori_loop(0, n_groups, max_step, jnp.full((BLK, D), jnp.finfo(F32).min, F32))
    m = jnp.max(mm, axis=1, keepdims=True)

    set_query_operand(-m)

    def acc_step(g, carry):
        acc, lsum = carry
        s, row0 = group_scores(g)
        p = jnp.exp(s)
        for c in lane_chunks(p):
            lsum = lsum + c
        vg = v_ref[0, pl.ds(row0, U * BLK), :]
        return acc + jnp.dot(p.astype(BF16), vg, preferred_element_type=F32), lsum

    zero = jnp.zeros((BLK, D), F32)
    acc, lsum = lax.fori_loop(0, n_groups, acc_step, (zero, zero))

    out = acc / jnp.sum(lsum, axis=1, keepdims=True)
    z = z_ref[0].astype(F32)
    o_ref[0] = (out * (z * jax.nn.sigmoid(z))).astype(o_ref.dtype)


def _moba(rel_bias, p3, n_heads, col_q, col_k, col_v, col_z, group=4):
    b, s, _ = p3.shape
    BLK, D = MOBA_BLOCK, LANES
    nb = s // BLK
    nbp = -(-nb // 16) * 16
    assert nbp + 32 <= D, "selection-mask columns must fit in the spare contraction columns"
    assert nb % group == 0, "key blocks are visited in whole groups"
    n_near = _n_near_tiles()
    seq = lambda off: pl.BlockSpec((1, s, D), lambda hi, bi, i: (bi, 0, off + hi))
    blk = lambda off: pl.BlockSpec((1, BLK, D), lambda hi, bi, i: (bi, i, off + hi))
    return pl.pallas_call(
        functools.partial(_moba_kernel, n_near=n_near, nbp=nbp, group=group),
        grid=(n_heads, b, nb),
        in_specs=[pl.BlockSpec(memory_space=pltpu.SMEM), blk(col_q), seq(col_k), seq(col_v), blk(col_z)],
        out_specs=blk(0),
        out_shape=jax.ShapeDtypeStruct((b, s, n_heads * D), BF16),
        scratch_shapes=[pltpu.VMEM((nb, D), F32), pltpu.VMEM((n_near + 1, BLK, BLK), F32),
                        pltpu.VMEM((nb + 2, 16, D), BF16), pltpu.VMEM((BLK, 2 * D), BF16)],
        compiler_params=_cparams(("arbitrary", "arbitrary", "arbitrary")),
        name="moba_attention",
    )(rel_bias, p3, p3, p3, p3)


def _merge_kernel(ya_ref, yb_ref, g0_ref, g1_ref, x_ref, w0_ref, w1_ref, wo_ref, b0_ref, b1_ref, nw_ref, o_ref):
    pa = jnp.dot(ya_ref[...], w0_ref[...], preferred_element_type=F32)
    pb = jnp.dot(yb_ref[...], w1_ref[...], preferred_element_type=F32)
    g0 = jax.nn.sigmoid(g0_ref[...].astype(F32) + b0_ref[...])
    g1 = jax.nn.sigmoid(g1_ref[...].astype(F32) + b1_ref[...])
    merged = (g0 * pa + g1 * pb).astype(BF16)
    out = jnp.dot(merged, wo_ref[...], preferred_element_type=F32)
    y = out * lax.rsqrt(jnp.mean(out * out, axis=-1, keepdims=True) + NORM_EPS) * nw_ref[...]
    o_ref[...] = x_ref[...] + y


def _merge(ya, yb, p2, x2d, w0, w1, wo, b_gate, norm_w, col_g0, col_g1, tm=512):
    t, d = x2d.shape
    row = lambda c: pl.BlockSpec((tm, d), lambda i: (i, c))
    full = pl.BlockSpec((d, d), lambda i: (0, 0))
    vec = lambda c: pl.BlockSpec((1, d), lambda i: (0, c))
    return pl.pallas_call(
        _merge_kernel,
        grid=(t // tm,),
        in_specs=[row(0), row(0), row(col_g0), row(col_g1), row(0), full, full, full, vec(0), vec(1), vec(0)],
        out_specs=row(0),
        out_shape=jax.ShapeDtypeStruct((t, d), F32),
        compiler_params=_cparams(("parallel",)),
        name="merge_out",
    )(ya, yb, p2, p2, x2d, w0, w1, wo, b_gate.reshape(1, -1), b_gate.reshape(1, -1), norm_w.reshape(1, d))


def kernel(x, norm_pre_w, w_in, b_gate, conv_w, dn_a_log, dn_dt_bias, dn_onorm_w, rel_bias, w_branch, w_out,
           norm_post_w):
    b, s, d = x.shape
    n_heads = dn_a_log.shape[0]
    hd = dn_onorm_w.shape[0]
    width = n_heads * hd
    assert hd == LANES and rel_bias.shape[1] == n_heads and width == d
    assert s % MOBA_BLOCK == 0 and s % DN_GROUP == 0 and 2 * n_heads <= LANES
    t = b * s

    c_beta = 4 * width
    c_qb = c_beta + 2 * n_heads
    w16 = w_in.astype(BF16)
    w_main = jnp.concatenate([w16[:, :c_beta], w16[:, c_qb:]], axis=1)
    w_small = jnp.pad(w16[:, c_beta:c_qb], ((0, 0), (0, LANES - 2 * n_heads)))
    nblk = width // LANES
    col = lambda k: k * nblk

    x2d = x.reshape(t, d)
    hn = _rmsnorm(x2d, norm_pre_w)
    p2 = _matmul(hn, w_main, BF16, tm=1024, tn=width, name="in_proj_main", scaled_tile=4, scale=hd ** -0.5)
    small = _matmul(hn, w_small, F32, tm=1024, tn=LANES, name="in_proj_small")
    p3 = p2.reshape(b, s, -1)

    pad = (0, LANES - 2 * n_heads)
    a_log_row = jnp.pad(jnp.concatenate([jnp.zeros_like(dn_a_log), dn_a_log]), pad).reshape(1, LANES).astype(F32)
    dt_row = jnp.pad(jnp.concatenate([jnp.zeros_like(dn_dt_bias), dn_dt_bias]), pad).reshape(1, LANES).astype(F32)
    ya = _deltanet(p3, small.reshape(b, s, LANES), conv_w.astype(F32), a_log_row, dt_row,
                   dn_onorm_w.reshape(1, hd).astype(F32), n_heads, col(0), col(1), col(2), col(3))
    yb = _moba(rel_bias.astype(F32), p3, n_heads, col(4), col(5), col(6), col(7))

    out = _merge(ya.reshape(t, d), yb.reshape(t, d), p2, x2d,
                 w_branch[0].astype(BF16), w_branch[1].astype(BF16), w_out.astype(BF16),
                 b_gate.astype(F32), norm_post_w.astype(F32), 8, 9)
    return out.reshape(b, s, d)
```

```python
import functools
import math

import jax
import jax.numpy as jnp
from jax import lax
from jax.experimental import pallas as pl
from jax.experimental.pallas import tpu as pltpu

F32 = jnp.float32
BF16 = jnp.bfloat16

LANES = 128
DN_CHUNK = 64
DN_GROUP = 128
DN_CONV_HALO = 8
MOBA_BLOCK = 256
MOBA_TOPK = 3
MOBA_SHIFT_SLACK = 40.0
REL_BUCKETS = 32
REL_MAX_DIST = 2048
NORM_EPS = 1e-6
NEG_INF = -1e30
VMEM_LIMIT = 56 * 1024 * 1024
NT_DIMS = (((1,), (1,)), ((), ()))


def _cparams(sem):
    return pltpu.CompilerParams(dimension_semantics=sem, vmem_limit_bytes=VMEM_LIMIT)


def _rmsnorm_kernel(x_ref, w_ref, o_ref):
    x = x_ref[...]
    y = x * lax.rsqrt(jnp.mean(x * x, axis=-1, keepdims=True) + NORM_EPS)
    o_ref[...] = (y * w_ref[...]).astype(o_ref.dtype)


def _rmsnorm(x2d, w, tm=1024):
    t, d = x2d.shape
    return pl.pallas_call(
        _rmsnorm_kernel,
        grid=(t // tm,),
        in_specs=[pl.BlockSpec((tm, d), lambda i: (i, 0)), pl.BlockSpec((1, d), lambda i: (0, 0))],
        out_specs=pl.BlockSpec((tm, d), lambda i: (i, 0)),
        out_shape=jax.ShapeDtypeStruct((t, d), BF16),
        compiler_params=_cparams(("parallel",)),
        name="rmsnorm_pre",
    )(x2d, w.reshape(1, d))


def _mm_kernel(h_ref, w_ref, o_ref, *, scaled_tile, scale):
    acc = jnp.dot(h_ref[...], w_ref[...], preferred_element_type=F32)
    if scaled_tile is not None:
        acc = acc * jnp.where(pl.program_id(1) == scaled_tile, scale, 1.0).astype(F32)
    o_ref[...] = acc.astype(o_ref.dtype)


def _matmul(h, w, out_dtype, tm, tn, name, scaled_tile=None, scale=1.0):
    t, d = h.shape
    n = w.shape[1]
    return pl.pallas_call(
        functools.partial(_mm_kernel, scaled_tile=scaled_tile, scale=scale),
        grid=(t // tm, n // tn),
        in_specs=[pl.BlockSpec((tm, d), lambda i, j: (i, 0)), pl.BlockSpec((d, tn), lambda i, j: (0, j))],
        out_specs=pl.BlockSpec((tm, tn), lambda i, j: (i, j)),
        out_shape=jax.ShapeDtypeStruct((t, n), out_dtype),
        compiler_params=_cparams(("parallel", "arbitrary")),
        name=name,
    )(h, w)


def _deltanet_kernel(q_ref, k_ref, v_ref, z_ref, sm_ref, cq_ref, ck_ref, cv_ref, alog_ref, dtb_ref, onw_ref,
                     o_ref, state_ref, halo_ref, vnew_ref, gct_ref, *, n_heads, heads_per_step):
    hg = pl.program_id(1)
    g = pl.program_id(2)
    G, C, D, NH = DN_GROUP, DN_CHUNK, LANES, heads_per_step
    lc = C.bit_length() - 1

    @pl.when(g == 0)
    def _():
        state_ref[...] = jnp.zeros_like(state_ref)
        halo_ref[...] = jnp.zeros_like(halo_ref)

    def conv_silu(x_ref, slot, w_ref):
        cur = x_ref[0].astype(F32)
        ext = jnp.concatenate([halo_ref[slot], cur], axis=0)
        w = w_ref[...]
        kw = w.shape[0]
        base = DN_CONV_HALO - (kw - 1)
        y = ext[base:base + G] * w[0:1]
        for kk in range(1, kw):
            y = y + ext[base + kk:base + kk + G] * w[kk:kk + 1]
        halo_ref[slot] = cur[G - DN_CONV_HALO:G]
        return y * jax.nn.sigmoid(y)

    def l2n(x):
        return x * lax.rsqrt(jnp.sum(x * x, axis=-1, keepdims=True) + NORM_EPS)

    qf = conv_silu(q_ref, 0, cq_ref)
    kf = conv_silu(k_ref, 1, ck_ref)
    vf = conv_silu(v_ref, 2, cv_ref)

    cs = sm_ref[0]
    lane = lax.broadcasted_iota(jnp.int32, (G, LANES), 1)
    row = lax.broadcasted_iota(jnp.int32, (G, LANES), 0)
    beta_all = jax.nn.sigmoid(cs)
    xg = cs + dtb_ref[...]
    softplus = jnp.maximum(xg, 0.0) + jnp.log1p(jnp.exp(-jnp.abs(xg)))
    g_all = -jnp.exp(alog_ref[...]) * softplus
    pos = row & (C - 1)
    gc_all = g_all
    shift = 1
    while shift < C:
        gc_all = gc_all + jnp.where(pos >= shift, pltpu.roll(gc_all, shift, 0), 0.0)
        shift *= 2
    gl_all = jnp.concatenate([jnp.broadcast_to(gc_all[c * C + C - 1:c * C + C], (C, LANES))
                              for c in range(G // C)], axis=0)
    gct_ref[...] = gc_all.T

    ri = lax.broadcasted_iota(jnp.int32, (G, G), 0)
    ci = lax.broadcasted_iota(jnp.int32, (G, G), 1)
    same = (ri >> lc) == (ci >> lc)
    tri_incl = same & (ri >= ci)
    tri_strict = same & (ri > ci)
    eye = jnp.where(ri == ci, 1.0, 0.0)
    ct = lax.broadcasted_iota(jnp.int32, (D, G), 1) >> lc

    def pair_mask(s):
        ls = s.bit_length() - 1
        return ((ri >> (ls + 1)) == (ci >> (ls + 1))) & (((ri >> ls) & 1) == 1) & (((ci >> ls) & 1) == 0)

    pair_masks = {}
    s = 1
    while s < C:
        pair_masks[s] = pair_mask(s)
        s *= 2

    heads = range(NH)
    sls = [slice(hl * D, (hl + 1) * D) for hl in heads]
    hidx = [hg * NH + hl for hl in heads]

    def col_of(x, lane_idx):
        return jnp.sum(jnp.where(lane == lane_idx, x, 0.0), axis=1, keepdims=True)

    q = [l2n(qf[:, sl]) * (D ** -0.5) for sl in sls]
    k = [l2n(kf[:, sl]) for sl in sls]
    beta_col = [col_of(beta_all, h) for h in hidx]
    gc_col = [col_of(gc_all, h + n_heads) for h in hidx]
    gl_col = [col_of(gl_all, h + n_heads) for h in hidx]
    gc_row = [gct_ref[pl.ds(h + n_heads, 1), :] for h in hidx]
    decay = [jnp.where(tri_incl, jnp.exp(jnp.where(tri_incl, gc_col[a] - gc_row[a], 0.0)), 0.0) for a in heads]
    kb = [k[a] * beta_col[a] for a in heads]
    vb = [vf[:, sls[a]] * beta_col[a] for a in heads]
    k16 = [x.astype(BF16) for x in k]
    kk = [lax.dot_general(kb[a].astype(BF16), k16[a], NT_DIMS, preferred_element_type=F32) for a in heads]
    qk = [lax.dot_general(q[a].astype(BF16), k16[a], NT_DIMS, preferred_element_type=F32) for a in heads]
    lower = [jnp.where(tri_strict, kk[a] * decay[a], 0.0) for a in heads]
    attn = [jnp.where(tri_incl, qk[a] * decay[a], 0.0).astype(BF16) for a in heads]

    tinv = [eye - jnp.where(pair_masks[1], lower[a], 0.0) for a in heads]
    s = 2
    while s < C:
        t16 = [x.astype(BF16) for x in tinv]
        md = [jnp.dot(jnp.where(pair_masks[s], lower[a], 0.0).astype(BF16), t16[a], preferred_element_type=F32)
              for a in heads]
        tinv = [tinv[a] - jnp.dot(t16[a], md[a].astype(BF16), preferred_element_type=F32) for a in heads]
        s *= 2

    egc = [jnp.exp(x) for x in gc_col]
    uw = [jnp.dot(tinv[a].astype(BF16), jnp.concatenate([vb[a], kb[a] * egc[a]], axis=1).astype(BF16),
                  preferred_element_type=F32) for a in heads]
    q_dec = [q[a] * egc[a] for a in heads]
    k_dec_t = [(k[a] * jnp.exp(gl_col[a] - gc_col[a])).T for a in heads]
    egl = [jnp.exp(x) for x in gl_col]

    vnew_ref[...] = jnp.zeros_like(vnew_ref)
    outs = [[] for _ in heads]
    for c in range(G // C):
        r0 = c * C
        st = [state_ref[a] for a in heads]
        ws_qs = [jnp.dot(jnp.concatenate([uw[a][r0:r0 + C, D:], q_dec[a][r0:r0 + C]], axis=0).astype(BF16),
                         st[a].astype(BF16), preferred_element_type=F32) for a in heads]
        for a in heads:
            vnew_ref[a, r0:r0 + C, :] = (uw[a][r0:r0 + C, :D] - ws_qs[a][:C]).astype(BF16)
        vn = [vnew_ref[a] for a in heads]
        for a in heads:
            outs[a].append(ws_qs[a][C:] + jnp.dot(attn[a][r0:r0 + C], vn[a], preferred_element_type=F32))
        for a in heads:
            kt_c = jnp.where(ct == c, k_dec_t[a], 0.0).astype(BF16)
            e_c = jnp.concatenate([egl[a][r0:r0 + C]] * (D // C), axis=0)
            state_ref[a] = st[a] * e_c + jnp.dot(kt_c, vn[a], preferred_element_type=F32)

    for a in heads:
        o = jnp.concatenate(outs[a], axis=0)
        o = o * lax.rsqrt(jnp.mean(o * o, axis=-1, keepdims=True) + NORM_EPS) * onw_ref[...]
        z = z_ref[0, :, sls[a]].astype(F32)
        o_ref[0, :, sls[a]] = (o * (z * jax.nn.sigmoid(z))).astype(o_ref.dtype)


def _deltanet(p3, small3, conv_w, a_log_row, dt_row, onorm_w, n_heads, col_q, col_k, col_v, col_z,
              heads_per_step=8):
    b, s, _ = p3.shape
    G, D, NH = DN_GROUP, LANES, heads_per_step
    assert n_heads % NH == 0 and all(c % NH == 0 for c in (col_q, col_k, col_v, col_z))
    kw = conv_w.shape[0]
    blk = lambda off: pl.BlockSpec((1, G, NH * D), lambda bi, hi, gi: (bi, gi, off // NH + hi))
    cw = lambda off: pl.BlockSpec((kw, NH * D), lambda bi, hi, gi: (0, off // NH + hi))
    row_spec = pl.BlockSpec((1, LANES), lambda bi, hi, gi: (0, 0))
    return pl.pallas_call(
        functools.partial(_deltanet_kernel, n_heads=n_heads, heads_per_step=NH),
        grid=(b, n_heads // NH, s // G),
        in_specs=[blk(col_q), blk(col_k), blk(col_v), blk(col_z),
                  pl.BlockSpec((1, G, LANES), lambda bi, hi, gi: (bi, gi, 0)),
                  cw(0), cw(n_heads), cw(2 * n_heads), row_spec, row_spec, row_spec],
        out_specs=pl.BlockSpec((1, G, NH * D), lambda bi, hi, gi: (bi, gi, hi)),
        out_shape=jax.ShapeDtypeStruct((b, s, n_heads * D), BF16),
        scratch_shapes=[pltpu.VMEM((NH, D, D), F32), pltpu.VMEM((3, DN_CONV_HALO, NH * D), F32),
                        pltpu.VMEM((NH, G, D), BF16), pltpu.VMEM((LANES, G), F32)],
        compiler_params=_cparams(("parallel", "parallel", "arbitrary")),
        name="gated_deltanet",
    )(p3, p3, p3, p3, small3, conv_w, conv_w, conv_w, a_log_row, dt_row, onorm_w)


def _t5_bucket_f32(dist):
    dist = jnp.maximum(dist, 0)
    max_exact = REL_BUCKETS // 2
    d = jnp.maximum(dist, 1).astype(F32)
    large = max_exact + (jnp.log(d / max_exact) / math.log(REL_MAX_DIST / max_exact)
                         * (REL_BUCKETS - max_exact)).astype(jnp.int32)
    large = jnp.minimum(large, REL_BUCKETS - 1)
    return jnp.where(dist < max_exact, dist, large)


def _n_near_tiles():
    return -(-(REL_MAX_DIST + MOBA_BLOCK - 1) // MOBA_BLOCK)


def _moba_kernel(rb_ref, q_ref, k_ref, v_ref, z_ref, o_ref, kmean_ref, bias_ref, etab_ref, qa_ref, k2max_ref,
                 *, n_near, nbp, group):
    hh = pl.program_id(0)
    bb = pl.program_id(1)
    i = pl.program_id(2)
    BLK, D, U = MOBA_BLOCK, LANES, group
    nb = k_ref.shape[1] // BLK
    col_shift, col_dead = nbp, nbp + 16
    e_own, e_dead = nb, nb + 1

    @pl.when((bb == 0) & (i == 0))
    def _():
        qr = lax.broadcasted_iota(jnp.int32, (BLK, BLK), 0)
        kc = lax.broadcasted_iota(jnp.int32, (BLK, BLK), 1)
        for t in range(n_near):
            dist = t * BLK + qr - kc
            bucket = _t5_bucket_f32(dist)
            val = jnp.full((BLK, BLK), rb_ref[0, hh], F32)
            for bk in range(1, REL_BUCKETS):
                val = jnp.where(bucket == bk, rb_ref[bk, hh], val)
            if t == 0:
                val = jnp.where(dist >= 0, val, NEG_INF)
            bias_ref[t] = val
        bias_ref[n_near] = jnp.full((BLK, BLK), rb_ref[REL_BUCKETS - 1, hh], F32)
        lane = lax.broadcasted_iota(jnp.int32, (16, D), 1)
        for j in range(nb + 2):
            hot = (lane == col_shift) | ((lane == j) & (j < nb)) | ((lane == col_dead) & (j == e_dead))
            etab_ref[j] = jnp.where(hot, 1.0, 0.0).astype(BF16)

    @pl.when(i == 0)
    def _():
        def body(j, k2max):
            kj = k_ref[0, pl.ds(pl.multiple_of(j * BLK, BLK), BLK), :].astype(F32)
            kmean_ref[pl.ds(j, 1), :] = jnp.mean(kj, axis=0, keepdims=True)
            return jnp.maximum(k2max, jnp.max(jnp.sum(kj * kj, axis=1, keepdims=True)))
        k2max_ref[0] = lax.fori_loop(0, nb, body, jnp.float32(0.0))

    q = q_ref[0]
    gate = lax.dot_general(kmean_ref[...].astype(BF16), q, NT_DIMS, preferred_element_type=F32)
    jrow = lax.broadcasted_iota(jnp.int32, gate.shape, 0)
    jrow_f = jrow.astype(F32)
    past = jrow < i
    sc = jnp.where(past, gate, NEG_INF)
    selected = jnp.zeros(gate.shape, jnp.bool_)
    for _ in range(MOBA_TOPK):
        top = jnp.max(sc, axis=0, keepdims=True)
        first = jnp.min(jnp.where(sc == top, jrow_f, float(nb)), axis=0, keepdims=True)
        pick = jrow_f == first
        selected = selected | (pick & past)
        sc = jnp.where(pick, jnp.finfo(F32).min, sc)
    parts = [jnp.where(selected, 0.0, NEG_INF)]
    if nbp > nb:
        parts.append(jnp.zeros((nbp - nb, BLK), F32))
    parts += [jnp.zeros((16, BLK), F32), jnp.full((16, BLK), NEG_INF, F32)]
    if D - col_dead - 16:
        parts.append(jnp.zeros((D - col_dead - 16, BLK), F32))
    extra = jnp.concatenate(parts, axis=0).T
    lane_q = lax.broadcasted_iota(jnp.int32, (BLK, D), 1)

    def set_query_operand(shift_col):
        qa_ref[:, D:2 * D] = jnp.where(lane_q == col_shift, shift_col, extra).astype(BF16)

    qa_ref[:, 0:D] = q
    set_query_operand(jnp.zeros((BLK, 1), F32))

    def group_scores(g):
        j0 = g * U
        row0 = pl.multiple_of(j0 * BLK, U * BLK)
        e_rows, b_cols = [], []
        for u in range(U):
            j = j0 + u
            e_idx = jnp.where(j == i, e_own, jnp.where(j > i, e_dead, j))
            e_rows += [etab_ref[e_idx]] * (BLK // 16)
            b_cols.append(bias_ref[jnp.clip(i - j, 0, n_near)])
        ka = jnp.concatenate([k_ref[0, pl.ds(row0, U * BLK), :], jnp.concatenate(e_rows, axis=0)], axis=1)
        s = lax.dot_general(qa_ref[...], ka, NT_DIMS, preferred_element_type=F32)
        return s + jnp.concatenate(b_cols, axis=1), row0

    def lane_chunks(s):
        return [s[:, c * D:(c + 1) * D] for c in range(s.shape[1] // D)]

    n_groups = i // U + 1

    qf = q.astype(F32)
    k_own = k_ref[0, pl.ds(pl.multiple_of(i * BLK, BLK), BLK), :].astype(F32)
    rb_max = rb_ref[0, hh]
    for bk in range(1, REL_BUCKETS):
        rb_max = jnp.maximum(rb_max, rb_ref[bk, hh])
    lo = jnp.sum(qf * k_own, axis=1, keepdims=True) + rb_ref[0, hh]
    hi = jnp.sqrt(jnp.sum(qf * qf, axis=1, keepdims=True) * k2max_ref[0]) + rb_max
    slack = 1e-3 * (jnp.abs(lo) + jnp.abs(hi) + 1.0)
    lo = lo - slack
    hi = hi + slack

    def exact_max():
        def max_step(g, mm):
            s, _ = group_scores(g)
            for c in lane_chunks(s):
                mm = jnp.maximum(mm, c)
            return mm
        mm = lax.fori_loop(0, n_groups, max_step, jnp.full((BLK, D), jnp.finfo(F32).min, F32))
        return jnp.max(mm, axis=1, keepdims=True)

    m = lax.cond(jnp.max(hi - lo) > 2.0 * MOBA_SHIFT_SLACK, exact_max, lambda: 0.5 * (lo + hi))

    set_query_operand(-m)

    def acc_step(g, carry):
        acc, lsum = carry
        s, row0 = group_scores(g)
        p = jnp.exp(s)
        for c in lane_chunks(p):
            lsum = lsum + c
        vg = v_ref[0, pl.ds(row0, U * BLK), :]
        return acc + jnp.dot(p.astype(BF16), vg, preferred_element_type=F32), lsum

    zero = jnp.zeros((BLK, D), F32)
    acc, lsum = lax.fori_loop(0, n_groups, acc_step, (zero, zero))

    out = acc / jnp.sum(lsum, axis=1, keepdims=True)
    z = z_ref[0].astype(F32)
    o_ref[0] = (out * (z * jax.nn.sigmoid(z))).astype(o_ref.dtype)


def _moba(rel_bias, p3, n_heads, col_q, col_k, col_v, col_z, group=4):
    b, s, _ = p3.shape
    BLK, D = MOBA_BLOCK, LANES
    nb = s // BLK
    nbp = -(-nb // 16) * 16
    assert nbp + 32 <= D, "selection-mask columns must fit in the spare contraction columns"
    assert nb % group == 0, "key blocks are visited in whole groups"
    n_near = _n_near_tiles()
    seq = lambda off: pl.BlockSpec((1, s, D), lambda hi, bi, i: (bi, 0, off + hi))
    blk = lambda off: pl.BlockSpec((1, BLK, D), lambda hi, bi, i: (bi, i, off + hi))
    return pl.pallas_call(
        functools.partial(_moba_kernel, n_near=n_near, nbp=nbp, group=group),
        grid=(n_heads, b, nb),
        in_specs=[pl.BlockSpec(memory_space=pltpu.SMEM), blk(col_q), seq(col_k), seq(col_v), blk(col_z)],
        out_specs=blk(0),
        out_shape=jax.ShapeDtypeStruct((b, s, n_heads * D), BF16),
        scratch_shapes=[pltpu.VMEM((nb, D), F32), pltpu.VMEM((n_near + 1, BLK, BLK), F32),
                        pltpu.VMEM((nb + 2, 16, D), BF16), pltpu.VMEM((BLK, 2 * D), BF16),
                        pltpu.SMEM((1,), F32)],
        compiler_params=_cparams(("arbitrary", "arbitrary", "arbitrary")),
        name="moba_attention",
    )(rel_bias, p3, p3, p3, p3)


def _merge_kernel(ya_ref, yb_ref, g0_ref, g1_ref, x_ref, w0_ref, w1_ref, wo_ref, b0_ref, b1_ref, nw_ref, o_ref):
    pa = jnp.dot(ya_ref[...], w0_ref[...], preferred_element_type=F32)
    pb = jnp.dot(yb_ref[...], w1_ref[...], preferred_element_type=F32)
    g0 = jax.nn.sigmoid(g0_ref[...].astype(F32) + b0_ref[...])
    g1 = jax.nn.sigmoid(g1_ref[...].astype(F32) + b1_ref[...])
    merged = (g0 * pa + g1 * pb).astype(BF16)
    out = jnp.dot(merged, wo_ref[...], preferred_element_type=F32)
    y = out * lax.rsqrt(jnp.mean(out * out, axis=-1, keepdims=True) + NORM_EPS) * nw_ref[...]
    o_ref[...] = x_ref[...] + y


def _merge(ya, yb, p2, x2d, w0, w1, wo, b_gate, norm_w, col_g0, col_g1, tm=512):
    t, d = x2d.shape
    row = lambda c: pl.BlockSpec((tm, d), lambda i: (i, c))
    full = pl.BlockSpec((d, d), lambda i: (0, 0))
    vec = lambda c: pl.BlockSpec((1, d), lambda i: (0, c))
    return pl.pallas_call(
        _merge_kernel,
        grid=(t // tm,),
        in_specs=[row(0), row(0), row(col_g0), row(col_g1), row(0), full, full, full, vec(0), vec(1), vec(0)],
        out_specs=row(0),
        out_shape=jax.ShapeDtypeStruct((t, d), F32),
        compiler_params=_cparams(("parallel",)),
        name="merge_out",
    )(ya, yb, p2, p2, x2d, w0, w1, wo, b_gate.reshape(1, -1), b_gate.reshape(1, -1), norm_w.reshape(1, d))


def kernel(x, norm_pre_w, w_in, b_gate, conv_w, dn_a_log, dn_dt_bias, dn_onorm_w, rel_bias, w_branch, w_out,
           norm_post_w):
    b, s, d = x.shape
    n_heads = dn_a_log.shape[0]
    hd = dn_onorm_w.shape[0]
    width = n_heads * hd
    assert hd == LANES and rel_bias.shape[1] == n_heads and width == d
    assert s % MOBA_BLOCK == 0 and s % DN_GROUP == 0 and 2 * n_heads <= LANES
    t = b * s

    c_beta = 4 * width
    c_qb = c_beta + 2 * n_heads
    w16 = w_in.astype(BF16)
    w_main = jnp.concatenate([w16[:, :c_beta], w16[:, c_qb:]], axis=1)
    w_small = jnp.pad(w16[:, c_beta:c_qb], ((0, 0), (0, LANES - 2 * n_heads)))
    nblk = width // LANES
    col = lambda k: k * nblk

    x2d = x.reshape(t, d)
    hn = _rmsnorm(x2d, norm_pre_w)
    p2 = _matmul(hn, w_main, BF16, tm=1024, tn=width, name="in_proj_main", scaled_tile=4, scale=hd ** -0.5)
    small = _matmul(hn, w_small, F32, tm=1024, tn=LANES, name="in_proj_small")
    p3 = p2.reshape(b, s, -1)

    pad = (0, LANES - 2 * n_heads)
    a_log_row = jnp.pad(jnp.concatenate([jnp.zeros_like(dn_a_log), dn_a_log]), pad).reshape(1, LANES).astype(F32)
    dt_row = jnp.pad(jnp.concatenate([jnp.zeros_like(dn_dt_bias), dn_dt_bias]), pad).reshape(1, LANES).astype(F32)
    ya = _deltanet(p3, small.reshape(b, s, LANES), conv_w.astype(F32), a_log_row, dt_row,
                   dn_onorm_w.reshape(1, hd).astype(F32), n_heads, col(0), col(1), col(2), col(3))
    yb = _moba(rel_bias.astype(F32), p3, n_heads, col(4), col(5), col(6), col(7))

    out = _merge(ya.reshape(t, d), yb.reshape(t, d), p2, x2d,
                 w_branch[0].astype(BF16), w_branch[1].astype(BF16), w_out.astype(BF16),
                 b_gate.astype(F32), norm_post_w.astype(F32), 8, 9)
    return out.reshape(b, s, d)
```

```python
import functools
import math

import jax
import jax.numpy as jnp
from jax import lax
from jax.experimental import pallas as pl
from jax.experimental.pallas import tpu as pltpu

F32 = jnp.float32
BF16 = jnp.bfloat16

LANES = 128
DN_CHUNK = 64
DN_GROUP = 128
DN_CONV_HALO = 8
MOBA_BLOCK = 256
MOBA_TOPK = 3
MOBA_SHIFT_SLACK = 40.0
REL_BUCKETS = 32
REL_MAX_DIST = 2048
NORM_EPS = 1e-6
NEG_INF = -1e30
VMEM_LIMIT = 56 * 1024 * 1024
NT_DIMS = (((1,), (1,)), ((), ()))


def _cparams(sem):
    return pltpu.CompilerParams(dimension_semantics=sem, vmem_limit_bytes=VMEM_LIMIT)


def _rmsnorm_kernel(x_ref, w_ref, o_ref):
    x = x_ref[...]
    y = x * lax.rsqrt(jnp.mean(x * x, axis=-1, keepdims=True) + NORM_EPS)
    o_ref[...] = (y * w_ref[...]).astype(o_ref.dtype)


def _rmsnorm(x2d, w, tm=1024):
    t, d = x2d.shape
    return pl.pallas_call(
        _rmsnorm_kernel,
        grid=(t // tm,),
        in_specs=[pl.BlockSpec((tm, d), lambda i: (i, 0)), pl.BlockSpec((1, d), lambda i: (0, 0))],
        out_specs=pl.BlockSpec((tm, d), lambda i: (i, 0)),
        out_shape=jax.ShapeDtypeStruct((t, d), BF16),
        compiler_params=_cparams(("parallel",)),
        name="rmsnorm_pre",
    )(x2d, w.reshape(1, d))


def _mm_kernel(h_ref, w_ref, o_ref, *, scaled_tile, scale):
    acc = jnp.dot(h_ref[...], w_ref[...], preferred_element_type=F32)
    if scaled_tile is not None:
        acc = acc * jnp.where(pl.program_id(1) == scaled_tile, scale, 1.0).astype(F32)
    o_ref[...] = acc.astype(o_ref.dtype)


def _matmul(h, w, out_dtype, tm, tn, name, scaled_tile=None, scale=1.0):
    t, d = h.shape
    n = w.shape[1]
    return pl.pallas_call(
        functools.partial(_mm_kernel, scaled_tile=scaled_tile, scale=scale),
        grid=(t // tm, n // tn),
        in_specs=[pl.BlockSpec((tm, d), lambda i, j: (i, 0)), pl.BlockSpec((d, tn), lambda i, j: (0, j))],
        out_specs=pl.BlockSpec((tm, tn), lambda i, j: (i, j)),
        out_shape=jax.ShapeDtypeStruct((t, n), out_dtype),
        compiler_params=_cparams(("parallel", "arbitrary")),
        name=name,
    )(h, w)


def _deltanet_kernel(q_ref, k_ref, v_ref, z_ref, sm_ref, cq_ref, ck_ref, cv_ref, alog_ref, dtb_ref, onw_ref,
                     o_ref, state_ref, halo_ref, vnew_ref, gct_ref, *, n_heads, heads_per_step):
    hg = pl.program_id(1)
    g = pl.program_id(2)
    G, C, D, NH = DN_GROUP, DN_CHUNK, LANES, heads_per_step
    lc = C.bit_length() - 1

    @pl.when(g == 0)
    def _():
        state_ref[...] = jnp.zeros_like(state_ref)
        halo_ref[...] = jnp.zeros_like(halo_ref)

    def conv_silu(x_ref, slot, w_ref):
        cur = x_ref[0].astype(F32)
        ext = jnp.concatenate([halo_ref[slot], cur], axis=0)
        w = w_ref[...]
        kw = w.shape[0]
        base = DN_CONV_HALO - (kw - 1)
        y = ext[base:base + G] * w[0:1]
        for kk in range(1, kw):
            y = y + ext[base + kk:base + kk + G] * w[kk:kk + 1]
        halo_ref[slot] = cur[G - DN_CONV_HALO:G]
        return y * jax.nn.sigmoid(y)

    def l2n(x):
        return x * lax.rsqrt(jnp.sum(x * x, axis=-1, keepdims=True) + NORM_EPS)

    qf = conv_silu(q_ref, 0, cq_ref)
    kf = conv_silu(k_ref, 1, ck_ref)
    vf = conv_silu(v_ref, 2, cv_ref)

    cs = sm_ref[0]
    lane = lax.broadcasted_iota(jnp.int32, (G, LANES), 1)
    row = lax.broadcasted_iota(jnp.int32, (G, LANES), 0)
    beta_all = jax.nn.sigmoid(cs)
    xg = cs + dtb_ref[...]
    softplus = jnp.maximum(xg, 0.0) + jnp.log1p(jnp.exp(-jnp.abs(xg)))
    g_all = -jnp.exp(alog_ref[...]) * softplus
    pos = row & (C - 1)
    gc_all = g_all
    shift = 1
    while shift < C:
        gc_all = gc_all + jnp.where(pos >= shift, pltpu.roll(gc_all, shift, 0), 0.0)
        shift *= 2
    gl_all = jnp.concatenate([jnp.broadcast_to(gc_all[c * C + C - 1:c * C + C], (C, LANES))
                              for c in range(G // C)], axis=0)
    gct_ref[...] = gc_all.T

    ri = lax.broadcasted_iota(jnp.int32, (G, G), 0)
    ci = lax.broadcasted_iota(jnp.int32, (G, G), 1)
    same = (ri >> lc) == (ci >> lc)
    tri_incl = same & (ri >= ci)
    tri_strict = same & (ri > ci)
    eye = jnp.where(ri == ci, 1.0, 0.0)
    ct = lax.broadcasted_iota(jnp.int32, (D, G), 1) >> lc

    def pair_mask(s):
        ls = s.bit_length() - 1
        return ((ri >> (ls + 1)) == (ci >> (ls + 1))) & (((ri >> ls) & 1) == 1) & (((ci >> ls) & 1) == 0)

    pair_masks = {}
    s = 1
    while s < C:
        pair_masks[s] = pair_mask(s)
        s *= 2

    heads = range(NH)
    sls = [slice(hl * D, (hl + 1) * D) for hl in heads]
    hidx = [hg * NH + hl for hl in heads]

    def col_of(x, lane_idx):
        return jnp.sum(jnp.where(lane == lane_idx, x, 0.0), axis=1, keepdims=True)

    q = [l2n(qf[:, sl]) * (D ** -0.5) for sl in sls]
    k = [l2n(kf[:, sl]) for sl in sls]
    beta_col = [col_of(beta_all, h) for h in hidx]
    gc_col = [col_of(gc_all, h + n_heads) for h in hidx]
    gl_col = [col_of(gl_all, h + n_heads) for h in hidx]
    gc_row = [gct_ref[pl.ds(h + n_heads, 1), :] for h in hidx]
    decay = [jnp.where(tri_incl, jnp.exp(jnp.where(tri_incl, gc_col[a] - gc_row[a], 0.0)), 0.0) for a in heads]
    kb = [k[a] * beta_col[a] for a in heads]
    vb = [vf[:, sls[a]] * beta_col[a] for a in heads]
    k16 = [x.astype(BF16) for x in k]
    kk = [lax.dot_general(kb[a].astype(BF16), k16[a], NT_DIMS, preferred_element_type=F32) for a in heads]
    qk = [lax.dot_general(q[a].astype(BF16), k16[a], NT_DIMS, preferred_element_type=F32) for a in heads]
    lower = [jnp.where(tri_strict, kk[a] * decay[a], 0.0) for a in heads]
    attn = [jnp.where(tri_incl, qk[a] * decay[a], 0.0).astype(BF16) for a in heads]

    tinv = [eye - jnp.where(pair_masks[1], lower[a], 0.0) for a in heads]
    s = 2
    while s < C:
        t16 = [x.astype(BF16) for x in tinv]
        md = [jnp.dot(jnp.where(pair_masks[s], lower[a], 0.0).astype(BF16), t16[a], preferred_element_type=F32)
              for a in heads]
        tinv = [tinv[a] - jnp.dot(t16[a], md[a].astype(BF16), preferred_element_type=F32) for a in heads]
        s *= 2

    egc = [jnp.exp(x) for x in gc_col]
    uw = [jnp.dot(tinv[a].astype(BF16), jnp.concatenate([vb[a], kb[a] * egc[a]], axis=1).astype(BF16),
                  preferred_element_type=F32) for a in heads]
    q_dec = [q[a] * egc[a] for a in heads]
    k_dec_t = [(k[a] * jnp.exp(gl_col[a] - gc_col[a])).T for a in heads]
    egl = [jnp.exp(x) for x in gl_col]

    vnew_ref[...] = jnp.zeros_like(vnew_ref)
    outs = [[] for _ in heads]
    for c in range(G // C):
        r0 = c * C
        st = [state_ref[a] for a in heads]
        ws_qs = [jnp.dot(jnp.concatenate([uw[a][r0:r0 + C, D:], q_dec[a][r0:r0 + C]], axis=0).astype(BF16),
                         st[a].astype(BF16), preferred_element_type=F32) for a in heads]
        for a in heads:
            vnew_ref[a, r0:r0 + C, :] = (uw[a][r0:r0 + C, :D] - ws_qs[a][:C]).astype(BF16)
        vn = [vnew_ref[a] for a in heads]
        for a in heads:
            outs[a].append(ws_qs[a][C:] + jnp.dot(attn[a][r0:r0 + C], vn[a], preferred_element_type=F32))
        for a in heads:
            kt_c = jnp.where(ct == c, k_dec_t[a], 0.0).astype(BF16)
            e_c = jnp.concatenate([egl[a][r0:r0 + C]] * (D // C), axis=0)
            state_ref[a] = st[a] * e_c + jnp.dot(kt_c, vn[a], preferred_element_type=F32)

    for a in heads:
        o = jnp.concatenate(outs[a], axis=0)
        o = o * lax.rsqrt(jnp.mean(o * o, axis=-1, keepdims=True) + NORM_EPS) * onw_ref[...]
        z = z_ref[0, :, sls[a]].astype(F32)
        o_ref[0, :, sls[a]] = (o * (z * jax.nn.sigmoid(z))).astype(o_ref.dtype)


def _deltanet(p3, small3, conv_w, a_log_row, dt_row, onorm_w, n_heads, col_q, col_k, col_v, col_z,
              heads_per_step=8):
    b, s, _ = p3.shape
    G, D, NH = DN_GROUP, LANES, heads_per_step
    assert n_heads % NH == 0 and all(c % NH == 0 for c in (col_q, col_k, col_v, col_z))
    kw = conv_w.shape[0]
    blk = lambda off: pl.BlockSpec((1, G, NH * D), lambda bi, hi, gi: (bi, gi, off // NH + hi))
    cw = lambda off: pl.BlockSpec((kw, NH * D), lambda bi, hi, gi: (0, off // NH + hi))
    row_spec = pl.BlockSpec((1, LANES), lambda bi, hi, gi: (0, 0))
    return pl.pallas_call(
        functools.partial(_deltanet_kernel, n_heads=n_heads, heads_per_step=NH),
        grid=(b, n_heads // NH, s // G),
        in_specs=[blk(col_q), blk(col_k), blk(col_v), blk(col_z),
                  pl.BlockSpec((1, G, LANES), lambda bi, hi, gi: (bi, gi, 0)),
                  cw(0), cw(n_heads), cw(2 * n_heads), row_spec, row_spec, row_spec],
        out_specs=pl.BlockSpec((1, G, NH * D), lambda bi, hi, gi: (bi, gi, hi)),
        out_shape=jax.ShapeDtypeStruct((b, s, n_heads * D), BF16),
        scratch_shapes=[pltpu.VMEM((NH, D, D), F32), pltpu.VMEM((3, DN_CONV_HALO, NH * D), F32),
                        pltpu.VMEM((NH, G, D), BF16), pltpu.VMEM((LANES, G), F32)],
        compiler_params=_cparams(("parallel", "parallel", "arbitrary")),
        name="gated_deltanet",
    )(p3, p3, p3, p3, small3, conv_w, conv_w, conv_w, a_log_row, dt_row, onorm_w)


def _t5_bucket_f32(dist):
    dist = jnp.maximum(dist, 0)
    max_exact = REL_BUCKETS // 2
    d = jnp.maximum(dist, 1).astype(F32)
    large = max_exact + (jnp.log(d / max_exact) / math.log(REL_MAX_DIST / max_exact)
                         * (REL_BUCKETS - max_exact)).astype(jnp.int32)
    large = jnp.minimum(large, REL_BUCKETS - 1)
    return jnp.where(dist < max_exact, dist, large)


def _n_near_tiles():
    return -(-(REL_MAX_DIST + MOBA_BLOCK - 1) // MOBA_BLOCK)


def _moba_kernel(rb_ref, q_ref, k_ref, v_ref, z_ref, o_ref, kmean_ref, bias_ref, etab_ref, qa_ref, s0_ref, s1_ref,
                 k2max_ref, *, n_near, nbp, group):
    hh = pl.program_id(0)
    bb = pl.program_id(1)
    i = pl.program_id(2)
    BLK, D, U = MOBA_BLOCK, LANES, group
    nb = k_ref.shape[1] // BLK
    col_shift, col_dead = nbp, nbp + 16
    e_own, e_dead = nb, nb + 1

    @pl.when((bb == 0) & (i == 0))
    def _():
        qr = lax.broadcasted_iota(jnp.int32, (BLK, BLK), 0)
        kc = lax.broadcasted_iota(jnp.int32, (BLK, BLK), 1)
        for t in range(n_near):
            dist = t * BLK + qr - kc
            bucket = _t5_bucket_f32(dist)
            val = jnp.full((BLK, BLK), rb_ref[0, hh], F32)
            for bk in range(1, REL_BUCKETS):
                val = jnp.where(bucket == bk, rb_ref[bk, hh], val)
            if t == 0:
                val = jnp.where(dist >= 0, val, NEG_INF)
            bias_ref[t] = val
        bias_ref[n_near] = jnp.full((BLK, BLK), rb_ref[REL_BUCKETS - 1, hh], F32)
        lane = lax.broadcasted_iota(jnp.int32, (16, D), 1)
        for j in range(nb + 2):
            hot = (lane == col_shift) | ((lane == j) & (j < nb)) | ((lane == col_dead) & (j == e_dead))
            etab_ref[j] = jnp.where(hot, 1.0, 0.0).astype(BF16)

    @pl.when(i == 0)
    def _():
        def body(j, k2max):
            kj = k_ref[0, pl.ds(pl.multiple_of(j * BLK, BLK), BLK), :].astype(F32)
            kmean_ref[pl.ds(j, 1), :] = jnp.mean(kj, axis=0, keepdims=True)
            return jnp.maximum(k2max, jnp.max(jnp.sum(kj * kj, axis=1, keepdims=True)))
        k2max_ref[0] = lax.fori_loop(0, nb, body, jnp.float32(0.0))

    q = q_ref[0]
    gate = lax.dot_general(kmean_ref[...].astype(BF16), q, NT_DIMS, preferred_element_type=F32)
    jrow = lax.broadcasted_iota(jnp.int32, gate.shape, 0)
    jrow_f = jrow.astype(F32)
    past = jrow < i
    sc = jnp.where(past, gate, NEG_INF)
    selected = jnp.zeros(gate.shape, jnp.bool_)
    for _ in range(MOBA_TOPK):
        top = jnp.max(sc, axis=0, keepdims=True)
        first = jnp.min(jnp.where(sc == top, jrow_f, float(nb)), axis=0, keepdims=True)
        pick = jrow_f == first
        selected = selected | (pick & past)
        sc = jnp.where(pick, jnp.finfo(F32).min, sc)
    parts = [jnp.where(selected, 0.0, NEG_INF)]
    if nbp > nb:
        parts.append(jnp.zeros((nbp - nb, BLK), F32))
    parts += [jnp.zeros((16, BLK), F32), jnp.full((16, BLK), NEG_INF, F32)]
    if D - col_dead - 16:
        parts.append(jnp.zeros((D - col_dead - 16, BLK), F32))
    extra = jnp.concatenate(parts, axis=0).T
    lane_q = lax.broadcasted_iota(jnp.int32, (BLK, D), 1)

    def set_query_operand(shift_col):
        qa_ref[:, D:2 * D] = jnp.where(lane_q == col_shift, shift_col, extra).astype(BF16)

    qa_ref[:, 0:D] = q
    set_query_operand(jnp.zeros((BLK, 1), F32))

    def group_scores(g):
        j0 = g * U
        row0 = pl.multiple_of(j0 * BLK, U * BLK)
        e_rows, b_cols = [], []
        for u in range(U):
            j = j0 + u
            e_idx = jnp.where(j == i, e_own, jnp.where(j > i, e_dead, j))
            e_rows += [etab_ref[e_idx]] * (BLK // 16)
            b_cols.append(bias_ref[jnp.clip(i - j, 0, n_near)])
        ka = jnp.concatenate([k_ref[0, pl.ds(row0, U * BLK), :], jnp.concatenate(e_rows, axis=0)], axis=1)
        s = lax.dot_general(qa_ref[...], ka, NT_DIMS, preferred_element_type=F32)
        return s + jnp.concatenate(b_cols, axis=1), row0

    def lane_chunks(s):
        return [s[:, c * D:(c + 1) * D] for c in range(s.shape[1] // D)]

    n_groups = i // U + 1

    qf = q.astype(F32)
    k_own = k_ref[0, pl.ds(pl.multiple_of(i * BLK, BLK), BLK), :].astype(F32)
    rb_max = rb_ref[0, hh]
    for bk in range(1, REL_BUCKETS):
        rb_max = jnp.maximum(rb_max, rb_ref[bk, hh])
    lo = jnp.sum(qf * k_own, axis=1, keepdims=True) + rb_ref[0, hh]
    hi = jnp.sqrt(jnp.sum(qf * qf, axis=1, keepdims=True) * k2max_ref[0]) + rb_max
    slack = 1e-3 * (jnp.abs(lo) + jnp.abs(hi) + 1.0)
    lo = lo - slack
    hi = hi + slack

    def exact_max():
        def max_step(g, mm):
            s, _ = group_scores(g)
            for c in lane_chunks(s):
                mm = jnp.maximum(mm, c)
            return mm
        mm = lax.fori_loop(0, n_groups, max_step, jnp.full((BLK, D), jnp.finfo(F32).min, F32))
        return jnp.max(mm, axis=1, keepdims=True)

    m = lax.cond(jnp.max(hi - lo) > 2.0 * MOBA_SHIFT_SLACK, exact_max, lambda: 0.5 * (lo + hi))

    set_query_operand(-m)

    def consume(g, s, carry):
        acc, lsum = carry
        p = jnp.exp(s)
        for c in lane_chunks(p):
            lsum = lsum + c
        vg = v_ref[0, pl.ds(pl.multiple_of(g * (U * BLK), U * BLK), U * BLK), :]
        return acc + jnp.dot(p.astype(BF16), vg, preferred_element_type=F32), lsum

    def stage(g, cur_ref, nxt_ref, carry):
        nxt_ref[...] = group_scores(g + 1)[0]
        return consume(g, cur_ref[...], carry)

    def two_stages(t, carry):
        carry = stage(2 * t, s0_ref, s1_ref, carry)
        return stage(2 * t + 1, s1_ref, s0_ref, carry)

    s0_ref[...] = group_scores(0)[0]
    zero = jnp.zeros((BLK, D), F32)
    last = n_groups - 1
    carry = lax.fori_loop(0, last // 2, two_stages, (zero, zero))
    def odd_tail(c):
        c = stage(last - 1, s0_ref, s1_ref, c)
        return consume(last, s1_ref[...], c)

    acc, lsum = lax.cond(last % 2 == 1, odd_tail, lambda c: consume(last, s0_ref[...], c), carry)

    out = acc / jnp.sum(lsum, axis=1, keepdims=True)
    z = z_ref[0].astype(F32)
    o_ref[0] = (out * (z * jax.nn.sigmoid(z))).astype(o_ref.dtype)


def _moba(rel_bias, p3, n_heads, col_q, col_k, col_v, col_z, group=4):
    b, s, _ = p3.shape
    BLK, D = MOBA_BLOCK, LANES
    nb = s // BLK
    nbp = -(-nb // 16) * 16
    assert nbp + 32 <= D, "selection-mask columns must fit in the spare contraction columns"
    assert nb % group == 0, "key blocks are visited in whole groups"
    n_near = _n_near_tiles()
    seq = lambda off: pl.BlockSpec((1, s, D), lambda hi, bi, i: (bi, 0, off + hi))
    blk = lambda off: pl.BlockSpec((1, BLK, D), lambda hi, bi, i: (bi, i, off + hi))
    return pl.pallas_call(
        functools.partial(_moba_kernel, n_near=n_near, nbp=nbp, group=group),
        grid=(n_heads, b, nb),
        in_specs=[pl.BlockSpec(memory_space=pltpu.SMEM), blk(col_q), seq(col_k), seq(col_v), blk(col_z)],
        out_specs=blk(0),
        out_shape=jax.ShapeDtypeStruct((b, s, n_heads * D), BF16),
        scratch_shapes=[pltpu.VMEM((nb, D), F32), pltpu.VMEM((n_near + 1, BLK, BLK), F32),
                        pltpu.VMEM((nb + 2, 16, D), BF16), pltpu.VMEM((BLK, 2 * D), BF16),
                        pltpu.VMEM((BLK, group * BLK), F32), pltpu.VMEM((BLK, group * BLK), F32),
                        pltpu.SMEM((1,), F32)],
        compiler_params=_cparams(("arbitrary", "arbitrary", "arbitrary")),
        name="moba_attention",
    )(rel_bias, p3, p3, p3, p3)


def _merge_kernel(ya_ref, yb_ref, g0_ref, g1_ref, x_ref, w0_ref, w1_ref, wo_ref, b0_ref, b1_ref, nw_ref, o_ref):
    pa = jnp.dot(ya_ref[...], w0_ref[...], preferred_element_type=F32)
    pb = jnp.dot(yb_ref[...], w1_ref[...], preferred_element_type=F32)
    g0 = jax.nn.sigmoid(g0_ref[...].astype(F32) + b0_ref[...])
    g1 = jax.nn.sigmoid(g1_ref[...].astype(F32) + b1_ref[...])
    merged = (g0 * pa + g1 * pb).astype(BF16)
    out = jnp.dot(merged, wo_ref[...], preferred_element_type=F32)
    y = out * lax.rsqrt(jnp.mean(out * out, axis=-1, keepdims=True) + NORM_EPS) * nw_ref[...]
    o_ref[...] = x_ref[...] + y


def _merge(ya, yb, p2, x2d, w0, w1, wo, b_gate, norm_w, col_g0, col_g1, tm=512):
    t, d = x2d.shape
    row = lambda c: pl.BlockSpec((tm, d), lambda i: (i, c))
    full = pl.BlockSpec((d, d), lambda i: (0, 0))
    vec = lambda c: pl.BlockSpec((1, d), lambda i: (0, c))
    return pl.pallas_call(
        _merge_kernel,
        grid=(t // tm,),
        in_specs=[row(0), row(0), row(col_g0), row(col_g1), row(0), full, full, full, vec(0), vec(1), vec(0)],
        out_specs=row(0),
        out_shape=jax.ShapeDtypeStruct((t, d), F32),
        compiler_params=_cparams(("parallel",)),
        name="merge_out",
    )(ya, yb, p2, p2, x2d, w0, w1, wo, b_gate.reshape(1, -1), b_gate.reshape(1, -1), norm_w.reshape(1, d))


def kernel(x, norm_pre_w, w_in, b_gate, conv_w, dn_a_log, dn_dt_bias, dn_onorm_w, rel_bias, w_branch, w_out,
           norm_post_w):
    b, s, d = x.shape
    n_heads = dn_a_log.shape[0]
    hd = dn_onorm_w.shape[0]
    width = n_heads * hd
    assert hd == LANES and rel_bias.shape[1] == n_heads and width == d
    assert s % MOBA_BLOCK == 0 and s % DN_GROUP == 0 and 2 * n_heads <= LANES
    t = b * s

    c_beta = 4 * width
    c_qb = c_beta + 2 * n_heads
    w16 = w_in.astype(BF16)
    w_main = jnp.concatenate([w16[:, :c_beta], w16[:, c_qb:]], axis=1)
    w_small = jnp.pad(w16[:, c_beta:c_qb], ((0, 0), (0, LANES - 2 * n_heads)))
    nblk = width // LANES
    col = lambda k: k * nblk

    x2d = x.reshape(t, d)
    hn = _rmsnorm(x2d, norm_pre_w)
    p2 = _matmul(hn, w_main, BF16, tm=1024, tn=width, name="in_proj_main", scaled_tile=4, scale=hd ** -0.5)
    small = _matmul(hn, w_small, F32, tm=1024, tn=LANES, name="in_proj_small")
    p3 = p2.reshape(b, s, -1)

    pad = (0, LANES - 2 * n_heads)
    a_log_row = jnp.pad(jnp.concatenate([jnp.zeros_like(dn_a_log), dn_a_log]), pad).reshape(1, LANES).astype(F32)
    dt_row = jnp.pad(jnp.concatenate([jnp.zeros_like(dn_dt_bias), dn_dt_bias]), pad).reshape(1, LANES).astype(F32)
    ya = _deltanet(p3, small.reshape(b, s, LANES), conv_w.astype(F32), a_log_row, dt_row,
                   dn_onorm_w.reshape(1, hd).astype(F32), n_heads, col(0), col(1), col(2), col(3))
    yb = _moba(rel_bias.astype(F32), p3, n_heads, col(4), col(5), col(6), col(7))

    out = _merge(ya.reshape(t, d), yb.reshape(t, d), p2, x2d,
                 w_branch[0].astype(BF16), w_branch[1].astype(BF16), w_out.astype(BF16),
                 b_gate.astype(F32), norm_post_w.astype(F32), 8, 9)
    return out.reshape(b, s, d)
```

```python
import functools
import math

import jax
import jax.numpy as jnp
from jax import lax
from jax.experimental import pallas as pl
from jax.experimental.pallas import tpu as pltpu

F32 = jnp.float32
BF16 = jnp.bfloat16

LANES = 128
DN_CHUNK = 64
DN_GROUP = 128
DN_CONV_HALO = 8
MOBA_BLOCK = 256
MOBA_TOPK = 3
MOBA_SHIFT_SLACK = 40.0
REL_BUCKETS = 32
REL_MAX_DIST = 2048
NORM_EPS = 1e-6
NEG_INF = -1e30
VMEM_LIMIT = 56 * 1024 * 1024
NT_DIMS = (((1,), (1,)), ((), ()))


def _cparams(sem):
    return pltpu.CompilerParams(dimension_semantics=sem, vmem_limit_bytes=VMEM_LIMIT)


def _rmsnorm_kernel(x_ref, w_ref, o_ref):
    x = x_ref[...]
    y = x * lax.rsqrt(jnp.mean(x * x, axis=-1, keepdims=True) + NORM_EPS)
    o_ref[...] = (y * w_ref[...]).astype(o_ref.dtype)


def _rmsnorm(x2d, w, tm=1024):
    t, d = x2d.shape
    return pl.pallas_call(
        _rmsnorm_kernel,
        grid=(t // tm,),
        in_specs=[pl.BlockSpec((tm, d), lambda i: (i, 0)), pl.BlockSpec((1, d), lambda i: (0, 0))],
        out_specs=pl.BlockSpec((tm, d), lambda i: (i, 0)),
        out_shape=jax.ShapeDtypeStruct((t, d), BF16),
        compiler_params=_cparams(("parallel",)),
        name="rmsnorm_pre",
    )(x2d, w.reshape(1, d))


def _mm_kernel(h_ref, w_ref, o_ref, *, scaled_tile, scale):
    acc = jnp.dot(h_ref[...], w_ref[...], preferred_element_type=F32)
    if scaled_tile is not None:
        acc = acc * jnp.where(pl.program_id(1) == scaled_tile, scale, 1.0).astype(F32)
    o_ref[...] = acc.astype(o_ref.dtype)


def _matmul(h, w, out_dtype, tm, tn, name, scaled_tile=None, scale=1.0):
    t, d = h.shape
    n = w.shape[1]
    return pl.pallas_call(
        functools.partial(_mm_kernel, scaled_tile=scaled_tile, scale=scale),
        grid=(t // tm, n // tn),
        in_specs=[pl.BlockSpec((tm, d), lambda i, j: (i, 0)), pl.BlockSpec((d, tn), lambda i, j: (0, j))],
        out_specs=pl.BlockSpec((tm, tn), lambda i, j: (i, j)),
        out_shape=jax.ShapeDtypeStruct((t, n), out_dtype),
        compiler_params=_cparams(("parallel", "arbitrary")),
        name=name,
    )(h, w)


def _deltanet_kernel(q_ref, k_ref, v_ref, z_ref, sm_ref, cq_ref, ck_ref, cv_ref, alog_ref, dtb_ref, onw_ref,
                     o_ref, state_ref, halo_ref, vnew_ref, gct_ref, *, n_heads, heads_per_step):
    hg = pl.program_id(1)
    g = pl.program_id(2)
    G, C, D, NH = DN_GROUP, DN_CHUNK, LANES, heads_per_step
    lc = C.bit_length() - 1

    @pl.when(g == 0)
    def _():
        state_ref[...] = jnp.zeros_like(state_ref)
        halo_ref[...] = jnp.zeros_like(halo_ref)

    def conv_silu(x_ref, slot, w_ref):
        cur = x_ref[0].astype(F32)
        ext = jnp.concatenate([halo_ref[slot], cur], axis=0)
        w = w_ref[...]
        kw = w.shape[0]
        base = DN_CONV_HALO - (kw - 1)
        y = ext[base:base + G] * w[0:1]
        for kk in range(1, kw):
            y = y + ext[base + kk:base + kk + G] * w[kk:kk + 1]
        halo_ref[slot] = cur[G - DN_CONV_HALO:G]
        return y * jax.nn.sigmoid(y)

    def l2n(x):
        return x * lax.rsqrt(jnp.sum(x * x, axis=-1, keepdims=True) + NORM_EPS)

    qf = conv_silu(q_ref, 0, cq_ref)
    kf = conv_silu(k_ref, 1, ck_ref)
    vf = conv_silu(v_ref, 2, cv_ref)

    cs = sm_ref[0]
    lane = lax.broadcasted_iota(jnp.int32, (G, LANES), 1)
    row = lax.broadcasted_iota(jnp.int32, (G, LANES), 0)
    beta_all = jax.nn.sigmoid(cs)
    xg = cs + dtb_ref[...]
    softplus = jnp.maximum(xg, 0.0) + jnp.log1p(jnp.exp(-jnp.abs(xg)))
    g_all = -jnp.exp(alog_ref[...]) * softplus
    pos = row & (C - 1)
    gc_all = g_all
    shift = 1
    while shift < C:
        gc_all = gc_all + jnp.where(pos >= shift, pltpu.roll(gc_all, shift, 0), 0.0)
        shift *= 2
    gl_all = jnp.concatenate([jnp.broadcast_to(gc_all[c * C + C - 1:c * C + C], (C, LANES))
                              for c in range(G // C)], axis=0)
    gct_ref[...] = gc_all.T

    ri = lax.broadcasted_iota(jnp.int32, (G, G), 0)
    ci = lax.broadcasted_iota(jnp.int32, (G, G), 1)
    same = (ri >> lc) == (ci >> lc)
    tri_incl = same & (ri >= ci)
    tri_strict = same & (ri > ci)
    eye = jnp.where(ri == ci, 1.0, 0.0)
    ct = lax.broadcasted_iota(jnp.int32, (D, G), 1) >> lc

    def pair_mask(s):
        ls = s.bit_length() - 1
        return ((ri >> (ls + 1)) == (ci >> (ls + 1))) & (((ri >> ls) & 1) == 1) & (((ci >> ls) & 1) == 0)

    pair_masks = {}
    s = 1
    while s < C:
        pair_masks[s] = pair_mask(s)
        s *= 2

    heads = range(NH)
    sls = [slice(hl * D, (hl + 1) * D) for hl in heads]
    hidx = [hg * NH + hl for hl in heads]

    def col_of(x, lane_idx):
        return jnp.sum(jnp.where(lane == lane_idx, x, 0.0), axis=1, keepdims=True)

    q = [l2n(qf[:, sl]) * (D ** -0.5) for sl in sls]
    k = [l2n(kf[:, sl]) for sl in sls]
    beta_col = [col_of(beta_all, h) for h in hidx]
    gc_col = [col_of(gc_all, h + n_heads) for h in hidx]
    gl_col = [col_of(gl_all, h + n_heads) for h in hidx]
    gc_row = [gct_ref[pl.ds(h + n_heads, 1), :] for h in hidx]
    decay = [jnp.where(tri_incl, jnp.exp(jnp.where(tri_incl, gc_col[a] - gc_row[a], 0.0)), 0.0) for a in heads]
    kb = [k[a] * beta_col[a] for a in heads]
    vb = [vf[:, sls[a]] * beta_col[a] for a in heads]
    k16 = [x.astype(BF16) for x in k]
    kk = [lax.dot_general(kb[a].astype(BF16), k16[a], NT_DIMS, preferred_element_type=F32) for a in heads]
    qk = [lax.dot_general(q[a].astype(BF16), k16[a], NT_DIMS, preferred_element_type=F32) for a in heads]
    lower = [jnp.where(tri_strict, kk[a] * decay[a], 0.0) for a in heads]
    attn = [jnp.where(tri_incl, qk[a] * decay[a], 0.0).astype(BF16) for a in heads]

    tinv = [eye - jnp.where(pair_masks[1], lower[a], 0.0) for a in heads]
    s = 2
    while s < C:
        t16 = [x.astype(BF16) for x in tinv]
        md = [jnp.dot(jnp.where(pair_masks[s], lower[a], 0.0).astype(BF16), t16[a], preferred_element_type=F32)
              for a in heads]
        tinv = [tinv[a] - jnp.dot(t16[a], md[a].astype(BF16), preferred_element_type=F32) for a in heads]
        s *= 2

    egc = [jnp.exp(x) for x in gc_col]
    uw = [jnp.dot(tinv[a].astype(BF16), jnp.concatenate([vb[a], kb[a] * egc[a]], axis=1).astype(BF16),
                  preferred_element_type=F32) for a in heads]
    q_dec = [q[a] * egc[a] for a in heads]
    k_dec_t = [(k[a] * jnp.exp(gl_col[a] - gc_col[a])).T for a in heads]
    egl = [jnp.exp(x) for x in gl_col]

    vnew_ref[...] = jnp.zeros_like(vnew_ref)
    outs = [[] for _ in heads]
    for c in range(G // C):
        r0 = c * C
        st = [state_ref[a] for a in heads]
        ws_qs = [jnp.dot(jnp.concatenate([uw[a][r0:r0 + C, D:], q_dec[a][r0:r0 + C]], axis=0).astype(BF16),
                         st[a].astype(BF16), preferred_element_type=F32) for a in heads]
        for a in heads:
            vnew_ref[a, r0:r0 + C, :] = (uw[a][r0:r0 + C, :D] - ws_qs[a][:C]).astype(BF16)
        vn = [vnew_ref[a] for a in heads]
        for a in heads:
            outs[a].append(ws_qs[a][C:] + jnp.dot(attn[a][r0:r0 + C], vn[a], preferred_element_type=F32))
        for a in heads:
            kt_c = jnp.where(ct == c, k_dec_t[a], 0.0).astype(BF16)
            e_c = jnp.concatenate([egl[a][r0:r0 + C]] * (D // C), axis=0)
            state_ref[a] = st[a] * e_c + jnp.dot(kt_c, vn[a], preferred_element_type=F32)

    for a in heads:
        o = jnp.concatenate(outs[a], axis=0)
        o = o * lax.rsqrt(jnp.mean(o * o, axis=-1, keepdims=True) + NORM_EPS) * onw_ref[...]
        z = z_ref[0, :, sls[a]].astype(F32)
        o_ref[0, :, sls[a]] = (o * (z * jax.nn.sigmoid(z))).astype(o_ref.dtype)


def _deltanet(p3, small3, conv_w, a_log_row, dt_row, onorm_w, n_heads, col_q, col_k, col_v, col_z,
              heads_per_step=8):
    b, s, _ = p3.shape
    G, D, NH = DN_GROUP, LANES, heads_per_step
    assert n_heads % NH == 0 and all(c % NH == 0 for c in (col_q, col_k, col_v, col_z))
    kw = conv_w.shape[0]
    blk = lambda off: pl.BlockSpec((1, G, NH * D), lambda bi, hi, gi: (bi, gi, off // NH + hi))
    cw = lambda off: pl.BlockSpec((kw, NH * D), lambda bi, hi, gi: (0, off // NH + hi))
    row_spec = pl.BlockSpec((1, LANES), lambda bi, hi, gi: (0, 0))
    return pl.pallas_call(
        functools.partial(_deltanet_kernel, n_heads=n_heads, heads_per_step=NH),
        grid=(b, n_heads // NH, s // G),
        in_specs=[blk(col_q), blk(col_k), blk(col_v), blk(col_z),
                  pl.BlockSpec((1, G, LANES), lambda bi, hi, gi: (bi, gi, 0)),
                  cw(0), cw(n_heads), cw(2 * n_heads), row_spec, row_spec, row_spec],
        out_specs=pl.BlockSpec((1, G, NH * D), lambda bi, hi, gi: (bi, gi, hi)),
        out_shape=jax.ShapeDtypeStruct((b, s, n_heads * D), BF16),
        scratch_shapes=[pltpu.VMEM((NH, D, D), F32), pltpu.VMEM((3, DN_CONV_HALO, NH * D), F32),
                        pltpu.VMEM((NH, G, D), BF16), pltpu.VMEM((LANES, G), F32)],
        compiler_params=_cparams(("parallel", "parallel", "arbitrary")),
        name="gated_deltanet",
    )(p3, p3, p3, p3, small3, conv_w, conv_w, conv_w, a_log_row, dt_row, onorm_w)


def _t5_bucket_f32(dist):
    dist = jnp.maximum(dist, 0)
    max_exact = REL_BUCKETS // 2
    d = jnp.maximum(dist, 1).astype(F32)
    large = max_exact + (jnp.log(d / max_exact) / math.log(REL_MAX_DIST / max_exact)
                         * (REL_BUCKETS - max_exact)).astype(jnp.int32)
    large = jnp.minimum(large, REL_BUCKETS - 1)
    return jnp.where(dist < max_exact, dist, large)


def _n_near_tiles():
    return -(-(REL_MAX_DIST + MOBA_BLOCK - 1) // MOBA_BLOCK)


def _moba_kernel(rb_ref, q_ref, k_ref, v_ref, z_ref, o_ref, kmean_ref, bias_ref, etab_ref, xt_ref, qa_ref,
                 s0_ref, s1_ref, wide_ref, *, n_near, nbp, group, qblocks):
    hh = pl.program_id(0)
    bb = pl.program_id(1)
    st = pl.program_id(2)
    BLK, D, U, QB = MOBA_BLOCK, LANES, group, qblocks
    R = QB * BLK
    nb = k_ref.shape[1] // BLK
    col_shift = nbp

    @pl.when((bb == 0) & (st == 0))
    def _():
        qr = lax.broadcasted_iota(jnp.int32, (BLK, BLK), 0)
        kc = lax.broadcasted_iota(jnp.int32, (BLK, BLK), 1)
        for t in range(n_near):
            dist = t * BLK + qr - kc
            bucket = _t5_bucket_f32(dist)
            val = jnp.full((BLK, BLK), rb_ref[0, hh], F32)
            for bk in range(1, REL_BUCKETS):
                val = jnp.where(bucket == bk, rb_ref[bk, hh], val)
            if t == 0:
                val = jnp.where(dist >= 0, val, NEG_INF)
            bias_ref[t] = val
        bias_ref[n_near] = jnp.full((BLK, BLK), rb_ref[REL_BUCKETS - 1, hh], F32)
        lane = lax.broadcasted_iota(jnp.int32, (16, D), 1)
        for j in range(nb):
            etab_ref[j] = jnp.where((lane == j) | (lane == col_shift), 1.0, 0.0).astype(BF16)

    @pl.when(st == 0)
    def _():
        def norms(j, carry):
            k2max, q2max = carry
            rows = pl.ds(pl.multiple_of(j * BLK, BLK), BLK)
            kj = k_ref[0, rows, :].astype(F32)
            qj = q_ref[0, rows, :].astype(F32)
            kmean_ref[pl.ds(j, 1), :] = jnp.mean(kj, axis=0, keepdims=True)
            return (jnp.maximum(k2max, jnp.max(jnp.sum(kj * kj, axis=1, keepdims=True))),
                    jnp.maximum(q2max, jnp.max(jnp.sum(qj * qj, axis=1, keepdims=True))))
        k2max, q2max = lax.fori_loop(0, nb, norms, (jnp.float32(0.0), jnp.float32(0.0)))
        rb_abs = jnp.abs(rb_ref[0, hh])
        for bk in range(1, REL_BUCKETS):
            rb_abs = jnp.maximum(rb_abs, jnp.abs(rb_ref[bk, hh]))
        bound = jnp.sqrt(q2max * k2max) * 1.001 + rb_abs + 1e-3
        wide_ref[0] = jnp.where(bound > MOBA_SHIFT_SLACK, 1, 0).astype(jnp.int32)

        kmean16 = kmean_ref[...].astype(BF16)
        jrow = lax.broadcasted_iota(jnp.int32, (nb, BLK), 0)
        jrow_f = jrow.astype(F32)

        def visibility(it, carry):
            for u in range(U):
                ib = it * U + u
                rows = pl.ds(pl.multiple_of(ib * BLK, BLK), BLK)
                gate = lax.dot_general(kmean16, q_ref[0, rows, :], NT_DIMS, preferred_element_type=F32)
                past = jrow < ib
                sc = jnp.where(past, gate, NEG_INF)
                visible = jrow == ib
                for _ in range(MOBA_TOPK):
                    top = jnp.max(sc, axis=0, keepdims=True)
                    first = jnp.min(jnp.where(sc == top, jrow_f, float(nb)), axis=0, keepdims=True)
                    pick = jrow_f == first
                    visible = visible | (pick & past)
                    sc = jnp.where(pick, jnp.finfo(F32).min, sc)
                cols = jnp.concatenate([jnp.where(visible, 0.0, NEG_INF), jnp.zeros((D - nb, BLK), F32)], axis=0)
                xt_ref[rows, :] = cols.T.astype(BF16)
            return carry
        lax.fori_loop(0, nb // U, visibility, 0)

    r0 = pl.multiple_of(st * R, R)
    qa_ref[:, 0:D] = q_ref[0, pl.ds(r0, R), :]
    qa_ref[:, D:2 * D] = xt_ref[pl.ds(r0, R), :]
    i_first = st * QB
    n_groups = (i_first + QB - 1) // U + 1

    def group_scores(g):
        j0 = g * U
        row0 = pl.multiple_of(j0 * BLK, U * BLK)
        e_rows = []
        for u in range(U):
            e_rows += [etab_ref[j0 + u]] * (BLK // 16)
        ka = jnp.concatenate([k_ref[0, pl.ds(row0, U * BLK), :], jnp.concatenate(e_rows, axis=0)], axis=1)
        s = lax.dot_general(qa_ref[...], ka, NT_DIMS, preferred_element_type=F32)
        bias = jnp.concatenate(
            [jnp.concatenate([bias_ref[jnp.clip(i_first + a - (j0 + u), 0, n_near)] for u in range(U)], axis=1)
             for a in range(QB)], axis=0)
        return s + bias

    def lane_chunks(s):
        return [s[:, c * D:(c + 1) * D] for c in range(s.shape[1] // D)]

    @pl.when(wide_ref[0] == 1)
    def _():
        def max_step(g, mm):
            for c in lane_chunks(group_scores(g)):
                mm = jnp.maximum(mm, c)
            return mm
        mm = lax.fori_loop(0, n_groups, max_step, jnp.full((R, D), jnp.finfo(F32).min, F32))
        m = jnp.max(mm, axis=1, keepdims=True)
        lane_q = lax.broadcasted_iota(jnp.int32, (R, D), 1)
        cols = xt_ref[pl.ds(r0, R), :].astype(F32)
        qa_ref[:, D:2 * D] = jnp.where(lane_q == col_shift, -m, cols).astype(BF16)

    def consume(g, s, carry):
        acc, lsum = carry
        p = jnp.exp(s)
        for c in lane_chunks(p):
            lsum = lsum + c
        vg = v_ref[0, pl.ds(pl.multiple_of(g * (U * BLK), U * BLK), U * BLK), :]
        return acc + jnp.dot(p.astype(BF16), vg, preferred_element_type=F32), lsum

    def stage(g, cur_ref, nxt_ref, carry):
        nxt_ref[...] = group_scores(g + 1)
        return consume(g, cur_ref[...], carry)

    def two_stages(t, carry):
        carry = stage(2 * t, s0_ref, s1_ref, carry)
        return stage(2 * t + 1, s1_ref, s0_ref, carry)

    s0_ref[...] = group_scores(0)
    zero = jnp.zeros((R, D), F32)
    last = n_groups - 1
    carry = lax.fori_loop(0, last // 2, two_stages, (zero, zero))

    def odd_tail(c):
        c = stage(last - 1, s0_ref, s1_ref, c)
        return consume(last, s1_ref[...], c)

    acc, lsum = lax.cond(last % 2 == 1, odd_tail, lambda c: consume(last, s0_ref[...], c), carry)

    out = acc / jnp.sum(lsum, axis=1, keepdims=True)
    z = z_ref[0].astype(F32)
    o_ref[0] = (out * (z * jax.nn.sigmoid(z))).astype(o_ref.dtype)


def _moba(rel_bias, p3, n_heads, col_q, col_k, col_v, col_z, group=4, qblocks=2):
    b, s, _ = p3.shape
    BLK, D = MOBA_BLOCK, LANES
    nb = s // BLK
    nbp = -(-nb // 16) * 16
    assert nbp + 16 <= D, "block-visibility columns must fit in the spare contraction columns"
    assert nb % group == 0 and nb % qblocks == 0, "key blocks are visited in whole groups"
    n_near = _n_near_tiles()
    rows = qblocks * BLK
    seq = lambda off: pl.BlockSpec((1, s, D), lambda hi, bi, i: (bi, 0, off + hi))
    blk = lambda off: pl.BlockSpec((1, rows, D), lambda hi, bi, i: (bi, i, off + hi))
    return pl.pallas_call(
        functools.partial(_moba_kernel, n_near=n_near, nbp=nbp, group=group, qblocks=qblocks),
        grid=(n_heads, b, nb // qblocks),
        in_specs=[pl.BlockSpec(memory_space=pltpu.SMEM), seq(col_q), seq(col_k), seq(col_v), blk(col_z)],
        out_specs=blk(0),
        out_shape=jax.ShapeDtypeStruct((b, s, n_heads * D), BF16),
        scratch_shapes=[pltpu.VMEM((nb, D), F32), pltpu.VMEM((n_near + 1, BLK, BLK), F32),
                        pltpu.VMEM((nb, 16, D), BF16), pltpu.VMEM((s, D), BF16), pltpu.VMEM((rows, 2 * D), BF16),
                        pltpu.VMEM((rows, group * BLK), F32), pltpu.VMEM((rows, group * BLK), F32),
                        pltpu.SMEM((1,), jnp.int32)],
        compiler_params=_cparams(("arbitrary", "arbitrary", "arbitrary")),
        name="moba_attention",
    )(rel_bias, p3, p3, p3, p3)


def _merge_kernel(ya_ref, yb_ref, g0_ref, g1_ref, x_ref, w0_ref, w1_ref, wo_ref, b0_ref, b1_ref, nw_ref, o_ref):
    pa = jnp.dot(ya_ref[...], w0_ref[...], preferred_element_type=F32)
    pb = jnp.dot(yb_ref[...], w1_ref[...], preferred_element_type=F32)
    g0 = jax.nn.sigmoid(g0_ref[...].astype(F32) + b0_ref[...])
    g1 = jax.nn.sigmoid(g1_ref[...].astype(F32) + b1_ref[...])
    merged = (g0 * pa + g1 * pb).astype(BF16)
    out = jnp.dot(merged, wo_ref[...], preferred_element_type=F32)
    y = out * lax.rsqrt(jnp.mean(out * out, axis=-1, keepdims=True) + NORM_EPS) * nw_ref[...]
    o_ref[...] = x_ref[...] + y


def _merge(ya, yb, p2, x2d, w0, w1, wo, b_gate, norm_w, col_g0, col_g1, tm=512):
    t, d = x2d.shape
    row = lambda c: pl.BlockSpec((tm, d), lambda i: (i, c))
    full = pl.BlockSpec((d, d), lambda i: (0, 0))
    vec = lambda c: pl.BlockSpec((1, d), lambda i: (0, c))
    return pl.pallas_call(
        _merge_kernel,
        grid=(t // tm,),
        in_specs=[row(0), row(0), row(col_g0), row(col_g1), row(0), full, full, full, vec(0), vec(1), vec(0)],
        out_specs=row(0),
        out_shape=jax.ShapeDtypeStruct((t, d), F32),
        compiler_params=_cparams(("parallel",)),
        name="merge_out",
    )(ya, yb, p2, p2, x2d, w0, w1, wo, b_gate.reshape(1, -1), b_gate.reshape(1, -1), norm_w.reshape(1, d))


def kernel(x, norm_pre_w, w_in, b_gate, conv_w, dn_a_log, dn_dt_bias, dn_onorm_w, rel_bias, w_branch, w_out,
           norm_post_w):
    b, s, d = x.shape
    n_heads = dn_a_log.shape[0]
    hd = dn_onorm_w.shape[0]
    width = n_heads * hd
    assert hd == LANES and rel_bias.shape[1] == n_heads and width == d
    assert s % MOBA_BLOCK == 0 and s % DN_GROUP == 0 and 2 * n_heads <= LANES
    t = b * s

    c_beta = 4 * width
    c_qb = c_beta + 2 * n_heads
    w16 = w_in.astype(BF16)
    w_main = jnp.concatenate([w16[:, :c_beta], w16[:, c_qb:]], axis=1)
    w_small = jnp.pad(w16[:, c_beta:c_qb], ((0, 0), (0, LANES - 2 * n_heads)))
    nblk = width // LANES
    col = lambda k: k * nblk

    x2d = x.reshape(t, d)
    hn = _rmsnorm(x2d, norm_pre_w)
    p2 = _matmul(hn, w_main, BF16, tm=1024, tn=width, name="in_proj_main", scaled_tile=4, scale=hd ** -0.5)
    small = _matmul(hn, w_small, F32, tm=1024, tn=LANES, name="in_proj_small")
    p3 = p2.reshape(b, s, -1)

    pad = (0, LANES - 2 * n_heads)
    a_log_row = jnp.pad(jnp.concatenate([jnp.zeros_like(dn_a_log), dn_a_log]), pad).reshape(1, LANES).astype(F32)
    dt_row = jnp.pad(jnp.concatenate([jnp.zeros_like(dn_dt_bias), dn_dt_bias]), pad).reshape(1, LANES).astype(F32)
    ya = _deltanet(p3, small.reshape(b, s, LANES), conv_w.astype(F32), a_log_row, dt_row,
                   dn_onorm_w.reshape(1, hd).astype(F32), n_heads, col(0), col(1), col(2), col(3))
    yb = _moba(rel_bias.astype(F32), p3, n_heads, col(4), col(5), col(6), col(7))

    out = _merge(ya.reshape(t, d), yb.reshape(t, d), p2, x2d,
                 w_branch[0].astype(BF16), w_branch[1].astype(BF16), w_out.astype(BF16),
                 b_gate.astype(F32), norm_post_w.astype(F32), 8, 9)
    return out.reshape(b, s, d)
```

```python
import functools
import math

import jax
import jax.numpy as jnp
from jax import lax
from jax.experimental import pallas as pl
from jax.experimental.pallas import tpu as pltpu

F32 = jnp.float32
BF16 = jnp.bfloat16

LANES = 128
DN_CHUNK = 64
DN_GROUP = 128
DN_CONV_HALO = 8
MOBA_BLOCK = 256
MOBA_TOPK = 3
MOBA_SHIFT_SLACK = 40.0
REL_BUCKETS = 32
REL_MAX_DIST = 2048
NORM_EPS = 1e-6
NEG_INF = -1e30
VMEM_LIMIT = 56 * 1024 * 1024
NT_DIMS = (((1,), (1,)), ((), ()))


def _cparams(sem):
    return pltpu.CompilerParams(dimension_semantics=sem, vmem_limit_bytes=VMEM_LIMIT)


def _rmsnorm_kernel(x_ref, w_ref, o_ref):
    x = x_ref[...]
    y = x * lax.rsqrt(jnp.mean(x * x, axis=-1, keepdims=True) + NORM_EPS)
    o_ref[...] = (y * w_ref[...]).astype(o_ref.dtype)


def _rmsnorm(x2d, w, tm=1024):
    t, d = x2d.shape
    return pl.pallas_call(
        _rmsnorm_kernel,
        grid=(t // tm,),
        in_specs=[pl.BlockSpec((tm, d), lambda i: (i, 0)), pl.BlockSpec((1, d), lambda i: (0, 0))],
        out_specs=pl.BlockSpec((tm, d), lambda i: (i, 0)),
        out_shape=jax.ShapeDtypeStruct((t, d), BF16),
        compiler_params=_cparams(("parallel",)),
        name="rmsnorm_pre",
    )(x2d, w.reshape(1, d))


def _mm_kernel(h_ref, w_ref, o_ref, *, scaled_tile, scale):
    acc = jnp.dot(h_ref[...], w_ref[...], preferred_element_type=F32)
    if scaled_tile is not None:
        acc = acc * jnp.where(pl.program_id(1) == scaled_tile, scale, 1.0).astype(F32)
    o_ref[...] = acc.astype(o_ref.dtype)


def _matmul(h, w, out_dtype, tm, tn, name, scaled_tile=None, scale=1.0):
    t, d = h.shape
    n = w.shape[1]
    return pl.pallas_call(
        functools.partial(_mm_kernel, scaled_tile=scaled_tile, scale=scale),
        grid=(t // tm, n // tn),
        in_specs=[pl.BlockSpec((tm, d), lambda i, j: (i, 0)), pl.BlockSpec((d, tn), lambda i, j: (0, j))],
        out_specs=pl.BlockSpec((tm, tn), lambda i, j: (i, j)),
        out_shape=jax.ShapeDtypeStruct((t, n), out_dtype),
        compiler_params=_cparams(("parallel", "arbitrary")),
        name=name,
    )(h, w)


def _deltanet_kernel(q_ref, k_ref, v_ref, z_ref, sm_ref, cq_ref, ck_ref, cv_ref, alog_ref, dtb_ref, onw_ref,
                     o_ref, state_ref, halo_ref, vnew_ref, gct_ref, *, n_heads, heads_per_step):
    hg = pl.program_id(1)
    g = pl.program_id(2)
    G, C, D, NH = DN_GROUP, DN_CHUNK, LANES, heads_per_step
    lc = C.bit_length() - 1

    @pl.when(g == 0)
    def _():
        state_ref[...] = jnp.zeros_like(state_ref)
        halo_ref[...] = jnp.zeros_like(halo_ref)

    def conv_silu(x_ref, slot, w_ref):
        cur = x_ref[0].astype(F32)
        ext = jnp.concatenate([halo_ref[slot], cur], axis=0)
        w = w_ref[...]
        kw = w.shape[0]
        base = DN_CONV_HALO - (kw - 1)
        y = ext[base:base + G] * w[0:1]
        for kk in range(1, kw):
            y = y + ext[base + kk:base + kk + G] * w[kk:kk + 1]
        halo_ref[slot] = cur[G - DN_CONV_HALO:G]
        return y * jax.nn.sigmoid(y)

    def l2n(x):
        return x * lax.rsqrt(jnp.sum(x * x, axis=-1, keepdims=True) + NORM_EPS)

    qf = conv_silu(q_ref, 0, cq_ref)
    kf = conv_silu(k_ref, 1, ck_ref)
    vf = conv_silu(v_ref, 2, cv_ref)

    cs = sm_ref[0]
    lane = lax.broadcasted_iota(jnp.int32, (G, LANES), 1)
    row = lax.broadcasted_iota(jnp.int32, (G, LANES), 0)
    beta_all = jax.nn.sigmoid(cs)
    xg = cs + dtb_ref[...]
    softplus = jnp.maximum(xg, 0.0) + jnp.log1p(jnp.exp(-jnp.abs(xg)))
    g_all = -jnp.exp(alog_ref[...]) * softplus
    pos = row & (C - 1)
    gc_all = g_all
    shift = 1
    while shift < C:
        gc_all = gc_all + jnp.where(pos >= shift, pltpu.roll(gc_all, shift, 0), 0.0)
        shift *= 2
    gl_all = jnp.concatenate([jnp.broadcast_to(gc_all[c * C + C - 1:c * C + C], (C, LANES))
                              for c in range(G // C)], axis=0)
    gct_ref[...] = gc_all.T

    ri = lax.broadcasted_iota(jnp.int32, (G, G), 0)
    ci = lax.broadcasted_iota(jnp.int32, (G, G), 1)
    same = (ri >> lc) == (ci >> lc)
    tri_incl = same & (ri >= ci)
    tri_strict = same & (ri > ci)
    eye = jnp.where(ri == ci, 1.0, 0.0)
    ct = lax.broadcasted_iota(jnp.int32, (D, G), 1) >> lc

    def pair_mask(s):
        ls = s.bit_length() - 1
        return ((ri >> (ls + 1)) == (ci >> (ls + 1))) & (((ri >> ls) & 1) == 1) & (((ci >> ls) & 1) == 0)

    pair_masks = {}
    s = 1
    while s < C:
        pair_masks[s] = pair_mask(s)
        s *= 2

    heads = range(NH)
    sls = [slice(hl * D, (hl + 1) * D) for hl in heads]
    hidx = [hg * NH + hl for hl in heads]

    def col_of(x, lane_idx):
        return jnp.sum(jnp.where(lane == lane_idx, x, 0.0), axis=1, keepdims=True)

    q = [l2n(qf[:, sl]) * (D ** -0.5) for sl in sls]
    k = [l2n(kf[:, sl]) for sl in sls]
    beta_col = [col_of(beta_all, h) for h in hidx]
    gc_col = [col_of(gc_all, h + n_heads) for h in hidx]
    gl_col = [col_of(gl_all, h + n_heads) for h in hidx]
    gc_row = [gct_ref[pl.ds(h + n_heads, 1), :] for h in hidx]
    decay = [jnp.where(tri_incl, jnp.exp(jnp.where(tri_incl, gc_col[a] - gc_row[a], 0.0)), 0.0) for a in heads]
    kb = [k[a] * beta_col[a] for a in heads]
    vb = [vf[:, sls[a]] * beta_col[a] for a in heads]
    k16 = [x.astype(BF16) for x in k]
    kk = [lax.dot_general(kb[a].astype(BF16), k16[a], NT_DIMS, preferred_element_type=F32) for a in heads]
    qk = [lax.dot_general(q[a].astype(BF16), k16[a], NT_DIMS, preferred_element_type=F32) for a in heads]
    lower = [jnp.where(tri_strict, kk[a] * decay[a], 0.0) for a in heads]
    attn = [jnp.where(tri_incl, qk[a] * decay[a], 0.0).astype(BF16) for a in heads]

    tinv = [eye - jnp.where(pair_masks[1], lower[a], 0.0) for a in heads]
    s = 2
    while s < C:
        t16 = [x.astype(BF16) for x in tinv]
        md = [jnp.dot(jnp.where(pair_masks[s], lower[a], 0.0).astype(BF16), t16[a], preferred_element_type=F32)
              for a in heads]
        tinv = [tinv[a] - jnp.dot(t16[a], md[a].astype(BF16), preferred_element_type=F32) for a in heads]
        s *= 2

    egc = [jnp.exp(x) for x in gc_col]
    uw = [jnp.dot(tinv[a].astype(BF16), jnp.concatenate([vb[a], kb[a] * egc[a]], axis=1).astype(BF16),
                  preferred_element_type=F32) for a in heads]
    q_dec = [q[a] * egc[a] for a in heads]
    k_dec_t = [(k[a] * jnp.exp(gl_col[a] - gc_col[a])).T for a in heads]
    egl = [jnp.exp(x) for x in gl_col]

    vnew_ref[...] = jnp.zeros_like(vnew_ref)
    outs = [[] for _ in heads]
    for c in range(G // C):
        r0 = c * C
        st = [state_ref[a] for a in heads]
        ws_qs = [jnp.dot(jnp.concatenate([uw[a][r0:r0 + C, D:], q_dec[a][r0:r0 + C]], axis=0).astype(BF16),
                         st[a].astype(BF16), preferred_element_type=F32) for a in heads]
        for a in heads:
            vnew_ref[a, r0:r0 + C, :] = (uw[a][r0:r0 + C, :D] - ws_qs[a][:C]).astype(BF16)
        vn = [vnew_ref[a] for a in heads]
        for a in heads:
            outs[a].append(ws_qs[a][C:] + jnp.dot(attn[a][r0:r0 + C], vn[a], preferred_element_type=F32))
        for a in heads:
            kt_c = jnp.where(ct == c, k_dec_t[a], 0.0).astype(BF16)
            e_c = jnp.concatenate([egl[a][r0:r0 + C]] * (D // C), axis=0)
            state_ref[a] = st[a] * e_c + jnp.dot(kt_c, vn[a], preferred_element_type=F32)

    for a in heads:
        o = jnp.concatenate(outs[a], axis=0)
        o = o * lax.rsqrt(jnp.mean(o * o, axis=-1, keepdims=True) + NORM_EPS) * onw_ref[...]
        z = z_ref[0, :, sls[a]].astype(F32)
        o_ref[0, :, sls[a]] = (o * (z * jax.nn.sigmoid(z))).astype(o_ref.dtype)


def _deltanet(p3, small3, conv_w, a_log_row, dt_row, onorm_w, n_heads, col_q, col_k, col_v, col_z,
              heads_per_step=8):
    b, s, _ = p3.shape
    G, D, NH = DN_GROUP, LANES, heads_per_step
    assert n_heads % NH == 0 and all(c % NH == 0 for c in (col_q, col_k, col_v, col_z))
    kw = conv_w.shape[0]
    blk = lambda off: pl.BlockSpec((1, G, NH * D), lambda bi, hi, gi: (bi, gi, off // NH + hi))
    cw = lambda off: pl.BlockSpec((kw, NH * D), lambda bi, hi, gi: (0, off // NH + hi))
    row_spec = pl.BlockSpec((1, LANES), lambda bi, hi, gi: (0, 0))
    return pl.pallas_call(
        functools.partial(_deltanet_kernel, n_heads=n_heads, heads_per_step=NH),
        grid=(b, n_heads // NH, s // G),
        in_specs=[blk(col_q), blk(col_k), blk(col_v), blk(col_z),
                  pl.BlockSpec((1, G, LANES), lambda bi, hi, gi: (bi, gi, 0)),
                  cw(0), cw(n_heads), cw(2 * n_heads), row_spec, row_spec, row_spec],
        out_specs=pl.BlockSpec((1, G, NH * D), lambda bi, hi, gi: (bi, gi, hi)),
        out_shape=jax.ShapeDtypeStruct((b, s, n_heads * D), BF16),
        scratch_shapes=[pltpu.VMEM((NH, D, D), F32), pltpu.VMEM((3, DN_CONV_HALO, NH * D), F32),
                        pltpu.VMEM((NH, G, D), BF16), pltpu.VMEM((LANES, G), F32)],
        compiler_params=_cparams(("parallel", "parallel", "arbitrary")),
        name="gated_deltanet",
    )(p3, p3, p3, p3, small3, conv_w, conv_w, conv_w, a_log_row, dt_row, onorm_w)


def _t5_bucket_f32(dist):
    dist = jnp.maximum(dist, 0)
    max_exact = REL_BUCKETS // 2
    d = jnp.maximum(dist, 1).astype(F32)
    large = max_exact + (jnp.log(d / max_exact) / math.log(REL_MAX_DIST / max_exact)
                         * (REL_BUCKETS - max_exact)).astype(jnp.int32)
    large = jnp.minimum(large, REL_BUCKETS - 1)
    return jnp.where(dist < max_exact, dist, large)


def _n_near_tiles():
    return -(-(REL_MAX_DIST + MOBA_BLOCK - 1) // MOBA_BLOCK)


def _moba_kernel(rb_ref, q_ref, k_ref, v_ref, z_ref, o_ref, kmean_ref, bias_ref, etab_ref, xt_ref, qa_ref,
                 s0_ref, s1_ref, wide_ref, *, n_near, nbp, group, qblocks):
    hh = pl.program_id(0)
    bb = pl.program_id(1)
    st = pl.program_id(2)
    BLK, D, U, QB = MOBA_BLOCK, LANES, group, qblocks
    R = QB * BLK
    nb = k_ref.shape[1] // BLK
    col_shift = nbp

    @pl.when((bb == 0) & (st == 0))
    def _():
        qr = lax.broadcasted_iota(jnp.int32, (BLK, BLK), 0)
        kc = lax.broadcasted_iota(jnp.int32, (BLK, BLK), 1)
        for t in range(n_near):
            dist = t * BLK + qr - kc
            bucket = _t5_bucket_f32(dist)
            val = jnp.full((BLK, BLK), rb_ref[0, hh], F32)
            for bk in range(1, REL_BUCKETS):
                val = jnp.where(bucket == bk, rb_ref[bk, hh], val)
            if t == 0:
                val = jnp.where(dist >= 0, val, NEG_INF)
            bias_ref[t] = val
        bias_ref[n_near] = jnp.full((BLK, BLK), rb_ref[REL_BUCKETS - 1, hh], F32)
        lane = lax.broadcasted_iota(jnp.int32, (16, D), 1)
        for j in range(nb):
            etab_ref[j] = jnp.where((lane == j) | (lane == col_shift), 1.0, 0.0).astype(BF16)

    @pl.when(st == 0)
    def _():
        def norms(j, carry):
            k2max, q2max = carry
            rows = pl.ds(pl.multiple_of(j * BLK, BLK), BLK)
            kj = k_ref[0, rows, :].astype(F32)
            qj = q_ref[0, rows, :].astype(F32)
            kmean_ref[pl.ds(j, 1), :] = jnp.mean(kj, axis=0, keepdims=True)
            return (jnp.maximum(k2max, jnp.max(jnp.sum(kj * kj, axis=1, keepdims=True))),
                    jnp.maximum(q2max, jnp.max(jnp.sum(qj * qj, axis=1, keepdims=True))))
        k2max, q2max = lax.fori_loop(0, nb, norms, (jnp.float32(0.0), jnp.float32(0.0)))
        rb_abs = jnp.abs(rb_ref[0, hh])
        for bk in range(1, REL_BUCKETS):
            rb_abs = jnp.maximum(rb_abs, jnp.abs(rb_ref[bk, hh]))
        bound = jnp.sqrt(q2max * k2max) * 1.001 + rb_abs + 1e-3
        wide_ref[0] = jnp.where(bound > MOBA_SHIFT_SLACK, 1, 0).astype(jnp.int32)

        kmean16 = kmean_ref[...].astype(BF16)
        jrow = lax.broadcasted_iota(jnp.int32, (nb, BLK), 0)
        jrow_f = jrow.astype(F32)

        def visibility(it, carry):
            for u in range(U):
                ib = it * U + u
                rows = pl.ds(pl.multiple_of(ib * BLK, BLK), BLK)
                gate = lax.dot_general(kmean16, q_ref[0, rows, :], NT_DIMS, preferred_element_type=F32)
                past = jrow < ib
                sc = jnp.where(past, gate, NEG_INF)
                visible = jrow == ib
                for _ in range(MOBA_TOPK):
                    top = jnp.max(sc, axis=0, keepdims=True)
                    first = jnp.min(jnp.where(sc == top, jrow_f, float(nb)), axis=0, keepdims=True)
                    pick = jrow_f == first
                    visible = visible | (pick & past)
                    sc = jnp.where(pick, jnp.finfo(F32).min, sc)
                cols = jnp.concatenate([jnp.where(visible, 0.0, NEG_INF), jnp.zeros((D - nb, BLK), F32)], axis=0)
                xt_ref[rows, :] = cols.T.astype(BF16)
            return carry
        lax.fori_loop(0, nb // U, visibility, 0)

    r0 = pl.multiple_of(st * R, R)
    qa_ref[:, 0:D] = q_ref[0, pl.ds(r0, R), :]
    qa_ref[:, D:2 * D] = xt_ref[pl.ds(r0, R), :]
    i_first = st * QB
    n_groups = (i_first + QB - 1) // U + 1

    def group_scores(g):
        j0 = g * U
        row0 = pl.multiple_of(j0 * BLK, U * BLK)
        e_rows = []
        for u in range(U):
            e_rows += [etab_ref[j0 + u]] * (BLK // 16)
        ka = jnp.concatenate([k_ref[0, pl.ds(row0, U * BLK), :], jnp.concatenate(e_rows, axis=0)], axis=1)
        s = lax.dot_general(qa_ref[...], ka, NT_DIMS, preferred_element_type=F32)
        bias = jnp.concatenate(
            [jnp.concatenate([bias_ref[jnp.clip(i_first + a - (j0 + u), 0, n_near)] for u in range(U)], axis=1)
             for a in range(QB)], axis=0)
        return s + bias

    def lane_chunks(s):
        return [s[:, c * D:(c + 1) * D] for c in range(s.shape[1] // D)]

    @pl.when(wide_ref[0] == 1)
    def _():
        def max_step(g, mm):
            for c in lane_chunks(group_scores(g)):
                mm = jnp.maximum(mm, c)
            return mm
        mm = lax.fori_loop(0, n_groups, max_step, jnp.full((R, D), jnp.finfo(F32).min, F32))
        m = jnp.max(mm, axis=1, keepdims=True)
        lane_q = lax.broadcasted_iota(jnp.int32, (R, D), 1)
        cols = xt_ref[pl.ds(r0, R), :].astype(F32)
        qa_ref[:, D:2 * D] = jnp.where(lane_q == col_shift, -m, cols).astype(BF16)

    def consume(g, s, carry):
        acc, lsum = carry
        p = jnp.exp(s)
        for c in lane_chunks(p):
            lsum = lsum + c
        vg = v_ref[0, pl.ds(pl.multiple_of(g * (U * BLK), U * BLK), U * BLK), :]
        return acc + jnp.dot(p.astype(BF16), vg, preferred_element_type=F32), lsum

    def stage(g, cur_ref, nxt_ref, carry):
        nxt_ref[...] = group_scores(g + 1)
        return consume(g, cur_ref[...], carry)

    def two_stages(t, carry):
        carry = stage(2 * t, s0_ref, s1_ref, carry)
        return stage(2 * t + 1, s1_ref, s0_ref, carry)

    s0_ref[...] = group_scores(0)
    zero = jnp.zeros((R, D), F32)
    last = n_groups - 1
    carry = lax.fori_loop(0, last // 2, two_stages, (zero, zero))

    def odd_tail(c):
        c = stage(last - 1, s0_ref, s1_ref, c)
        return consume(last, s1_ref[...], c)

    acc, lsum = lax.cond(last % 2 == 1, odd_tail, lambda c: consume(last, s0_ref[...], c), carry)

    out = acc / jnp.sum(lsum, axis=1, keepdims=True)
    z = z_ref[0].astype(F32)
    o_ref[0] = (out * (z * jax.nn.sigmoid(z))).astype(o_ref.dtype)


def _moba(rel_bias, p3, n_heads, col_q, col_k, col_v, col_z, group=4, qblocks=4):
    b, s, _ = p3.shape
    BLK, D = MOBA_BLOCK, LANES
    nb = s // BLK
    nbp = -(-nb // 16) * 16
    assert nbp + 16 <= D, "block-visibility columns must fit in the spare contraction columns"
    assert nb % group == 0 and nb % qblocks == 0, "key blocks are visited in whole groups"
    n_near = _n_near_tiles()
    rows = qblocks * BLK
    seq = lambda off: pl.BlockSpec((1, s, D), lambda hi, bi, i: (bi, 0, off + hi))
    blk = lambda off: pl.BlockSpec((1, rows, D), lambda hi, bi, i: (bi, i, off + hi))
    return pl.pallas_call(
        functools.partial(_moba_kernel, n_near=n_near, nbp=nbp, group=group, qblocks=qblocks),
        grid=(n_heads, b, nb // qblocks),
        in_specs=[pl.BlockSpec(memory_space=pltpu.SMEM), seq(col_q), seq(col_k), seq(col_v), blk(col_z)],
        out_specs=blk(0),
        out_shape=jax.ShapeDtypeStruct((b, s, n_heads * D), BF16),
        scratch_shapes=[pltpu.VMEM((nb, D), F32), pltpu.VMEM((n_near + 1, BLK, BLK), F32),
                        pltpu.VMEM((nb, 16, D), BF16), pltpu.VMEM((s, D), BF16), pltpu.VMEM((rows, 2 * D), BF16),
                        pltpu.VMEM((rows, group * BLK), F32), pltpu.VMEM((rows, group * BLK), F32),
                        pltpu.SMEM((1,), jnp.int32)],
        compiler_params=_cparams(("arbitrary", "arbitrary", "arbitrary")),
        name="moba_attention",
    )(rel_bias, p3, p3, p3, p3)


def _merge_kernel(ya_ref, yb_ref, g0_ref, g1_ref, x_ref, w0_ref, w1_ref, wo_ref, b0_ref, b1_ref, nw_ref, o_ref):
    pa = jnp.dot(ya_ref[...], w0_ref[...], preferred_element_type=F32)
    pb = jnp.dot(yb_ref[...], w1_ref[...], preferred_element_type=F32)
    g0 = jax.nn.sigmoid(g0_ref[...].astype(F32) + b0_ref[...])
    g1 = jax.nn.sigmoid(g1_ref[...].astype(F32) + b1_ref[...])
    merged = (g0 * pa + g1 * pb).astype(BF16)
    out = jnp.dot(merged, wo_ref[...], preferred_element_type=F32)
    y = out * lax.rsqrt(jnp.mean(out * out, axis=-1, keepdims=True) + NORM_EPS) * nw_ref[...]
    o_ref[...] = x_ref[...] + y


def _merge(ya, yb, p2, x2d, w0, w1, wo, b_gate, norm_w, col_g0, col_g1, tm=512):
    t, d = x2d.shape
    row = lambda c: pl.BlockSpec((tm, d), lambda i: (i, c))
    full = pl.BlockSpec((d, d), lambda i: (0, 0))
    vec = lambda c: pl.BlockSpec((1, d), lambda i: (0, c))
    return pl.pallas_call(
        _merge_kernel,
        grid=(t // tm,),
        in_specs=[row(0), row(0), row(col_g0), row(col_g1), row(0), full, full, full, vec(0), vec(1), vec(0)],
        out_specs=row(0),
        out_shape=jax.ShapeDtypeStruct((t, d), F32),
        compiler_params=_cparams(("parallel",)),
        name="merge_out",
    )(ya, yb, p2, p2, x2d, w0, w1, wo, b_gate.reshape(1, -1), b_gate.reshape(1, -1), norm_w.reshape(1, d))


def kernel(x, norm_pre_w, w_in, b_gate, conv_w, dn_a_log, dn_dt_bias, dn_onorm_w, rel_bias, w_branch, w_out,
           norm_post_w):
    b, s, d = x.shape
    n_heads = dn_a_log.shape[0]
    hd = dn_onorm_w.shape[0]
    width = n_heads * hd
    assert hd == LANES and rel_bias.shape[1] == n_heads and width == d
    assert s % MOBA_BLOCK == 0 and s % DN_GROUP == 0 and 2 * n_heads <= LANES
    t = b * s

    c_beta = 4 * width
    c_qb = c_beta + 2 * n_heads
    w16 = w_in.astype(BF16)
    w_main = jnp.concatenate([w16[:, :c_beta], w16[:, c_qb:]], axis=1)
    w_small = jnp.pad(w16[:, c_beta:c_qb], ((0, 0), (0, LANES - 2 * n_heads)))
    nblk = width // LANES
    col = lambda k: k * nblk

    x2d = x.reshape(t, d)
    hn = _rmsnorm(x2d, norm_pre_w)
    p2 = _matmul(hn, w_main, BF16, tm=1024, tn=width, name="in_proj_main", scaled_tile=4, scale=hd ** -0.5)
    small = _matmul(hn, w_small, F32, tm=1024, tn=LANES, name="in_proj_small")
    p3 = p2.reshape(b, s, -1)

    pad = (0, LANES - 2 * n_heads)
    a_log_row = jnp.pad(jnp.concatenate([jnp.zeros_like(dn_a_log), dn_a_log]), pad).reshape(1, LANES).astype(F32)
    dt_row = jnp.pad(jnp.concatenate([jnp.zeros_like(dn_dt_bias), dn_dt_bias]), pad).reshape(1, LANES).astype(F32)
    ya = _deltanet(p3, small.reshape(b, s, LANES), conv_w.astype(F32), a_log_row, dt_row,
                   dn_onorm_w.reshape(1, hd).astype(F32), n_heads, col(0), col(1), col(2), col(3))
    yb = _moba(rel_bias.astype(F32), p3, n_heads, col(4), col(5), col(6), col(7))

    out = _merge(ya.reshape(t, d), yb.reshape(t, d), p2, x2d,
                 w_branch[0].astype(BF16), w_branch[1].astype(BF16), w_out.astype(BF16),
                 b_gate.astype(F32), norm_post_w.astype(F32), 8, 9)
    return out.reshape(b, s, d)
```

```python
import functools
import math

import jax
import jax.numpy as jnp
from jax import lax
from jax.experimental import pallas as pl
from jax.experimental.pallas import tpu as pltpu

F32 = jnp.float32
BF16 = jnp.bfloat16

LANES = 128
DN_CHUNK = 64
DN_GROUP = 128
DN_CONV_HALO = 8
MOBA_BLOCK = 256
MOBA_TOPK = 3
MOBA_SHIFT_SLACK = 40.0
REL_BUCKETS = 32
REL_MAX_DIST = 2048
NORM_EPS = 1e-6
NEG_INF = -1e30
VMEM_LIMIT = 56 * 1024 * 1024
NT_DIMS = (((1,), (1,)), ((), ()))


def _cparams(sem):
    return pltpu.CompilerParams(dimension_semantics=sem, vmem_limit_bytes=VMEM_LIMIT)


def _in_proj_kernel(x_ref, nw_ref, w_ref, cs_ref, ws_ref, o_ref, sm_ref, h_ref):
    @pl.when(pl.program_id(1) == 0)
    def _():
        x = x_ref[...]
        y = x * lax.rsqrt(jnp.mean(x * x, axis=-1, keepdims=True) + NORM_EPS)
        h = (y * nw_ref[...]).astype(BF16)
        h_ref[...] = h
        sm_ref[...] = jnp.dot(h, ws_ref[...], preferred_element_type=F32)

    acc = jnp.dot(h_ref[...], w_ref[...], preferred_element_type=F32)
    o_ref[...] = (acc * cs_ref[...]).astype(o_ref.dtype)


def _in_proj(x2d, norm_w, w_main, col_scale, w_small, tm=1024, tn=2048):
    t, d = x2d.shape
    n = w_main.shape[1]
    ns = w_small.shape[1]
    return pl.pallas_call(
        _in_proj_kernel,
        grid=(t // tm, n // tn),
        in_specs=[pl.BlockSpec((tm, d), lambda i, j: (i, 0)), pl.BlockSpec((1, d), lambda i, j: (0, 0)),
                  pl.BlockSpec((d, tn), lambda i, j: (0, j)), pl.BlockSpec((1, tn), lambda i, j: (0, j)),
                  pl.BlockSpec((d, ns), lambda i, j: (0, 0))],
        out_specs=[pl.BlockSpec((tm, tn), lambda i, j: (i, j)), pl.BlockSpec((tm, ns), lambda i, j: (i, 0))],
        out_shape=[jax.ShapeDtypeStruct((t, n), BF16), jax.ShapeDtypeStruct((t, ns), F32)],
        scratch_shapes=[pltpu.VMEM((tm, d), BF16)],
        compiler_params=_cparams(("parallel", "arbitrary")),
        name="in_proj",
    )(x2d, norm_w.reshape(1, d), w_main, col_scale, w_small)


def _deltanet_kernel(q_ref, k_ref, v_ref, z_ref, sm_ref, cq_ref, ck_ref, cv_ref, alog_ref, dtb_ref, onw_ref,
                     o_ref, state_ref, halo_ref, vnew_ref, gct_ref, *, n_heads, heads_per_step):
    hg = pl.program_id(1)
    g = pl.program_id(2)
    G, C, D, NH = DN_GROUP, DN_CHUNK, LANES, heads_per_step
    lc = C.bit_length() - 1

    @pl.when(g == 0)
    def _():
        state_ref[...] = jnp.zeros_like(state_ref)
        halo_ref[...] = jnp.zeros_like(halo_ref)

    def conv_silu(x_ref, slot, w_ref):
        H = DN_CONV_HALO
        halo_ref[slot, H:H + G, :] = x_ref[0].astype(F32)
        w = w_ref[...]
        kw = w.shape[0]
        base = H - (kw - 1)
        y = halo_ref[slot, base:base + G, :] * w[0:1]
        for kk in range(1, kw):
            y = y + halo_ref[slot, base + kk:base + kk + G, :] * w[kk:kk + 1]
        halo_ref[slot, 0:H, :] = halo_ref[slot, G:G + H, :]
        return y * jax.nn.sigmoid(y)

    def l2n(x):
        return x * lax.rsqrt(jnp.sum(x * x, axis=-1, keepdims=True) + NORM_EPS)

    qf = conv_silu(q_ref, 0, cq_ref)
    kf = conv_silu(k_ref, 1, ck_ref)
    vf = conv_silu(v_ref, 2, cv_ref)

    cs = sm_ref[0]
    lane = lax.broadcasted_iota(jnp.int32, (G, LANES), 1)
    row = lax.broadcasted_iota(jnp.int32, (G, LANES), 0)
    beta_all = jax.nn.sigmoid(cs)
    xg = cs + dtb_ref[...]
    softplus = jnp.maximum(xg, 0.0) + jnp.log1p(jnp.exp(-jnp.abs(xg)))
    g_all = -jnp.exp(alog_ref[...]) * softplus
    pos = row & (C - 1)
    gc_all = g_all
    shift = 1
    while shift < C:
        gc_all = gc_all + jnp.where(pos >= shift, pltpu.roll(gc_all, shift, 0), 0.0)
        shift *= 2
    gl_all = jnp.concatenate([jnp.broadcast_to(gc_all[c * C + C - 1:c * C + C], (C, LANES))
                              for c in range(G // C)], axis=0)
    gct_ref[...] = gc_all.T

    ri = lax.broadcasted_iota(jnp.int32, (G, G), 0)
    ci = lax.broadcasted_iota(jnp.int32, (G, G), 1)
    same = (ri >> lc) == (ci >> lc)
    tri_incl = same & (ri >= ci)
    tri_strict = same & (ri > ci)
    eye = jnp.where(ri == ci, 1.0, 0.0)
    ct = lax.broadcasted_iota(jnp.int32, (D, G), 1) >> lc

    def pair_mask(s):
        ls = s.bit_length() - 1
        return ((ri >> (ls + 1)) == (ci >> (ls + 1))) & (((ri >> ls) & 1) == 1) & (((ci >> ls) & 1) == 0)

    pair_masks = {}
    s = 1
    while s < C:
        pair_masks[s] = pair_mask(s)
        s *= 2

    heads = range(NH)
    sls = [slice(hl * D, (hl + 1) * D) for hl in heads]
    hidx = [hg * NH + hl for hl in heads]

    def col_of(x, lane_idx):
        return jnp.sum(jnp.where(lane == lane_idx, x, 0.0), axis=1, keepdims=True)

    q = [l2n(qf[:, sl]) * (D ** -0.5) for sl in sls]
    k = [l2n(kf[:, sl]) for sl in sls]
    beta_col = [col_of(beta_all, h) for h in hidx]
    gc_col = [col_of(gc_all, h + n_heads) for h in hidx]
    gl_col = [col_of(gl_all, h + n_heads) for h in hidx]
    gc_row = [gct_ref[pl.ds(h + n_heads, 1), :] for h in hidx]
    decay = [jnp.where(tri_incl, jnp.exp(jnp.where(tri_incl, gc_col[a] - gc_row[a], 0.0)), 0.0) for a in heads]
    kb = [k[a] * beta_col[a] for a in heads]
    vb = [vf[:, sls[a]] * beta_col[a] for a in heads]
    k16 = [x.astype(BF16) for x in k]
    kk = [lax.dot_general(kb[a].astype(BF16), k16[a], NT_DIMS, preferred_element_type=F32) for a in heads]
    qk = [lax.dot_general(q[a].astype(BF16), k16[a], NT_DIMS, preferred_element_type=F32) for a in heads]
    lower = [jnp.where(tri_strict, kk[a] * decay[a], 0.0) for a in heads]
    attn = [jnp.where(tri_incl, qk[a] * decay[a], 0.0).astype(BF16) for a in heads]

    tinv = [eye - jnp.where(pair_masks[1], lower[a], 0.0) for a in heads]
    s = 2
    while s < C:
        t16 = [x.astype(BF16) for x in tinv]
        md = [jnp.dot(jnp.where(pair_masks[s], lower[a], 0.0).astype(BF16), t16[a], preferred_element_type=F32)
              for a in heads]
        tinv = [tinv[a] - jnp.dot(t16[a], md[a].astype(BF16), preferred_element_type=F32) for a in heads]
        s *= 2

    egc = [jnp.exp(x) for x in gc_col]
    uw = [jnp.dot(tinv[a].astype(BF16), jnp.concatenate([vb[a], kb[a] * egc[a]], axis=1).astype(BF16),
                  preferred_element_type=F32) for a in heads]
    q_dec = [q[a] * egc[a] for a in heads]
    k_dec_t = [(k[a] * jnp.exp(gl_col[a] - gc_col[a])).T for a in heads]
    egl = [jnp.exp(x) for x in gl_col]

    vnew_ref[...] = jnp.zeros_like(vnew_ref)
    outs = [[] for _ in heads]
    for c in range(G // C):
        r0 = c * C
        st = [state_ref[a] for a in heads]
        ws_qs = [jnp.dot(jnp.concatenate([uw[a][r0:r0 + C, D:], q_dec[a][r0:r0 + C]], axis=0).astype(BF16),
                         st[a].astype(BF16), preferred_element_type=F32) for a in heads]
        for a in heads:
            vnew_ref[a, r0:r0 + C, :] = (uw[a][r0:r0 + C, :D] - ws_qs[a][:C]).astype(BF16)
        vn = [vnew_ref[a] for a in heads]
        for a in heads:
            outs[a].append(ws_qs[a][C:] + jnp.dot(attn[a][r0:r0 + C], vn[a], preferred_element_type=F32))
        for a in heads:
            kt_c = jnp.where(ct == c, k_dec_t[a], 0.0).astype(BF16)
            e_c = jnp.concatenate([egl[a][r0:r0 + C]] * (D // C), axis=0)
            state_ref[a] = st[a] * e_c + jnp.dot(kt_c, vn[a], preferred_element_type=F32)

    for a in heads:
        o = jnp.concatenate(outs[a], axis=0)
        o = o * lax.rsqrt(jnp.mean(o * o, axis=-1, keepdims=True) + NORM_EPS) * onw_ref[...]
        z = z_ref[0, :, sls[a]].astype(F32)
        o_ref[0, :, sls[a]] = (o * (z * jax.nn.sigmoid(z))).astype(o_ref.dtype)


def _deltanet(p3, small3, conv_w, a_log_row, dt_row, onorm_w, n_heads, col_q, col_k, col_v, col_z,
              heads_per_step=8):
    b, s, _ = p3.shape
    G, D, NH = DN_GROUP, LANES, heads_per_step
    assert n_heads % NH == 0 and all(c % NH == 0 for c in (col_q, col_k, col_v, col_z))
    kw = conv_w.shape[0]
    blk = lambda off: pl.BlockSpec((1, G, NH * D), lambda bi, hi, gi: (bi, gi, off // NH + hi))
    cw = lambda off: pl.BlockSpec((kw, NH * D), lambda bi, hi, gi: (0, off // NH + hi))
    row_spec = pl.BlockSpec((1, LANES), lambda bi, hi, gi: (0, 0))
    return pl.pallas_call(
        functools.partial(_deltanet_kernel, n_heads=n_heads, heads_per_step=NH),
        grid=(b, n_heads // NH, s // G),
        in_specs=[blk(col_q), blk(col_k), blk(col_v), blk(col_z),
                  pl.BlockSpec((1, G, LANES), lambda bi, hi, gi: (bi, gi, 0)),
                  cw(0), cw(n_heads), cw(2 * n_heads), row_spec, row_spec, row_spec],
        out_specs=pl.BlockSpec((1, G, NH * D), lambda bi, hi, gi: (bi, gi, hi)),
        out_shape=jax.ShapeDtypeStruct((b, s, n_heads * D), BF16),
        scratch_shapes=[pltpu.VMEM((NH, D, D), F32), pltpu.VMEM((3, DN_CONV_HALO + G, NH * D), F32),
                        pltpu.VMEM((NH, G, D), BF16), pltpu.VMEM((LANES, G), F32)],
        compiler_params=_cparams(("parallel", "parallel", "arbitrary")),
        name="gated_deltanet",
    )(p3, p3, p3, p3, small3, conv_w, conv_w, conv_w, a_log_row, dt_row, onorm_w)


def _t5_bucket_f32(dist):
    dist = jnp.maximum(dist, 0)
    max_exact = REL_BUCKETS // 2
    d = jnp.maximum(dist, 1).astype(F32)
    large = max_exact + (jnp.log(d / max_exact) / math.log(REL_MAX_DIST / max_exact)
                         * (REL_BUCKETS - max_exact)).astype(jnp.int32)
    large = jnp.minimum(large, REL_BUCKETS - 1)
    return jnp.where(dist < max_exact, dist, large)


def _n_near_tiles():
    return -(-(REL_MAX_DIST + MOBA_BLOCK - 1) // MOBA_BLOCK)


def _moba_kernel(rb_ref, q_ref, k_ref, v_ref, z_ref, o_ref, kmean_ref, bias_ref, etab_ref, xt_ref, qa_ref,
                 s0_ref, s1_ref, wide_ref, *, n_near, nbp, group, qblocks):
    hh = pl.program_id(0)
    bb = pl.program_id(1)
    st = pl.program_id(2)
    BLK, D, U, QB = MOBA_BLOCK, LANES, group, qblocks
    R = QB * BLK
    nb = k_ref.shape[1] // BLK
    col_shift = nbp

    @pl.when((bb == 0) & (st == 0))
    def _():
        qr = lax.broadcasted_iota(jnp.int32, (BLK, BLK), 0)
        kc = lax.broadcasted_iota(jnp.int32, (BLK, BLK), 1)
        for t in range(n_near):
            dist = t * BLK + qr - kc
            bucket = _t5_bucket_f32(dist)
            val = jnp.full((BLK, BLK), rb_ref[0, hh], F32)
            for bk in range(1, REL_BUCKETS):
                val = jnp.where(bucket == bk, rb_ref[bk, hh], val)
            if t == 0:
                val = jnp.where(dist >= 0, val, NEG_INF)
            bias_ref[t] = val
        bias_ref[n_near] = jnp.full((BLK, BLK), rb_ref[REL_BUCKETS - 1, hh], F32)
        lane = lax.broadcasted_iota(jnp.int32, (16, D), 1)
        for j in range(nb):
            etab_ref[j] = jnp.where((lane == j) | (lane == col_shift), 1.0, 0.0).astype(BF16)

    @pl.when(st == 0)
    def _():
        def norms(j, carry):
            k2max, q2max = carry
            rows = pl.ds(pl.multiple_of(j * BLK, BLK), BLK)
            kj = k_ref[0, rows, :].astype(F32)
            qj = q_ref[0, rows, :].astype(F32)
            kmean_ref[pl.ds(j, 1), :] = jnp.mean(kj, axis=0, keepdims=True)
            return (jnp.maximum(k2max, jnp.max(jnp.sum(kj * kj, axis=1, keepdims=True))),
                    jnp.maximum(q2max, jnp.max(jnp.sum(qj * qj, axis=1, keepdims=True))))
        k2max, q2max = lax.fori_loop(0, nb, norms, (jnp.float32(0.0), jnp.float32(0.0)))
        rb_abs = jnp.abs(rb_ref[0, hh])
        for bk in range(1, REL_BUCKETS):
            rb_abs = jnp.maximum(rb_abs, jnp.abs(rb_ref[bk, hh]))
        bound = jnp.sqrt(q2max * k2max) * 1.001 + rb_abs + 1e-3
        wide_ref[0] = jnp.where(bound > MOBA_SHIFT_SLACK, 1, 0).astype(jnp.int32)

        kmean16 = kmean_ref[...].astype(BF16)
        jrow = lax.broadcasted_iota(jnp.int32, (nb, BLK), 0)
        jrow_f = jrow.astype(F32)

        def visibility(it, carry):
            for u in range(U):
                ib = it * U + u
                rows = pl.ds(pl.multiple_of(ib * BLK, BLK), BLK)
                gate = lax.dot_general(kmean16, q_ref[0, rows, :], NT_DIMS, preferred_element_type=F32)
                past = jrow < ib
                sc = jnp.where(past, gate, NEG_INF)
                visible = jrow == ib
                for _ in range(MOBA_TOPK):
                    top = jnp.max(sc, axis=0, keepdims=True)
                    first = jnp.min(jnp.where(sc == top, jrow_f, float(nb)), axis=0, keepdims=True)
                    pick = jrow_f == first
                    visible = visible | (pick & past)
                    sc = jnp.where(pick, jnp.finfo(F32).min, sc)
                cols = jnp.concatenate([jnp.where(visible, 0.0, NEG_INF), jnp.zeros((D - nb, BLK), F32)], axis=0)
                xt_ref[rows, :] = cols.T.astype(BF16)
            return carry
        lax.fori_loop(0, nb // U, visibility, 0)

    r0 = pl.multiple_of(st * R, R)
    qa_ref[:, 0:D] = q_ref[0, pl.ds(r0, R), :]
    qa_ref[:, D:2 * D] = xt_ref[pl.ds(r0, R), :]
    i_first = st * QB
    n_groups = (i_first + QB - 1) // U + 1

    def group_scores(g):
        j0 = g * U
        row0 = pl.multiple_of(j0 * BLK, U * BLK)
        e_rows = []
        for u in range(U):
            e_rows += [etab_ref[j0 + u]] * (BLK // 16)
        ka = jnp.concatenate([k_ref[0, pl.ds(row0, U * BLK), :], jnp.concatenate(e_rows, axis=0)], axis=1)
        s = lax.dot_general(qa_ref[...], ka, NT_DIMS, preferred_element_type=F32)
        bias = jnp.concatenate(
            [jnp.concatenate([bias_ref[jnp.clip(i_first + a - (j0 + u), 0, n_near)] for u in range(U)], axis=1)
             for a in range(QB)], axis=0)
        return s + bias

    def lane_chunks(s):
        return [s[:, c * D:(c + 1) * D] for c in range(s.shape[1] // D)]

    @pl.when(wide_ref[0] == 1)
    def _():
        def max_step(g, mm):
            for c in lane_chunks(group_scores(g)):
                mm = jnp.maximum(mm, c)
            return mm
        mm = lax.fori_loop(0, n_groups, max_step, jnp.full((R, D), jnp.finfo(F32).min, F32))
        m = jnp.max(mm, axis=1, keepdims=True)
        lane_q = lax.broadcasted_iota(jnp.int32, (R, D), 1)
        cols = xt_ref[pl.ds(r0, R), :].astype(F32)
        qa_ref[:, D:2 * D] = jnp.where(lane_q == col_shift, -m, cols).astype(BF16)

    def consume(g, s, carry):
        acc, lsum = carry
        p = jnp.exp(s)
        for c in lane_chunks(p):
            lsum = lsum + c
        vg = v_ref[0, pl.ds(pl.multiple_of(g * (U * BLK), U * BLK), U * BLK), :]
        return acc + jnp.dot(p.astype(BF16), vg, preferred_element_type=F32), lsum

    def stage(g, cur_ref, nxt_ref, carry):
        nxt_ref[...] = group_scores(g + 1)
        return consume(g, cur_ref[...], carry)

    def two_stages(t, carry):
        carry = stage(2 * t, s0_ref, s1_ref, carry)
        return stage(2 * t + 1, s1_ref, s0_ref, carry)

    s0_ref[...] = group_scores(0)
    zero = jnp.zeros((R, D), F32)
    last = n_groups - 1
    carry = lax.fori_loop(0, last // 2, two_stages, (zero, zero))

    def odd_tail(c):
        c = stage(last - 1, s0_ref, s1_ref, c)
        return consume(last, s1_ref[...], c)

    acc, lsum = lax.cond(last % 2 == 1, odd_tail, lambda c: consume(last, s0_ref[...], c), carry)

    out = acc / jnp.sum(lsum, axis=1, keepdims=True)
    z = z_ref[0].astype(F32)
    o_ref[0] = (out * (z * jax.nn.sigmoid(z))).astype(o_ref.dtype)


def _moba(rel_bias, p3, n_heads, col_q, col_k, col_v, col_z, group=4, qblocks=4):
    b, s, _ = p3.shape
    BLK, D = MOBA_BLOCK, LANES
    nb = s // BLK
    nbp = -(-nb // 16) * 16
    assert nbp + 16 <= D, "block-visibility columns must fit in the spare contraction columns"
    assert nb % group == 0 and nb % qblocks == 0, "key blocks are visited in whole groups"
    n_near = _n_near_tiles()
    rows = qblocks * BLK
    seq = lambda off: pl.BlockSpec((1, s, D), lambda hi, bi, i: (bi, 0, off + hi))
    blk = lambda off: pl.BlockSpec((1, rows, D), lambda hi, bi, i: (bi, i, off + hi))
    return pl.pallas_call(
        functools.partial(_moba_kernel, n_near=n_near, nbp=nbp, group=group, qblocks=qblocks),
        grid=(n_heads, b, nb // qblocks),
        in_specs=[pl.BlockSpec(memory_space=pltpu.SMEM), seq(col_q), seq(col_k), seq(col_v), blk(col_z)],
        out_specs=blk(0),
        out_shape=jax.ShapeDtypeStruct((b, s, n_heads * D), BF16),
        scratch_shapes=[pltpu.VMEM((nb, D), F32), pltpu.VMEM((n_near + 1, BLK, BLK), F32),
                        pltpu.VMEM((nb, 16, D), BF16), pltpu.VMEM((s, D), BF16), pltpu.VMEM((rows, 2 * D), BF16),
                        pltpu.VMEM((rows, group * BLK), F32), pltpu.VMEM((rows, group * BLK), F32),
                        pltpu.SMEM((1,), jnp.int32)],
        compiler_params=_cparams(("arbitrary", "arbitrary", "arbitrary")),
        name="moba_attention",
    )(rel_bias, p3, p3, p3, p3)


def _merge_kernel(ya_ref, yb_ref, g0_ref, g1_ref, x_ref, w0_ref, w1_ref, wo_ref, b0_ref, b1_ref, nw_ref, o_ref):
    pa = jnp.dot(ya_ref[...], w0_ref[...], preferred_element_type=F32)
    pb = jnp.dot(yb_ref[...], w1_ref[...], preferred_element_type=F32)
    g0 = jax.nn.sigmoid(g0_ref[...].astype(F32) + b0_ref[...])
    g1 = jax.nn.sigmoid(g1_ref[...].astype(F32) + b1_ref[...])
    merged = (g0 * pa + g1 * pb).astype(BF16)
    out = jnp.dot(merged, wo_ref[...], preferred_element_type=F32)
    y = out * lax.rsqrt(jnp.mean(out * out, axis=-1, keepdims=True) + NORM_EPS) * nw_ref[...]
    o_ref[...] = x_ref[...] + y


def _merge(ya, yb, p2, x2d, w0, w1, wo, b_gate, norm_w, col_g0, col_g1, tm=512):
    t, d = x2d.shape
    row = lambda c: pl.BlockSpec((tm, d), lambda i: (i, c))
    full = pl.BlockSpec((d, d), lambda i: (0, 0))
    vec = lambda c: pl.BlockSpec((1, d), lambda i: (0, c))
    return pl.pallas_call(
        _merge_kernel,
        grid=(t // tm,),
        in_specs=[row(0), row(0), row(col_g0), row(col_g1), row(0), full, full, full, vec(0), vec(1), vec(0)],
        out_specs=row(0),
        out_shape=jax.ShapeDtypeStruct((t, d), F32),
        compiler_params=_cparams(("parallel",)),
        name="merge_out",
    )(ya, yb, p2, p2, x2d, w0, w1, wo, b_gate.reshape(1, -1), b_gate.reshape(1, -1), norm_w.reshape(1, d))


def kernel(x, norm_pre_w, w_in, b_gate, conv_w, dn_a_log, dn_dt_bias, dn_onorm_w, rel_bias, w_branch, w_out,
           norm_post_w):
    b, s, d = x.shape
    n_heads = dn_a_log.shape[0]
    hd = dn_onorm_w.shape[0]
    width = n_heads * hd
    assert hd == LANES and rel_bias.shape[1] == n_heads and width == d
    assert s % MOBA_BLOCK == 0 and s % DN_GROUP == 0 and 2 * n_heads <= LANES
    t = b * s

    c_beta = 4 * width
    c_qb = c_beta + 2 * n_heads
    w16 = w_in.astype(BF16)
    w_main = jnp.concatenate([w16[:, :c_beta], w16[:, c_qb:]], axis=1)
    w_small = jnp.pad(w16[:, c_beta:c_qb], ((0, 0), (0, LANES - 2 * n_heads)))
    nblk = width // LANES
    col = lambda k: k * nblk

    x2d = x.reshape(t, d)
    col_scale = jnp.ones((1, w_main.shape[1]), F32).at[:, 4 * width:5 * width].set(hd ** -0.5)
    p2, small = _in_proj(x2d, norm_pre_w.astype(F32), w_main, col_scale, w_small)
    p3 = p2.reshape(b, s, -1)

    pad = (0, LANES - 2 * n_heads)
    a_log_row = jnp.pad(jnp.concatenate([jnp.zeros_like(dn_a_log), dn_a_log]), pad).reshape(1, LANES).astype(F32)
    dt_row = jnp.pad(jnp.concatenate([jnp.zeros_like(dn_dt_bias), dn_dt_bias]), pad).reshape(1, LANES).astype(F32)
    ya = _deltanet(p3, small.reshape(b, s, LANES), conv_w.astype(F32), a_log_row, dt_row,
                   dn_onorm_w.reshape(1, hd).astype(F32), n_heads, col(0), col(1), col(2), col(3))
    yb = _moba(rel_bias.astype(F32), p3, n_heads, col(4), col(5), col(6), col(7))

    out = _merge(ya.reshape(t, d), yb.reshape(t, d), p2, x2d,
                 w_branch[0].astype(BF16), w_branch[1].astype(BF16), w_out.astype(BF16),
                 b_gate.astype(F32), norm_post_w.astype(F32), 8, 9)
    return out.reshape(b, s, d)
```

```python
import functools
import math

import jax
import jax.numpy as jnp
from jax import lax
from jax.experimental import pallas as pl
from jax.experimental.pallas import tpu as pltpu

F32 = jnp.float32
BF16 = jnp.bfloat16

LANES = 128
DN_CHUNK = 64
DN_GROUP = 128
DN_CONV_HALO = 8
MOBA_BLOCK = 256
MOBA_TOPK = 3
MOBA_SHIFT_SLACK = 40.0
REL_BUCKETS = 32
REL_MAX_DIST = 2048
NORM_EPS = 1e-6
NEG_INF = -1e30
VMEM_LIMIT = 56 * 1024 * 1024
NT_DIMS = (((1,), (1,)), ((), ()))


def _cparams(sem):
    return pltpu.CompilerParams(dimension_semantics=sem, vmem_limit_bytes=VMEM_LIMIT)


def _normalise(x, w):
    y = x * lax.rsqrt(jnp.mean(x * x, axis=-1, keepdims=True) + NORM_EPS)
    return (y * w).astype(BF16)


def _in_proj_kernel(x_ref, xh_ref, nw_ref, wc_ref, wr_ref, cs_ref, ws_ref, cw_ref, oc_ref, or_ref, sm_ref,
                    h_ref, hh_ref, *, tiles_per_seq):
    i = pl.program_id(0)
    tm = oc_ref.shape[0]
    halo = hh_ref.shape[0]

    @pl.when(pl.program_id(1) == 0)
    def _():
        h = _normalise(x_ref[...], nw_ref[...])
        h_ref[...] = h
        sm_ref[...] = jnp.dot(h, ws_ref[...], preferred_element_type=F32)
        hh = _normalise(xh_ref[...], nw_ref[...])
        hh_ref[...] = jnp.where(i % tiles_per_seq == 0, jnp.zeros_like(hh), hh)

    h = h_ref[...]
    wc = wc_ref[...]
    ext = jnp.concatenate([jnp.dot(hh_ref[...], wc, preferred_element_type=F32),
                           jnp.dot(h, wc, preferred_element_type=F32)], axis=0)
    acc = jnp.dot(h, wr_ref[...], preferred_element_type=F32)
    or_ref[...] = (acc * cs_ref[...]).astype(or_ref.dtype)

    cw = cw_ref[...]
    kw = cw.shape[0]
    base = halo - (kw - 1)
    y = ext[base:base + tm] * cw[0:1]
    for kk in range(1, kw):
        y = y + ext[base + kk:base + kk + tm] * cw[kk:kk + 1]
    oc_ref[...] = (y * jax.nn.sigmoid(y)).astype(oc_ref.dtype)


def _in_proj(x2d, norm_w, w_conv, w_rest, col_scale, w_small, conv_w, seq_len, tm=1024, steps=4, halo=16):
    t, d = x2d.shape
    nc, nr, ns = w_conv.shape[1], w_rest.shape[1], w_small.shape[1]
    kw = conv_w.shape[0]
    assert nc % (steps * LANES) == 0 and nr % (steps * LANES) == 0
    assert seq_len % tm == 0 and tm % halo == 0 and kw - 1 <= halo and conv_w.shape[1] == nc
    tc, tr = nc // steps, nr // steps
    row = lambda w: pl.BlockSpec((tm, w), lambda i, j: (i, j))
    col = lambda r, w: pl.BlockSpec((r, w), lambda i, j: (0, j))
    fixed = lambda r, w: pl.BlockSpec((r, w), lambda i, j: (0, 0))
    return pl.pallas_call(
        functools.partial(_in_proj_kernel, tiles_per_seq=seq_len // tm),
        grid=(t // tm, steps),
        in_specs=[pl.BlockSpec((tm, d), lambda i, j: (i, 0)),
                  pl.BlockSpec((halo, d), lambda i, j: (jnp.maximum(i * (tm // halo) - 1, 0), 0)),
                  fixed(1, d), col(d, tc), col(d, tr), col(1, tr), fixed(d, ns), col(kw, tc)],
        out_specs=[row(tc), row(tr), pl.BlockSpec((tm, ns), lambda i, j: (i, 0))],
        out_shape=[jax.ShapeDtypeStruct((t, nc), BF16), jax.ShapeDtypeStruct((t, nr), BF16),
                   jax.ShapeDtypeStruct((t, ns), F32)],
        scratch_shapes=[pltpu.VMEM((tm, d), BF16), pltpu.VMEM((halo, d), BF16)],
        compiler_params=_cparams(("parallel", "arbitrary")),
        name="in_proj",
    )(x2d, x2d, norm_w.reshape(1, d), w_conv, w_rest, col_scale, w_small, conv_w)


def _deltanet_kernel(q_ref, k_ref, v_ref, z_ref, sm_ref, alog_ref, dtb_ref, onw_ref,
                     o_ref, state_ref, vnew_ref, gct_ref, *, n_heads, heads_per_step):
    hg = pl.program_id(1)
    g = pl.program_id(2)
    G, C, D, NH = DN_GROUP, DN_CHUNK, LANES, heads_per_step
    lc = C.bit_length() - 1

    @pl.when(g == 0)
    def _():
        state_ref[...] = jnp.zeros_like(state_ref)

    def l2n(x):
        return x * lax.rsqrt(jnp.sum(x * x, axis=-1, keepdims=True) + NORM_EPS)

    qf = q_ref[0].astype(F32)
    kf = k_ref[0].astype(F32)
    vf = v_ref[0].astype(F32)

    cs = sm_ref[0]
    lane = lax.broadcasted_iota(jnp.int32, (G, LANES), 1)
    row = lax.broadcasted_iota(jnp.int32, (G, LANES), 0)
    beta_all = jax.nn.sigmoid(cs)
    xg = cs + dtb_ref[...]
    softplus = jnp.maximum(xg, 0.0) + jnp.log1p(jnp.exp(-jnp.abs(xg)))
    g_all = -jnp.exp(alog_ref[...]) * softplus
    pos = row & (C - 1)
    gc_all = g_all
    shift = 1
    while shift < C:
        gc_all = gc_all + jnp.where(pos >= shift, pltpu.roll(gc_all, shift, 0), 0.0)
        shift *= 2
    gl_all = jnp.concatenate([jnp.broadcast_to(gc_all[c * C + C - 1:c * C + C], (C, LANES))
                              for c in range(G // C)], axis=0)
    gct_ref[...] = gc_all.T

    ri = lax.broadcasted_iota(jnp.int32, (G, G), 0)
    ci = lax.broadcasted_iota(jnp.int32, (G, G), 1)
    same = (ri >> lc) == (ci >> lc)
    tri_incl = same & (ri >= ci)
    tri_strict = same & (ri > ci)
    eye = jnp.where(ri == ci, 1.0, 0.0)
    ct = lax.broadcasted_iota(jnp.int32, (D, G), 1) >> lc

    def pair_mask(s):
        ls = s.bit_length() - 1
        return ((ri >> (ls + 1)) == (ci >> (ls + 1))) & (((ri >> ls) & 1) == 1) & (((ci >> ls) & 1) == 0)

    pair_masks = {}
    s = 1
    while s < C:
        pair_masks[s] = pair_mask(s)
        s *= 2

    heads = range(NH)
    sls = [slice(hl * D, (hl + 1) * D) for hl in heads]
    hidx = [hg * NH + hl for hl in heads]

    def col_of(x, lane_idx):
        return jnp.sum(jnp.where(lane == lane_idx, x, 0.0), axis=1, keepdims=True)

    q = [l2n(qf[:, sl]) * (D ** -0.5) for sl in sls]
    k = [l2n(kf[:, sl]) for sl in sls]
    beta_col = [col_of(beta_all, h) for h in hidx]
    gc_col = [col_of(gc_all, h + n_heads) for h in hidx]
    gl_col = [col_of(gl_all, h + n_heads) for h in hidx]
    gc_row = [gct_ref[pl.ds(h + n_heads, 1), :] for h in hidx]
    decay = [jnp.where(tri_incl, jnp.exp(jnp.where(tri_incl, gc_col[a] - gc_row[a], 0.0)), 0.0) for a in heads]
    kb = [k[a] * beta_col[a] for a in heads]
    vb = [vf[:, sls[a]] * beta_col[a] for a in heads]
    k16 = [x.astype(BF16) for x in k]
    kk = [lax.dot_general(kb[a].astype(BF16), k16[a], NT_DIMS, preferred_element_type=F32) for a in heads]
    qk = [lax.dot_general(q[a].astype(BF16), k16[a], NT_DIMS, preferred_element_type=F32) for a in heads]
    lower = [jnp.where(tri_strict, kk[a] * decay[a], 0.0) for a in heads]
    attn = [jnp.where(tri_incl, qk[a] * decay[a], 0.0).astype(BF16) for a in heads]

    tinv = [eye - jnp.where(pair_masks[1], lower[a], 0.0) for a in heads]
    s = 2
    while s < C:
        t16 = [x.astype(BF16) for x in tinv]
        md = [jnp.dot(jnp.where(pair_masks[s], lower[a], 0.0).astype(BF16), t16[a], preferred_element_type=F32)
              for a in heads]
        tinv = [tinv[a] - jnp.dot(t16[a], md[a].astype(BF16), preferred_element_type=F32) for a in heads]
        s *= 2

    egc = [jnp.exp(x) for x in gc_col]
    uw = [jnp.dot(tinv[a].astype(BF16), jnp.concatenate([vb[a], kb[a] * egc[a]], axis=1).astype(BF16),
                  preferred_element_type=F32) for a in heads]
    q_dec = [q[a] * egc[a] for a in heads]
    k_dec_t = [(k[a] * jnp.exp(gl_col[a] - gc_col[a])).T for a in heads]
    egl = [jnp.exp(x) for x in gl_col]

    vnew_ref[...] = jnp.zeros_like(vnew_ref)
    outs = [[] for _ in heads]
    for c in range(G // C):
        r0 = c * C
        st = [state_ref[a] for a in heads]
        ws_qs = [jnp.dot(jnp.concatenate([uw[a][r0:r0 + C, D:], q_dec[a][r0:r0 + C]], axis=0).astype(BF16),
                         st[a].astype(BF16), preferred_element_type=F32) for a in heads]
        for a in heads:
            vnew_ref[a, r0:r0 + C, :] = (uw[a][r0:r0 + C, :D] - ws_qs[a][:C]).astype(BF16)
        vn = [vnew_ref[a] for a in heads]
        for a in heads:
            outs[a].append(ws_qs[a][C:] + jnp.dot(attn[a][r0:r0 + C], vn[a], preferred_element_type=F32))
        for a in heads:
            kt_c = jnp.where(ct == c, k_dec_t[a], 0.0).astype(BF16)
            e_c = jnp.concatenate([egl[a][r0:r0 + C]] * (D // C), axis=0)
            state_ref[a] = st[a] * e_c + jnp.dot(kt_c, vn[a], preferred_element_type=F32)

    for a in heads:
        o = jnp.concatenate(outs[a], axis=0)
        o = o * lax.rsqrt(jnp.mean(o * o, axis=-1, keepdims=True) + NORM_EPS) * onw_ref[...]
        z = z_ref[0, :, sls[a]].astype(F32)
        o_ref[0, :, sls[a]] = (o * (z * jax.nn.sigmoid(z))).astype(o_ref.dtype)


def _deltanet(qkv3, z3, small3, a_log_row, dt_row, onorm_w, n_heads, col_q, col_k, col_v, col_z, heads_per_step=8):
    b, s, _ = qkv3.shape
    G, D, NH = DN_GROUP, LANES, heads_per_step
    assert n_heads % NH == 0 and all(c % NH == 0 for c in (col_q, col_k, col_v, col_z))
    blk = lambda off: pl.BlockSpec((1, G, NH * D), lambda bi, hi, gi: (bi, gi, off // NH + hi))
    row_spec = pl.BlockSpec((1, LANES), lambda bi, hi, gi: (0, 0))
    return pl.pallas_call(
        functools.partial(_deltanet_kernel, n_heads=n_heads, heads_per_step=NH),
        grid=(b, n_heads // NH, s // G),
        in_specs=[blk(col_q), blk(col_k), blk(col_v), blk(col_z),
                  pl.BlockSpec((1, G, LANES), lambda bi, hi, gi: (bi, gi, 0)),
                  row_spec, row_spec, row_spec],
        out_specs=pl.BlockSpec((1, G, NH * D), lambda bi, hi, gi: (bi, gi, hi)),
        out_shape=jax.ShapeDtypeStruct((b, s, n_heads * D), BF16),
        scratch_shapes=[pltpu.VMEM((NH, D, D), F32),
                        pltpu.VMEM((NH, G, D), BF16), pltpu.VMEM((LANES, G), F32)],
        compiler_params=_cparams(("parallel", "parallel", "arbitrary")),
        name="gated_deltanet",
    )(qkv3, qkv3, qkv3, z3, small3, a_log_row, dt_row, onorm_w)


def _t5_bucket_f32(dist):
    dist = jnp.maximum(dist, 0)
    max_exact = REL_BUCKETS // 2
    d = jnp.maximum(dist, 1).astype(F32)
    large = max_exact + (jnp.log(d / max_exact) / math.log(REL_MAX_DIST / max_exact)
                         * (REL_BUCKETS - max_exact)).astype(jnp.int32)
    large = jnp.minimum(large, REL_BUCKETS - 1)
    return jnp.where(dist < max_exact, dist, large)


def _n_near_tiles():
    return -(-(REL_MAX_DIST + MOBA_BLOCK - 1) // MOBA_BLOCK)


def _moba_kernel(rb_ref, q_ref, k_ref, v_ref, z_ref, o_ref, kmean_ref, bias_ref, etab_ref, xt_ref, qa_ref,
                 s0_ref, s1_ref, wide_ref, *, n_near, nbp, group, qblocks):
    hh = pl.program_id(0)
    bb = pl.program_id(1)
    st = pl.program_id(2)
    BLK, D, U, QB = MOBA_BLOCK, LANES, group, qblocks
    R = QB * BLK
    nb = k_ref.shape[1] // BLK
    col_shift = nbp

    @pl.when((bb == 0) & (st == 0))
    def _():
        qr = lax.broadcasted_iota(jnp.int32, (BLK, BLK), 0)
        kc = lax.broadcasted_iota(jnp.int32, (BLK, BLK), 1)
        for t in range(n_near):
            dist = t * BLK + qr - kc
            bucket = _t5_bucket_f32(dist)
            val = jnp.full((BLK, BLK), rb_ref[0, hh], F32)
            for bk in range(1, REL_BUCKETS):
                val = jnp.where(bucket == bk, rb_ref[bk, hh], val)
            if t == 0:
                val = jnp.where(dist >= 0, val, NEG_INF)
            bias_ref[t] = val
        bias_ref[n_near] = jnp.full((BLK, BLK), rb_ref[REL_BUCKETS - 1, hh], F32)
        lane = lax.broadcasted_iota(jnp.int32, (16, D), 1)
        for j in range(nb):
            etab_ref[j] = jnp.where((lane == j) | (lane == col_shift), 1.0, 0.0).astype(BF16)

    @pl.when(st == 0)
    def _():
        def norms(j, carry):
            k2max, q2max = carry
            rows = pl.ds(pl.multiple_of(j * BLK, BLK), BLK)
            kj = k_ref[0, rows, :].astype(F32)
            qj = q_ref[0, rows, :].astype(F32)
            kmean_ref[pl.ds(j, 1), :] = jnp.mean(kj, axis=0, keepdims=True)
            return (jnp.maximum(k2max, jnp.max(jnp.sum(kj * kj, axis=1, keepdims=True))),
                    jnp.maximum(q2max, jnp.max(jnp.sum(qj * qj, axis=1, keepdims=True))))
        k2max, q2max = lax.fori_loop(0, nb, norms, (jnp.float32(0.0), jnp.float32(0.0)))
        rb_abs = jnp.abs(rb_ref[0, hh])
        for bk in range(1, REL_BUCKETS):
            rb_abs = jnp.maximum(rb_abs, jnp.abs(rb_ref[bk, hh]))
        bound = jnp.sqrt(q2max * k2max) * 1.001 + rb_abs + 1e-3
        wide_ref[0] = jnp.where(bound > MOBA_SHIFT_SLACK, 1, 0).astype(jnp.int32)

        kmean16 = kmean_ref[...].astype(BF16)
        jrow = lax.broadcasted_iota(jnp.int32, (nb, BLK), 0)
        jrow_f = jrow.astype(F32)

        def visibility(it, carry):
            for u in range(U):
                ib = it * U + u
                rows = pl.ds(pl.multiple_of(ib * BLK, BLK), BLK)
                gate = lax.dot_general(kmean16, q_ref[0, rows, :], NT_DIMS, preferred_element_type=F32)
                past = jrow < ib
                sc = jnp.where(past, gate, NEG_INF)
                visible = jrow == ib
                for _ in range(MOBA_TOPK):
                    top = jnp.max(sc, axis=0, keepdims=True)
                    first = jnp.min(jnp.where(sc == top, jrow_f, float(nb)), axis=0, keepdims=True)
                    pick = jrow_f == first
                    visible = visible | (pick & past)
                    sc = jnp.where(pick, jnp.finfo(F32).min, sc)
                cols = jnp.concatenate([jnp.where(visible, 0.0, NEG_INF), jnp.zeros((D - nb, BLK), F32)], axis=0)
                xt_ref[rows, :] = cols.T.astype(BF16)
            return carry
        lax.fori_loop(0, nb // U, visibility, 0)

    r0 = pl.multiple_of(st * R, R)
    qa_ref[:, 0:D] = q_ref[0, pl.ds(r0, R), :]
    qa_ref[:, D:2 * D] = xt_ref[pl.ds(r0, R), :]
    i_first = st * QB
    n_groups = (i_first + QB - 1) // U + 1

    def group_scores(g):
        j0 = g * U
        row0 = pl.multiple_of(j0 * BLK, U * BLK)
        e_rows = []
        for u in range(U):
            e_rows += [etab_ref[j0 + u]] * (BLK // 16)
        ka = jnp.concatenate([k_ref[0, pl.ds(row0, U * BLK), :], jnp.concatenate(e_rows, axis=0)], axis=1)
        s = lax.dot_general(qa_ref[...], ka, NT_DIMS, preferred_element_type=F32)
        bias = jnp.concatenate(
            [jnp.concatenate([bias_ref[jnp.clip(i_first + a - (j0 + u), 0, n_near)] for u in range(U)], axis=1)
             for a in range(QB)], axis=0)
        return s + bias

    def lane_chunks(s):
        return [s[:, c * D:(c + 1) * D] for c in range(s.shape[1] // D)]

    @pl.when(wide_ref[0] == 1)
    def _():
        def max_step(g, mm):
            for c in lane_chunks(group_scores(g)):
                mm = jnp.maximum(mm, c)
            return mm
        mm = lax.fori_loop(0, n_groups, max_step, jnp.full((R, D), jnp.finfo(F32).min, F32))
        m = jnp.max(mm, axis=1, keepdims=True)
        lane_q = lax.broadcasted_iota(jnp.int32, (R, D), 1)
        cols = xt_ref[pl.ds(r0, R), :].astype(F32)
        qa_ref[:, D:2 * D] = jnp.where(lane_q == col_shift, -m, cols).astype(BF16)

    def consume(g, s, carry):
        acc, lsum = carry
        p = jnp.exp(s)
        for c in lane_chunks(p):
            lsum = lsum + c
        vg = v_ref[0, pl.ds(pl.multiple_of(g * (U * BLK), U * BLK), U * BLK), :]
        return acc + jnp.dot(p.astype(BF16), vg, preferred_element_type=F32), lsum

    def stage(g, cur_ref, nxt_ref, carry):
        nxt_ref[...] = group_scores(g + 1)
        return consume(g, cur_ref[...], carry)

    def two_stages(t, carry):
        carry = stage(2 * t, s0_ref, s1_ref, carry)
        return stage(2 * t + 1, s1_ref, s0_ref, carry)

    s0_ref[...] = group_scores(0)
    zero = jnp.zeros((R, D), F32)
    last = n_groups - 1
    carry = lax.fori_loop(0, last // 2, two_stages, (zero, zero))

    def odd_tail(c):
        c = stage(last - 1, s0_ref, s1_ref, c)
        return consume(last, s1_ref[...], c)

    acc, lsum = lax.cond(last % 2 == 1, odd_tail, lambda c: consume(last, s0_ref[...], c), carry)

    out = acc / jnp.sum(lsum, axis=1, keepdims=True)
    z = z_ref[0].astype(F32)
    o_ref[0] = (out * (z * jax.nn.sigmoid(z))).astype(o_ref.dtype)


def _moba(rel_bias, p3, n_heads, col_q, col_k, col_v, col_z, group=4, qblocks=4):
    b, s, _ = p3.shape
    BLK, D = MOBA_BLOCK, LANES
    nb = s // BLK
    nbp = -(-nb // 16) * 16
    assert nbp + 16 <= D, "block-visibility columns must fit in the spare contraction columns"
    assert nb % group == 0 and nb % qblocks == 0, "key blocks are visited in whole groups"
    n_near = _n_near_tiles()
    rows = qblocks * BLK
    seq = lambda off: pl.BlockSpec((1, s, D), lambda hi, bi, i: (bi, 0, off + hi))
    blk = lambda off: pl.BlockSpec((1, rows, D), lambda hi, bi, i: (bi, i, off + hi))
    return pl.pallas_call(
        functools.partial(_moba_kernel, n_near=n_near, nbp=nbp, group=group, qblocks=qblocks),
        grid=(n_heads, b, nb // qblocks),
        in_specs=[pl.BlockSpec(memory_space=pltpu.SMEM), seq(col_q), seq(col_k), seq(col_v), blk(col_z)],
        out_specs=blk(0),
        out_shape=jax.ShapeDtypeStruct((b, s, n_heads * D), BF16),
        scratch_shapes=[pltpu.VMEM((nb, D), F32), pltpu.VMEM((n_near + 1, BLK, BLK), F32),
                        pltpu.VMEM((nb, 16, D), BF16), pltpu.VMEM((s, D), BF16), pltpu.VMEM((rows, 2 * D), BF16),
                        pltpu.VMEM((rows, group * BLK), F32), pltpu.VMEM((rows, group * BLK), F32),
                        pltpu.SMEM((1,), jnp.int32)],
        compiler_params=_cparams(("arbitrary", "arbitrary", "arbitrary")),
        name="moba_attention",
    )(rel_bias, p3, p3, p3, p3)


def _merge_kernel(ya_ref, yb_ref, g0_ref, g1_ref, x_ref, w0_ref, w1_ref, wo_ref, b0_ref, b1_ref, nw_ref, o_ref):
    pa = jnp.dot(ya_ref[...], w0_ref[...], preferred_element_type=F32)
    pb = jnp.dot(yb_ref[...], w1_ref[...], preferred_element_type=F32)
    g0 = jax.nn.sigmoid(g0_ref[...].astype(F32) + b0_ref[...])
    g1 = jax.nn.sigmoid(g1_ref[...].astype(F32) + b1_ref[...])
    merged = (g0 * pa + g1 * pb).astype(BF16)
    out = jnp.dot(merged, wo_ref[...], preferred_element_type=F32)
    y = out * lax.rsqrt(jnp.mean(out * out, axis=-1, keepdims=True) + NORM_EPS) * nw_ref[...]
    o_ref[...] = x_ref[...] + y


def _merge(ya, yb, p2, x2d, w0, w1, wo, b_gate, norm_w, col_g0, col_g1, tm=512):
    t, d = x2d.shape
    row = lambda c: pl.BlockSpec((tm, d), lambda i: (i, c))
    full = pl.BlockSpec((d, d), lambda i: (0, 0))
    vec = lambda c: pl.BlockSpec((1, d), lambda i: (0, c))
    return pl.pallas_call(
        _merge_kernel,
        grid=(t // tm,),
        in_specs=[row(0), row(0), row(col_g0), row(col_g1), row(0), full, full, full, vec(0), vec(1), vec(0)],
        out_specs=row(0),
        out_shape=jax.ShapeDtypeStruct((t, d), F32),
        compiler_params=_cparams(("parallel",)),
        name="merge_out",
    )(ya, yb, p2, p2, x2d, w0, w1, wo, b_gate.reshape(1, -1), b_gate.reshape(1, -1), norm_w.reshape(1, d))


def kernel(x, norm_pre_w, w_in, b_gate, conv_w, dn_a_log, dn_dt_bias, dn_onorm_w, rel_bias, w_branch, w_out,
           norm_post_w):
    b, s, d = x.shape
    n_heads = dn_a_log.shape[0]
    hd = dn_onorm_w.shape[0]
    width = n_heads * hd
    assert hd == LANES and rel_bias.shape[1] == n_heads and width == d
    assert s % MOBA_BLOCK == 0 and s % DN_GROUP == 0 and 2 * n_heads <= LANES
    t = b * s

    c_za = 3 * width
    c_beta = 4 * width
    c_qb = c_beta + 2 * n_heads
    assert conv_w.shape[1] == c_za
    w16 = w_in.astype(BF16)
    w_qkv_a = w16[:, :c_za]
    w_rest = jnp.concatenate([w16[:, c_za:c_beta], w16[:, c_qb:]], axis=1)
    w_small = jnp.pad(w16[:, c_beta:c_qb], ((0, 0), (0, LANES - 2 * n_heads)))
    nblk = width // LANES
    col = lambda k: k * nblk

    x2d = x.reshape(t, d)
    norm_w = norm_pre_w.astype(F32)
    col_scale = jnp.concatenate([jnp.ones((1, width), F32), jnp.full((1, width), hd ** -0.5, F32),
                                 jnp.ones((1, w_rest.shape[1] - 2 * width), F32)], axis=1)
    qkv_a, rest, small = _in_proj(x2d, norm_w, w_qkv_a, w_rest, col_scale, w_small, conv_w.astype(F32), s)
    qkv_a3 = qkv_a.reshape(b, s, -1)
    rest3 = rest.reshape(b, s, -1)

    pad = (0, LANES - 2 * n_heads)
    a_log_row = jnp.pad(jnp.concatenate([jnp.zeros_like(dn_a_log), dn_a_log]), pad).reshape(1, LANES).astype(F32)
    dt_row = jnp.pad(jnp.concatenate([jnp.zeros_like(dn_dt_bias), dn_dt_bias]), pad).reshape(1, LANES).astype(F32)
    ya = _deltanet(qkv_a3, rest3, small.reshape(b, s, LANES), a_log_row, dt_row,
                   dn_onorm_w.reshape(1, hd).astype(F32), n_heads, col(0), col(1), col(2), col(0))
    yb = _moba(rel_bias.astype(F32), rest3, n_heads, col(1), col(2), col(3), col(4))

    out = _merge(ya.reshape(t, d), yb.reshape(t, d), rest, x2d,
                 w_branch[0].astype(BF16), w_branch[1].astype(BF16), w_out.astype(BF16),
                 b_gate.astype(F32), norm_post_w.astype(F32), 5, 6)
    return out.reshape(b, s, d)
```

```python
import functools
import math

import jax
import jax.numpy as jnp
from jax import lax
from jax.experimental import pallas as pl
from jax.experimental.pallas import tpu as pltpu

F32 = jnp.float32
BF16 = jnp.bfloat16

LANES = 128
DN_CHUNK = 64
DN_GROUP = 128
DN_CONV_HALO = 8
MOBA_BLOCK = 256
MOBA_TOPK = 3
MOBA_SHIFT_SLACK = 40.0
REL_BUCKETS = 32
REL_MAX_DIST = 2048
NORM_EPS = 1e-6
NEG_INF = -1e30
VMEM_LIMIT = 56 * 1024 * 1024
NT_DIMS = (((1,), (1,)), ((), ()))


def _cparams(sem):
    return pltpu.CompilerParams(dimension_semantics=sem, vmem_limit_bytes=VMEM_LIMIT)


def _normalise(x, w):
    y = x * lax.rsqrt(jnp.mean(x * x, axis=-1, keepdims=True) + NORM_EPS)
    return (y * w).astype(BF16)


def _in_proj_kernel(x_ref, xh_ref, nw_ref, wc_ref, wr_ref, cs_ref, ws_ref, cw_ref, oc_ref, or_ref, sm_ref,
                    h_ref, hh_ref, *, tiles_per_seq):
    i = pl.program_id(0)
    tm = oc_ref.shape[0]
    halo = hh_ref.shape[0]

    @pl.when(pl.program_id(1) == 0)
    def _():
        h = _normalise(x_ref[...], nw_ref[...])
        h_ref[...] = h
        sm_ref[...] = jnp.dot(h, ws_ref[...], preferred_element_type=F32)
        hh = _normalise(xh_ref[...], nw_ref[...])
        hh_ref[...] = jnp.where(i % tiles_per_seq == 0, jnp.zeros_like(hh), hh)

    h = h_ref[...]
    wc = wc_ref[...]
    ext = jnp.concatenate([jnp.dot(hh_ref[...], wc, preferred_element_type=F32),
                           jnp.dot(h, wc, preferred_element_type=F32)], axis=0)
    acc = jnp.dot(h, wr_ref[...], preferred_element_type=F32)
    or_ref[...] = (acc * cs_ref[...]).astype(or_ref.dtype)

    cw = cw_ref[...]
    kw = cw.shape[0]
    base = halo - (kw - 1)
    y = ext[base:base + tm] * cw[0:1]
    for kk in range(1, kw):
        y = y + ext[base + kk:base + kk + tm] * cw[kk:kk + 1]
    oc_ref[...] = (y * jax.nn.sigmoid(y)).astype(oc_ref.dtype)


def _in_proj(x2d, norm_w, w_conv, w_rest, col_scale, w_small, conv_w, seq_len, tm=1024, steps=4, halo=16):
    t, d = x2d.shape
    nc, nr, ns = w_conv.shape[1], w_rest.shape[1], w_small.shape[1]
    kw = conv_w.shape[0]
    assert nc % (steps * LANES) == 0 and nr % (steps * LANES) == 0
    assert seq_len % tm == 0 and tm % halo == 0 and kw - 1 <= halo and conv_w.shape[1] == nc
    tc, tr = nc // steps, nr // steps
    row = lambda w: pl.BlockSpec((tm, w), lambda i, j: (i, j))
    col = lambda r, w: pl.BlockSpec((r, w), lambda i, j: (0, j))
    fixed = lambda r, w: pl.BlockSpec((r, w), lambda i, j: (0, 0))
    return pl.pallas_call(
        functools.partial(_in_proj_kernel, tiles_per_seq=seq_len // tm),
        grid=(t // tm, steps),
        in_specs=[pl.BlockSpec((tm, d), lambda i, j: (i, 0)),
                  pl.BlockSpec((halo, d), lambda i, j: (jnp.maximum(i * (tm // halo) - 1, 0), 0)),
                  fixed(1, d), col(d, tc), col(d, tr), col(1, tr), fixed(d, ns), col(kw, tc)],
        out_specs=[row(tc), row(tr), pl.BlockSpec((tm, ns), lambda i, j: (i, 0))],
        out_shape=[jax.ShapeDtypeStruct((t, nc), BF16), jax.ShapeDtypeStruct((t, nr), BF16),
                   jax.ShapeDtypeStruct((t, ns), F32)],
        scratch_shapes=[pltpu.VMEM((tm, d), BF16), pltpu.VMEM((halo, d), BF16)],
        compiler_params=_cparams(("parallel", "arbitrary")),
        name="in_proj",
    )(x2d, x2d, norm_w.reshape(1, d), w_conv, w_rest, col_scale, w_small, conv_w)


def _deltanet_kernel(q_ref, k_ref, v_ref, z_ref, sm_ref, alog_ref, dtb_ref, onw_ref,
                     o_ref, state_ref, vnew_ref, gct_ref, *, n_heads, heads_per_step):
    hg = pl.program_id(1)
    g = pl.program_id(2)
    G, C, D, NH = DN_GROUP, DN_CHUNK, LANES, heads_per_step
    lc = C.bit_length() - 1

    @pl.when(g == 0)
    def _():
        state_ref[...] = jnp.zeros_like(state_ref)

    def l2n(x):
        return x * lax.rsqrt(jnp.sum(x * x, axis=-1, keepdims=True) + NORM_EPS)

    qf = q_ref[0].astype(F32)
    kf = k_ref[0].astype(F32)
    vf = v_ref[0].astype(F32)

    cs = sm_ref[0]
    lane = lax.broadcasted_iota(jnp.int32, (G, LANES), 1)
    row = lax.broadcasted_iota(jnp.int32, (G, LANES), 0)
    beta_all = jax.nn.sigmoid(cs)
    xg = cs + dtb_ref[...]
    softplus = jnp.maximum(xg, 0.0) + jnp.log1p(jnp.exp(-jnp.abs(xg)))
    g_all = -jnp.exp(alog_ref[...]) * softplus
    pos = row & (C - 1)
    gc_all = g_all
    shift = 1
    while shift < C:
        gc_all = gc_all + jnp.where(pos >= shift, pltpu.roll(gc_all, shift, 0), 0.0)
        shift *= 2
    gl_all = jnp.concatenate([jnp.broadcast_to(gc_all[c * C + C - 1:c * C + C], (C, LANES))
                              for c in range(G // C)], axis=0)
    gct_ref[...] = gc_all.T

    ri = lax.broadcasted_iota(jnp.int32, (G, G), 0)
    ci = lax.broadcasted_iota(jnp.int32, (G, G), 1)
    same = (ri >> lc) == (ci >> lc)
    tri_incl = same & (ri >= ci)
    tri_strict = same & (ri > ci)
    eye = jnp.where(ri == ci, 1.0, 0.0)
    ct = lax.broadcasted_iota(jnp.int32, (D, G), 1) >> lc

    def pair_mask(s):
        ls = s.bit_length() - 1
        return ((ri >> (ls + 1)) == (ci >> (ls + 1))) & (((ri >> ls) & 1) == 1) & (((ci >> ls) & 1) == 0)

    pair_masks = {}
    s = 1
    while s < C:
        pair_masks[s] = pair_mask(s)
        s *= 2

    heads = range(NH)
    sls = [slice(hl * D, (hl + 1) * D) for hl in heads]
    hidx = [hg * NH + hl for hl in heads]

    def col_of(x, lane_idx):
        return jnp.sum(jnp.where(lane == lane_idx, x, 0.0), axis=1, keepdims=True)

    q = [l2n(qf[:, sl]) * (D ** -0.5) for sl in sls]
    k = [l2n(kf[:, sl]) for sl in sls]
    beta_col = [col_of(beta_all, h) for h in hidx]
    gc_col = [col_of(gc_all, h + n_heads) for h in hidx]
    gl_col = [col_of(gl_all, h + n_heads) for h in hidx]
    gc_row = [gct_ref[pl.ds(h + n_heads, 1), :] for h in hidx]
    decay = [jnp.where(tri_incl, jnp.exp(jnp.where(tri_incl, gc_col[a] - gc_row[a], 0.0)), 0.0) for a in heads]
    kb = [k[a] * beta_col[a] for a in heads]
    vb = [vf[:, sls[a]] * beta_col[a] for a in heads]
    k16 = [x.astype(BF16) for x in k]
    kk = [lax.dot_general(kb[a].astype(BF16), k16[a], NT_DIMS, preferred_element_type=F32) for a in heads]
    qk = [lax.dot_general(q[a].astype(BF16), k16[a], NT_DIMS, preferred_element_type=F32) for a in heads]
    lower = [jnp.where(tri_strict, kk[a] * decay[a], 0.0) for a in heads]
    attn = [jnp.where(tri_incl, qk[a] * decay[a], 0.0).astype(BF16) for a in heads]

    tinv = [eye - jnp.where(pair_masks[1], lower[a], 0.0) for a in heads]
    s = 2
    while s < C:
        t16 = [x.astype(BF16) for x in tinv]
        md = [jnp.dot(jnp.where(pair_masks[s], lower[a], 0.0).astype(BF16), t16[a], preferred_element_type=F32)
              for a in heads]
        tinv = [tinv[a] - jnp.dot(t16[a], md[a].astype(BF16), preferred_element_type=F32) for a in heads]
        s *= 2

    egc = [jnp.exp(x) for x in gc_col]
    uw = [jnp.dot(tinv[a].astype(BF16), jnp.concatenate([vb[a], kb[a] * egc[a]], axis=1).astype(BF16),
                  preferred_element_type=F32) for a in heads]
    q_dec = [q[a] * egc[a] for a in heads]
    k_dec_t = [(k[a] * jnp.exp(gl_col[a] - gc_col[a])).T for a in heads]
    egl = [jnp.exp(x) for x in gl_col]

    vnew_ref[...] = jnp.zeros_like(vnew_ref)
    outs = [[] for _ in heads]
    for c in range(G // C):
        r0 = c * C
        st = [state_ref[a] for a in heads]
        ws_qs = [jnp.dot(jnp.concatenate([uw[a][r0:r0 + C, D:], q_dec[a][r0:r0 + C]], axis=0).astype(BF16),
                         st[a].astype(BF16), preferred_element_type=F32) for a in heads]
        for a in heads:
            vnew_ref[a, r0:r0 + C, :] = (uw[a][r0:r0 + C, :D] - ws_qs[a][:C]).astype(BF16)
        vn = [vnew_ref[a] for a in heads]
        for a in heads:
            outs[a].append(ws_qs[a][C:] + jnp.dot(attn[a][r0:r0 + C], vn[a], preferred_element_type=F32))
        for a in heads:
            kt_c = jnp.where(ct == c, k_dec_t[a], 0.0).astype(BF16)
            e_c = jnp.concatenate([egl[a][r0:r0 + C]] * (D // C), axis=0)
            state_ref[a] = st[a] * e_c + jnp.dot(kt_c, vn[a], preferred_element_type=F32)

    for a in heads:
        o = jnp.concatenate(outs[a], axis=0)
        o = o * lax.rsqrt(jnp.mean(o * o, axis=-1, keepdims=True) + NORM_EPS) * onw_ref[...]
        z = z_ref[0, :, sls[a]].astype(F32)
        o_ref[0, :, sls[a]] = (o * (z * jax.nn.sigmoid(z))).astype(o_ref.dtype)


def _deltanet(qkv3, z3, small3, a_log_row, dt_row, onorm_w, n_heads, col_q, col_k, col_v, col_z, heads_per_step=8):
    b, s, _ = qkv3.shape
    G, D, NH = DN_GROUP, LANES, heads_per_step
    assert n_heads % NH == 0 and all(c % NH == 0 for c in (col_q, col_k, col_v, col_z))
    blk = lambda off: pl.BlockSpec((1, G, NH * D), lambda bi, hi, gi: (bi, gi, off // NH + hi))
    row_spec = pl.BlockSpec((1, LANES), lambda bi, hi, gi: (0, 0))
    return pl.pallas_call(
        functools.partial(_deltanet_kernel, n_heads=n_heads, heads_per_step=NH),
        grid=(b, n_heads // NH, s // G),
        in_specs=[blk(col_q), blk(col_k), blk(col_v), blk(col_z),
                  pl.BlockSpec((1, G, LANES), lambda bi, hi, gi: (bi, gi, 0)),
                  row_spec, row_spec, row_spec],
        out_specs=pl.BlockSpec((1, G, NH * D), lambda bi, hi, gi: (bi, gi, hi)),
        out_shape=jax.ShapeDtypeStruct((b, s, n_heads * D), BF16),
        scratch_shapes=[pltpu.VMEM((NH, D, D), F32),
                        pltpu.VMEM((NH, G, D), BF16), pltpu.VMEM((LANES, G), F32)],
        compiler_params=_cparams(("parallel", "parallel", "arbitrary")),
        name="gated_deltanet",
    )(qkv3, qkv3, qkv3, z3, small3, a_log_row, dt_row, onorm_w)


def _t5_bucket_f32(dist):
    dist = jnp.maximum(dist, 0)
    max_exact = REL_BUCKETS // 2
    d = jnp.maximum(dist, 1).astype(F32)
    large = max_exact + (jnp.log(d / max_exact) / math.log(REL_MAX_DIST / max_exact)
                         * (REL_BUCKETS - max_exact)).astype(jnp.int32)
    large = jnp.minimum(large, REL_BUCKETS - 1)
    return jnp.where(dist < max_exact, dist, large)


def _n_near_tiles():
    return -(-(REL_MAX_DIST + MOBA_BLOCK - 1) // MOBA_BLOCK)


def _moba_kernel(rb_ref, q_ref, k_ref, v_ref, z_ref, o_ref, kmean_ref, bias_ref, etab_ref, xt_ref, qa_ref,
                 s0_ref, s1_ref, wide_ref, *, n_near, nbp, group, qblocks):
    hh = pl.program_id(0)
    bb = pl.program_id(1)
    BLK, D, U, QB = MOBA_BLOCK, LANES, group, qblocks
    R = QB * BLK
    nb = k_ref.shape[1] // BLK
    n_steps = nb // QB
    col_shift = nbp

    @pl.when(bb == 0)
    def _():
        qr = lax.broadcasted_iota(jnp.int32, (BLK, BLK), 0)
        kc = lax.broadcasted_iota(jnp.int32, (BLK, BLK), 1)
        for t in range(n_near):
            dist = t * BLK + qr - kc
            bucket = _t5_bucket_f32(dist)
            val = jnp.full((BLK, BLK), rb_ref[0, hh], F32)
            for bk in range(1, REL_BUCKETS):
                val = jnp.where(bucket == bk, rb_ref[bk, hh], val)
            if t == 0:
                val = jnp.where(dist >= 0, val, NEG_INF)
            bias_ref[t] = val
        bias_ref[n_near] = jnp.full((BLK, BLK), rb_ref[REL_BUCKETS - 1, hh], F32)
        lane = lax.broadcasted_iota(jnp.int32, (16, D), 1)
        for j in range(nb):
            etab_ref[j] = jnp.where((lane == j) | (lane == col_shift), 1.0, 0.0).astype(BF16)

    def norms(j, carry):
        k2max, q2max = carry
        rows = pl.ds(pl.multiple_of(j * BLK, BLK), BLK)
        kj = k_ref[0, rows, :].astype(F32)
        qj = q_ref[0, rows, :].astype(F32)
        kmean_ref[pl.ds(j, 1), :] = jnp.mean(kj, axis=0, keepdims=True)
        return (jnp.maximum(k2max, jnp.max(jnp.sum(kj * kj, axis=1, keepdims=True))),
                jnp.maximum(q2max, jnp.max(jnp.sum(qj * qj, axis=1, keepdims=True))))
    k2max, q2max = lax.fori_loop(0, nb, norms, (jnp.float32(0.0), jnp.float32(0.0)))
    rb_abs = jnp.abs(rb_ref[0, hh])
    for bk in range(1, REL_BUCKETS):
        rb_abs = jnp.maximum(rb_abs, jnp.abs(rb_ref[bk, hh]))
    bound = jnp.sqrt(q2max * k2max) * 1.001 + rb_abs + 1e-3
    wide_ref[0] = jnp.where(bound > MOBA_SHIFT_SLACK, 1, 0).astype(jnp.int32)

    kmean16 = kmean_ref[...].astype(BF16)
    jrow = lax.broadcasted_iota(jnp.int32, (nb, BLK), 0)
    jrow_f = jrow.astype(F32)

    def visibility(it, carry):
        for u in range(U):
            ib = it * U + u
            rows = pl.ds(pl.multiple_of(ib * BLK, BLK), BLK)
            gate = lax.dot_general(kmean16, q_ref[0, rows, :], NT_DIMS, preferred_element_type=F32)
            past = jrow < ib
            sc = jnp.where(past, gate, NEG_INF)
            visible = jrow == ib
            for _ in range(MOBA_TOPK):
                top = jnp.max(sc, axis=0, keepdims=True)
                first = jnp.min(jnp.where(sc == top, jrow_f, float(nb)), axis=0, keepdims=True)
                pick = jrow_f == first
                visible = visible | (pick & past)
                sc = jnp.where(pick, jnp.finfo(F32).min, sc)
            cols = jnp.concatenate([jnp.where(visible, 0.0, NEG_INF), jnp.zeros((D - nb, BLK), F32)], axis=0)
            xt_ref[rows, :] = cols.T.astype(BF16)
        return carry
    lax.fori_loop(0, nb // U, visibility, 0)

    def set_query_operand(s, slot):
        rows = pl.ds(pl.multiple_of(s * R, R), R)
        qa_ref[slot, :, 0:D] = q_ref[0, rows, :]
        qa_ref[slot, :, D:2 * D] = xt_ref[rows, :]

    def pair_scores(s, g, slot):
        j0 = g * U
        row0 = pl.multiple_of(j0 * BLK, U * BLK)
        e_rows = []
        for u in range(U):
            e_rows += [etab_ref[j0 + u]] * (BLK // 16)
        ka = jnp.concatenate([k_ref[0, pl.ds(row0, U * BLK), :], jnp.concatenate(e_rows, axis=0)], axis=1)
        sc = lax.dot_general(qa_ref[slot], ka, NT_DIMS, preferred_element_type=F32)
        bias = jnp.concatenate(
            [jnp.concatenate([bias_ref[jnp.clip(s * QB + a - (j0 + u), 0, n_near)] for u in range(U)], axis=1)
             for a in range(QB)], axis=0)
        return sc + bias

    def lane_chunks(x):
        return [x[:, c * D:(c + 1) * D] for c in range(x.shape[1] // D)]

    @pl.when(wide_ref[0] == 1)
    def _():
        def step_max(s, carry):
            set_query_operand(s, 0)

            def max_step(g, mm):
                for c in lane_chunks(pair_scores(s, g, 0)):
                    mm = jnp.maximum(mm, c)
                return mm
            mm = lax.fori_loop(0, s + 1, max_step, jnp.full((R, D), jnp.finfo(F32).min, F32))
            m = jnp.max(mm, axis=1, keepdims=True)
            rows = pl.ds(pl.multiple_of(s * R, R), R)
            lane_q = lax.broadcasted_iota(jnp.int32, (R, D), 1)
            xt_ref[rows, :] = jnp.where(lane_q == col_shift, -m, xt_ref[rows, :].astype(F32)).astype(BF16)
            return carry
        lax.fori_loop(0, n_steps, step_max, 0)

    def consume(g, sc, acc, lsum):
        p = jnp.exp(sc)
        for c in lane_chunks(p):
            lsum = lsum + c
        vg = v_ref[0, pl.ds(pl.multiple_of(g * (U * BLK), U * BLK), U * BLK), :]
        return acc + jnp.dot(p.astype(BF16), vg, preferred_element_type=F32), lsum

    def finish_step(s, g, acc, lsum):
        @pl.when(g == s)
        def _():
            rows = pl.ds(pl.multiple_of(s * R, R), R)
            out = acc / jnp.sum(lsum, axis=1, keepdims=True)
            z = z_ref[0, rows, :].astype(F32)
            o_ref[0, rows, :] = (out * (z * jax.nn.sigmoid(z))).astype(o_ref.dtype)
        keep = jnp.where(g == s, 0.0, 1.0).astype(F32)
        return acc * keep, lsum * keep

    def stage(cur_ref, nxt_ref, carry):
        s, g, acc, lsum = carry
        wrap = g == s
        s2 = jnp.where(wrap, s + 1, s)
        g2 = jnp.where(wrap, 0, g + 1)

        @pl.when(wrap)
        def _():
            set_query_operand(s2, s2 & 1)

        nxt_ref[...] = pair_scores(s2, g2, s2 & 1)
        acc, lsum = consume(g, cur_ref[...], acc, lsum)
        acc, lsum = finish_step(s, g, acc, lsum)
        return s2, g2, acc, lsum

    def two_stages(t, carry):
        return stage(s1_ref, s0_ref, stage(s0_ref, s1_ref, carry))

    n_pairs = n_steps * (n_steps + 1) // 2
    set_query_operand(0, 0)
    s0_ref[...] = pair_scores(0, 0, 0)
    zero = jnp.zeros((R, D), F32)
    carry = (jnp.int32(0), jnp.int32(0), zero, zero)
    carry = lax.fori_loop(0, (n_pairs - 1) // 2, two_stages, carry)
    last_ref = s0_ref
    if (n_pairs - 1) % 2:
        carry = stage(s0_ref, s1_ref, carry)
        last_ref = s1_ref
    s, g, acc, lsum = carry
    acc, lsum = consume(g, last_ref[...], acc, lsum)
    finish_step(s, g, acc, lsum)


def _moba(rel_bias, p3, n_heads, col_q, col_k, col_v, col_z, group=4, qblocks=4):
    b, s, _ = p3.shape
    BLK, D = MOBA_BLOCK, LANES
    nb = s // BLK
    nbp = -(-nb // 16) * 16
    assert nbp + 16 <= D, "block-visibility columns must fit in the spare contraction columns"
    assert group == qblocks and nb % group == 0, "step s visits key groups 0..s"
    n_near = _n_near_tiles()
    rows = qblocks * BLK
    seq = lambda off: pl.BlockSpec((1, s, D), lambda hi, bi: (bi, 0, off + hi))
    return pl.pallas_call(
        functools.partial(_moba_kernel, n_near=n_near, nbp=nbp, group=group, qblocks=qblocks),
        grid=(n_heads, b),
        in_specs=[pl.BlockSpec(memory_space=pltpu.SMEM), seq(col_q), seq(col_k), seq(col_v), seq(col_z)],
        out_specs=seq(0),
        out_shape=jax.ShapeDtypeStruct((b, s, n_heads * D), BF16),
        scratch_shapes=[pltpu.VMEM((nb, D), F32), pltpu.VMEM((n_near + 1, BLK, BLK), F32),
                        pltpu.VMEM((nb, 16, D), BF16), pltpu.VMEM((s, D), BF16), pltpu.VMEM((2, rows, 2 * D), BF16),
                        pltpu.VMEM((rows, group * BLK), F32), pltpu.VMEM((rows, group * BLK), F32),
                        pltpu.SMEM((1,), jnp.int32)],
        compiler_params=_cparams(("arbitrary", "arbitrary")),
        name="moba_attention",
    )(rel_bias, p3, p3, p3, p3)


def _merge_kernel(ya_ref, yb_ref, g0_ref, g1_ref, x_ref, w0_ref, w1_ref, wo_ref, b0_ref, b1_ref, nw_ref, o_ref):
    pa = jnp.dot(ya_ref[...], w0_ref[...], preferred_element_type=F32)
    pb = jnp.dot(yb_ref[...], w1_ref[...], preferred_element_type=F32)
    g0 = jax.nn.sigmoid(g0_ref[...].astype(F32) + b0_ref[...])
    g1 = jax.nn.sigmoid(g1_ref[...].astype(F32) + b1_ref[...])
    merged = (g0 * pa + g1 * pb).astype(BF16)
    out = jnp.dot(merged, wo_ref[...], preferred_element_type=F32)
    y = out * lax.rsqrt(jnp.mean(out * out, axis=-1, keepdims=True) + NORM_EPS) * nw_ref[...]
    o_ref[...] = x_ref[...] + y


def _merge(ya, yb, p2, x2d, w0, w1, wo, b_gate, norm_w, col_g0, col_g1, tm=512):
    t, d = x2d.shape
    row = lambda c: pl.BlockSpec((tm, d), lambda i: (i, c))
    full = pl.BlockSpec((d, d), lambda i: (0, 0))
    vec = lambda c: pl.BlockSpec((1, d), lambda i: (0, c))
    return pl.pallas_call(
        _merge_kernel,
        grid=(t // tm,),
        in_specs=[row(0), row(0), row(col_g0), row(col_g1), row(0), full, full, full, vec(0), vec(1), vec(0)],
        out_specs=row(0),
        out_shape=jax.ShapeDtypeStruct((t, d), F32),
        compiler_params=_cparams(("parallel",)),
        name="merge_out",
    )(ya, yb, p2, p2, x2d, w0, w1, wo, b_gate.reshape(1, -1), b_gate.reshape(1, -1), norm_w.reshape(1, d))


def kernel(x, norm_pre_w, w_in, b_gate, conv_w, dn_a_log, dn_dt_bias, dn_onorm_w, rel_bias, w_branch, w_out,
           norm_post_w):
    b, s, d = x.shape
    n_heads = dn_a_log.shape[0]
    hd = dn_onorm_w.shape[0]
    width = n_heads * hd
    assert hd == LANES and rel_bias.shape[1] == n_heads and width == d
    assert s % MOBA_BLOCK == 0 and s % DN_GROUP == 0 and 2 * n_heads <= LANES
    t = b * s

    c_za = 3 * width
    c_beta = 4 * width
    c_qb = c_beta + 2 * n_heads
    assert conv_w.shape[1] == c_za
    w16 = w_in.astype(BF16)
    w_qkv_a = w16[:, :c_za]
    w_rest = jnp.concatenate([w16[:, c_za:c_beta], w16[:, c_qb:]], axis=1)
    w_small = jnp.pad(w16[:, c_beta:c_qb], ((0, 0), (0, LANES - 2 * n_heads)))
    nblk = width // LANES
    col = lambda k: k * nblk

    x2d = x.reshape(t, d)
    norm_w = norm_pre_w.astype(F32)
    col_scale = jnp.concatenate([jnp.ones((1, width), F32), jnp.full((1, width), hd ** -0.5, F32),
                                 jnp.ones((1, w_rest.shape[1] - 2 * width), F32)], axis=1)
    qkv_a, rest, small = _in_proj(x2d, norm_w, w_qkv_a, w_rest, col_scale, w_small, conv_w.astype(F32), s)
    qkv_a3 = qkv_a.reshape(b, s, -1)
    rest3 = rest.reshape(b, s, -1)

    pad = (0, LANES - 2 * n_heads)
    a_log_row = jnp.pad(jnp.concatenate([jnp.zeros_like(dn_a_log), dn_a_log]), pad).reshape(1, LANES).astype(F32)
    dt_row = jnp.pad(jnp.concatenate([jnp.zeros_like(dn_dt_bias), dn_dt_bias]), pad).reshape(1, LANES).astype(F32)
    ya = _deltanet(qkv_a3, rest3, small.reshape(b, s, LANES), a_log_row, dt_row,
                   dn_onorm_w.reshape(1, hd).astype(F32), n_heads, col(0), col(1), col(2), col(0))
    yb = _moba(rel_bias.astype(F32), rest3, n_heads, col(1), col(2), col(3), col(4))

    out = _merge(ya.reshape(t, d), yb.reshape(t, d), rest, x2d,
                 w_branch[0].astype(BF16), w_branch[1].astype(BF16), w_out.astype(BF16),
                 b_gate.astype(F32), norm_post_w.astype(F32), 5, 6)
    return out.reshape(b, s, d)
```

```python
import functools
import math

import jax
import jax.numpy as jnp
from jax import lax
from jax.experimental import pallas as pl
from jax.experimental.pallas import tpu as pltpu

F32 = jnp.float32
BF16 = jnp.bfloat16

LANES = 128
DN_CHUNK = 64
DN_GROUP = 128
DN_CONV_HALO = 8
MOBA_BLOCK = 256
MOBA_TOPK = 3
MOBA_SHIFT_SLACK = 40.0
REL_BUCKETS = 32
REL_MAX_DIST = 2048
NORM_EPS = 1e-6
NEG_INF = -1e30
VMEM_LIMIT = 56 * 1024 * 1024
NT_DIMS = (((1,), (1,)), ((), ()))


def _cparams(sem):
    return pltpu.CompilerParams(dimension_semantics=sem, vmem_limit_bytes=VMEM_LIMIT)


def _sigmoid(x):
    return 0.5 * jnp.tanh(0.5 * x) + 0.5


def _silu(x):
    h = 0.5 * x
    return h * jnp.tanh(h) + h


def _normalise(x, w):
    y = x * lax.rsqrt(jnp.mean(x * x, axis=-1, keepdims=True) + NORM_EPS)
    return (y * w).astype(BF16)


def _in_proj_kernel(x_ref, xh_ref, nw_ref, wc_ref, wr_ref, cs_ref, ws_ref, cw_ref, oc_ref, or_ref, sm_ref,
                    h_ref, hh_ref, *, tiles_per_seq):
    i = pl.program_id(0)
    tm = oc_ref.shape[0]
    halo = hh_ref.shape[0]

    @pl.when(pl.program_id(1) == 0)
    def _():
        h = _normalise(x_ref[...], nw_ref[...])
        h_ref[...] = h
        sm_ref[...] = jnp.dot(h, ws_ref[...], preferred_element_type=F32)
        hh = _normalise(xh_ref[...], nw_ref[...])
        hh_ref[...] = jnp.where(i % tiles_per_seq == 0, jnp.zeros_like(hh), hh)

    h = h_ref[...]
    wc = wc_ref[...]
    ext = jnp.concatenate([jnp.dot(hh_ref[...], wc, preferred_element_type=F32),
                           jnp.dot(h, wc, preferred_element_type=F32)], axis=0)
    acc = jnp.dot(h, wr_ref[...], preferred_element_type=F32)
    or_ref[...] = (acc * cs_ref[...]).astype(or_ref.dtype)

    cw = cw_ref[...]
    kw = cw.shape[0]
    base = halo - (kw - 1)
    y = ext[base:base + tm] * cw[0:1]
    for kk in range(1, kw):
        y = y + ext[base + kk:base + kk + tm] * cw[kk:kk + 1]
    oc_ref[...] = _silu(y).astype(oc_ref.dtype)


def _in_proj(x2d, norm_w, w_conv, w_rest, col_scale, w_small, conv_w, seq_len, tm=1024, steps=4, halo=16):
    t, d = x2d.shape
    nc, nr, ns = w_conv.shape[1], w_rest.shape[1], w_small.shape[1]
    kw = conv_w.shape[0]
    assert nc % (steps * LANES) == 0 and nr % (steps * LANES) == 0
    assert seq_len % tm == 0 and tm % halo == 0 and kw - 1 <= halo and conv_w.shape[1] == nc
    tc, tr = nc // steps, nr // steps
    row = lambda w: pl.BlockSpec((tm, w), lambda i, j: (i, j))
    col = lambda r, w: pl.BlockSpec((r, w), lambda i, j: (0, j))
    fixed = lambda r, w: pl.BlockSpec((r, w), lambda i, j: (0, 0))
    return pl.pallas_call(
        functools.partial(_in_proj_kernel, tiles_per_seq=seq_len // tm),
        grid=(t // tm, steps),
        in_specs=[pl.BlockSpec((tm, d), lambda i, j: (i, 0)),
                  pl.BlockSpec((halo, d), lambda i, j: (jnp.maximum(i * (tm // halo) - 1, 0), 0)),
                  fixed(1, d), col(d, tc), col(d, tr), col(1, tr), fixed(d, ns), col(kw, tc)],
        out_specs=[row(tc), row(tr), pl.BlockSpec((tm, ns), lambda i, j: (i, 0))],
        out_shape=[jax.ShapeDtypeStruct((t, nc), BF16), jax.ShapeDtypeStruct((t, nr), BF16),
                   jax.ShapeDtypeStruct((t, ns), F32)],
        scratch_shapes=[pltpu.VMEM((tm, d), BF16), pltpu.VMEM((halo, d), BF16)],
        compiler_params=_cparams(("parallel", "arbitrary")),
        name="in_proj",
    )(x2d, x2d, norm_w.reshape(1, d), w_conv, w_rest, col_scale, w_small, conv_w)


def _deltanet_kernel(q_ref, k_ref, v_ref, z_ref, sm_ref, alog_ref, dtb_ref, onw_ref,
                     o_ref, state_ref, vnew_ref, gct_ref, *, n_heads, heads_per_step):
    hg = pl.program_id(1)
    g = pl.program_id(2)
    G, C, D, NH = DN_GROUP, DN_CHUNK, LANES, heads_per_step
    lc = C.bit_length() - 1

    @pl.when(g == 0)
    def _():
        state_ref[...] = jnp.zeros_like(state_ref)

    def l2n(x):
        return x * lax.rsqrt(jnp.sum(x * x, axis=-1, keepdims=True) + NORM_EPS)

    qf = q_ref[0].astype(F32)
    kf = k_ref[0].astype(F32)
    vf = v_ref[0].astype(F32)

    cs = sm_ref[0]
    lane = lax.broadcasted_iota(jnp.int32, (G, LANES), 1)
    row = lax.broadcasted_iota(jnp.int32, (G, LANES), 0)
    beta_all = _sigmoid(cs)
    xg = cs + dtb_ref[...]
    softplus = jnp.maximum(xg, 0.0) + jnp.log1p(jnp.exp(-jnp.abs(xg)))
    g_all = -jnp.exp(alog_ref[...]) * softplus
    pos = row & (C - 1)
    gc_all = g_all
    shift = 1
    while shift < C:
        gc_all = gc_all + jnp.where(pos >= shift, pltpu.roll(gc_all, shift, 0), 0.0)
        shift *= 2
    gl_all = jnp.concatenate([jnp.broadcast_to(gc_all[c * C + C - 1:c * C + C], (C, LANES))
                              for c in range(G // C)], axis=0)
    gct_ref[...] = gc_all.T

    ri = lax.broadcasted_iota(jnp.int32, (G, G), 0)
    ci = lax.broadcasted_iota(jnp.int32, (G, G), 1)
    same = (ri >> lc) == (ci >> lc)
    tri_incl = same & (ri >= ci)
    tri_strict = same & (ri > ci)
    eye = jnp.where(ri == ci, 1.0, 0.0)
    ct = lax.broadcasted_iota(jnp.int32, (D, G), 1) >> lc

    def pair_mask(s):
        ls = s.bit_length() - 1
        return ((ri >> (ls + 1)) == (ci >> (ls + 1))) & (((ri >> ls) & 1) == 1) & (((ci >> ls) & 1) == 0)

    pair_masks = {}
    s = 1
    while s < C:
        pair_masks[s] = pair_mask(s)
        s *= 2

    heads = range(NH)
    sls = [slice(hl * D, (hl + 1) * D) for hl in heads]
    hidx = [hg * NH + hl for hl in heads]

    def col_of(x, lane_idx):
        return jnp.sum(jnp.where(lane == lane_idx, x, 0.0), axis=1, keepdims=True)

    q = [l2n(qf[:, sl]) * (D ** -0.5) for sl in sls]
    k = [l2n(kf[:, sl]) for sl in sls]
    beta_col = [col_of(beta_all, h) for h in hidx]
    gc_col = [col_of(gc_all, h + n_heads) for h in hidx]
    gl_col = [col_of(gl_all, h + n_heads) for h in hidx]
    gc_row = [gct_ref[pl.ds(h + n_heads, 1), :] for h in hidx]
    decay = [jnp.where(tri_incl, jnp.exp(jnp.where(tri_incl, gc_col[a] - gc_row[a], 0.0)), 0.0) for a in heads]
    kb = [k[a] * beta_col[a] for a in heads]
    vb = [vf[:, sls[a]] * beta_col[a] for a in heads]
    k16 = [x.astype(BF16) for x in k]
    kk = [lax.dot_general(kb[a].astype(BF16), k16[a], NT_DIMS, preferred_element_type=F32) for a in heads]
    qk = [lax.dot_general(q[a].astype(BF16), k16[a], NT_DIMS, preferred_element_type=F32) for a in heads]
    lower = [jnp.where(tri_strict, kk[a] * decay[a], 0.0) for a in heads]
    attn = [jnp.where(tri_incl, qk[a] * decay[a], 0.0).astype(BF16) for a in heads]

    tinv = [eye - jnp.where(pair_masks[1], lower[a], 0.0) for a in heads]
    s = 2
    while s < C:
        t16 = [x.astype(BF16) for x in tinv]
        md = [jnp.dot(jnp.where(pair_masks[s], lower[a], 0.0).astype(BF16), t16[a], preferred_element_type=F32)
              for a in heads]
        tinv = [tinv[a] - jnp.dot(t16[a], md[a].astype(BF16), preferred_element_type=F32) for a in heads]
        s *= 2

    egc = [jnp.exp(x) for x in gc_col]
    uw = [jnp.dot(tinv[a].astype(BF16), jnp.concatenate([vb[a], kb[a] * egc[a]], axis=1).astype(BF16),
                  preferred_element_type=F32) for a in heads]
    q_dec = [q[a] * egc[a] for a in heads]
    k_dec_t = [(k[a] * jnp.exp(gl_col[a] - gc_col[a])).T for a in heads]
    egl = [jnp.exp(x) for x in gl_col]

    vnew_ref[...] = jnp.zeros_like(vnew_ref)
    outs = [[] for _ in heads]
    for c in range(G // C):
        r0 = c * C
        st = [state_ref[a] for a in heads]
        ws_qs = [jnp.dot(jnp.concatenate([uw[a][r0:r0 + C, D:], q_dec[a][r0:r0 + C]], axis=0).astype(BF16),
                         st[a].astype(BF16), preferred_element_type=F32) for a in heads]
        for a in heads:
            vnew_ref[a, r0:r0 + C, :] = (uw[a][r0:r0 + C, :D] - ws_qs[a][:C]).astype(BF16)
        vn = [vnew_ref[a] for a in heads]
        for a in heads:
            outs[a].append(ws_qs[a][C:] + jnp.dot(attn[a][r0:r0 + C], vn[a], preferred_element_type=F32))
        for a in heads:
            kt_c = jnp.where(ct == c, k_dec_t[a], 0.0).astype(BF16)
            e_c = jnp.concatenate([egl[a][r0:r0 + C]] * (D // C), axis=0)
            state_ref[a] = st[a] * e_c + jnp.dot(kt_c, vn[a], preferred_element_type=F32)

    for a in heads:
        o = jnp.concatenate(outs[a], axis=0)
        o = o * lax.rsqrt(jnp.mean(o * o, axis=-1, keepdims=True) + NORM_EPS) * onw_ref[...]
        z = z_ref[0, :, sls[a]].astype(F32)
        o_ref[0, :, sls[a]] = (o * _silu(z)).astype(o_ref.dtype)


def _deltanet(qkv3, z3, small3, a_log_row, dt_row, onorm_w, n_heads, col_q, col_k, col_v, col_z, heads_per_step=8):
    b, s, _ = qkv3.shape
    G, D, NH = DN_GROUP, LANES, heads_per_step
    assert n_heads % NH == 0 and all(c % NH == 0 for c in (col_q, col_k, col_v, col_z))
    blk = lambda off: pl.BlockSpec((1, G, NH * D), lambda bi, hi, gi: (bi, gi, off // NH + hi))
    row_spec = pl.BlockSpec((1, LANES), lambda bi, hi, gi: (0, 0))
    return pl.pallas_call(
        functools.partial(_deltanet_kernel, n_heads=n_heads, heads_per_step=NH),
        grid=(b, n_heads // NH, s // G),
        in_specs=[blk(col_q), blk(col_k), blk(col_v), blk(col_z),
                  pl.BlockSpec((1, G, LANES), lambda bi, hi, gi: (bi, gi, 0)),
                  row_spec, row_spec, row_spec],
        out_specs=pl.BlockSpec((1, G, NH * D), lambda bi, hi, gi: (bi, gi, hi)),
        out_shape=jax.ShapeDtypeStruct((b, s, n_heads * D), BF16),
        scratch_shapes=[pltpu.VMEM((NH, D, D), F32),
                        pltpu.VMEM((NH, G, D), BF16), pltpu.VMEM((LANES, G), F32)],
        compiler_params=_cparams(("parallel", "parallel", "arbitrary")),
        name="gated_deltanet",
    )(qkv3, qkv3, qkv3, z3, small3, a_log_row, dt_row, onorm_w)


def _t5_bucket_f32(dist):
    dist = jnp.maximum(dist, 0)
    max_exact = REL_BUCKETS // 2
    d = jnp.maximum(dist, 1).astype(F32)
    large = max_exact + (jnp.log(d / max_exact) / math.log(REL_MAX_DIST / max_exact)
                         * (REL_BUCKETS - max_exact)).astype(jnp.int32)
    large = jnp.minimum(large, REL_BUCKETS - 1)
    return jnp.where(dist < max_exact, dist, large)


def _n_near_tiles():
    return -(-(REL_MAX_DIST + MOBA_BLOCK - 1) // MOBA_BLOCK)


def _moba_kernel(rb_ref, q_ref, k_ref, v_ref, z_ref, o_ref, kmean_ref, bias_ref, etab_ref, xt_ref, qa_ref,
                 s0_ref, s1_ref, wide_ref, *, n_near, nbp, group, qblocks):
    hh = pl.program_id(0)
    bb = pl.program_id(1)
    BLK, D, U, QB = MOBA_BLOCK, LANES, group, qblocks
    R = QB * BLK
    nb = k_ref.shape[1] // BLK
    n_steps = nb // QB
    col_shift = nbp

    @pl.when(bb == 0)
    def _():
        qr = lax.broadcasted_iota(jnp.int32, (BLK, BLK), 0)
        kc = lax.broadcasted_iota(jnp.int32, (BLK, BLK), 1)
        for t in range(n_near):
            dist = t * BLK + qr - kc
            bucket = _t5_bucket_f32(dist)
            val = jnp.full((BLK, BLK), rb_ref[0, hh], F32)
            for bk in range(1, REL_BUCKETS):
                val = jnp.where(bucket == bk, rb_ref[bk, hh], val)
            if t == 0:
                val = jnp.where(dist >= 0, val, NEG_INF)
            bias_ref[t] = val
        bias_ref[n_near] = jnp.full((BLK, BLK), rb_ref[REL_BUCKETS - 1, hh], F32)
        lane = lax.broadcasted_iota(jnp.int32, (16, D), 1)
        for j in range(nb):
            etab_ref[j] = jnp.where((lane == j) | (lane == col_shift), 1.0, 0.0).astype(BF16)

    ones_dd = jnp.ones((D, D), BF16)

    def norms(it, carry):
        k2, q2 = carry
        for u in range(U):
            j = it * U + u
            rows = pl.ds(pl.multiple_of(j * BLK, BLK), BLK)
            kj = k_ref[0, rows, :].astype(F32)
            qj = q_ref[0, rows, :].astype(F32)
            kmean_ref[pl.ds(j, 1), :] = jnp.mean(kj, axis=0, keepdims=True)
            k2 = jnp.maximum(k2, jnp.dot((kj * kj).astype(BF16), ones_dd, preferred_element_type=F32))
            q2 = jnp.maximum(q2, jnp.dot((qj * qj).astype(BF16), ones_dd, preferred_element_type=F32))
        return k2, q2
    k2, q2 = lax.fori_loop(0, nb // U, norms, (jnp.zeros((BLK, D), F32), jnp.zeros((BLK, D), F32)))
    rb_abs = jnp.abs(rb_ref[0, hh])
    for bk in range(1, REL_BUCKETS):
        rb_abs = jnp.maximum(rb_abs, jnp.abs(rb_ref[bk, hh]))
    bound = jnp.sqrt(jnp.max(q2) * jnp.max(k2)) * 1.02 + rb_abs + 1e-3
    wide_ref[0] = jnp.where(bound > MOBA_SHIFT_SLACK, 1, 0).astype(jnp.int32)

    kmean16 = kmean_ref[...].astype(BF16)
    jrow = lax.broadcasted_iota(jnp.int32, (nb, BLK), 0)
    jrow_f = jrow.astype(F32)

    def visibility(it, carry):
        for u in range(U):
            ib = it * U + u
            rows = pl.ds(pl.multiple_of(ib * BLK, BLK), BLK)
            gate = lax.dot_general(kmean16, q_ref[0, rows, :], NT_DIMS, preferred_element_type=F32)
            past = jrow < ib
            sc = jnp.where(past, gate, NEG_INF)
            visible = jrow == ib
            for _ in range(MOBA_TOPK):
                top = jnp.max(sc, axis=0, keepdims=True)
                first = jnp.min(jnp.where(sc == top, jrow_f, float(nb)), axis=0, keepdims=True)
                pick = jrow_f == first
                visible = visible | (pick & past)
                sc = jnp.where(pick, jnp.finfo(F32).min, sc)
            cols = jnp.concatenate([jnp.where(visible, 0.0, NEG_INF), jnp.zeros((D - nb, BLK), F32)], axis=0)
            xt_ref[rows, :] = cols.T.astype(BF16)
        return carry
    lax.fori_loop(0, nb // U, visibility, 0)

    def set_query_operand(s, slot):
        rows = pl.ds(pl.multiple_of(s * R, R), R)
        qa_ref[slot, :, 0:D] = q_ref[0, rows, :]
        qa_ref[slot, :, D:2 * D] = xt_ref[rows, :]

    def pair_scores(s, g, slot):
        j0 = g * U
        row0 = pl.multiple_of(j0 * BLK, U * BLK)
        e_rows = []
        for u in range(U):
            e_rows += [etab_ref[j0 + u]] * (BLK // 16)
        ka = jnp.concatenate([k_ref[0, pl.ds(row0, U * BLK), :], jnp.concatenate(e_rows, axis=0)], axis=1)
        sc = lax.dot_general(qa_ref[slot], ka, NT_DIMS, preferred_element_type=F32)
        bias = jnp.concatenate(
            [jnp.concatenate([bias_ref[jnp.clip(s * QB + a - (j0 + u), 0, n_near)] for u in range(U)], axis=1)
             for a in range(QB)], axis=0)
        return sc + bias

    def lane_chunks(x):
        return [x[:, c * D:(c + 1) * D] for c in range(x.shape[1] // D)]

    @pl.when(wide_ref[0] == 1)
    def _():
        def step_max(s, carry):
            set_query_operand(s, 0)

            def max_step(g, mm):
                for c in lane_chunks(pair_scores(s, g, 0)):
                    mm = jnp.maximum(mm, c)
                return mm
            mm = lax.fori_loop(0, s + 1, max_step, jnp.full((R, D), jnp.finfo(F32).min, F32))
            m = jnp.max(mm, axis=1, keepdims=True)
            rows = pl.ds(pl.multiple_of(s * R, R), R)
            lane_q = lax.broadcasted_iota(jnp.int32, (R, D), 1)
            xt_ref[rows, :] = jnp.where(lane_q == col_shift, -m, xt_ref[rows, :].astype(F32)).astype(BF16)
            return carry
        lax.fori_loop(0, n_steps, step_max, 0)

    def consume(g, sc, acc, lsum):
        p = jnp.exp(sc)
        for c in lane_chunks(p):
            lsum = lsum + c
        vg = v_ref[0, pl.ds(pl.multiple_of(g * (U * BLK), U * BLK), U * BLK), :]
        return acc + jnp.dot(p.astype(BF16), vg, preferred_element_type=F32), lsum

    def finish_step(s, g, acc, lsum):
        @pl.when(g == s)
        def _():
            rows = pl.ds(pl.multiple_of(s * R, R), R)
            out = acc / jnp.sum(lsum, axis=1, keepdims=True)
            z = z_ref[0, rows, :].astype(F32)
            o_ref[0, rows, :] = (out * _silu(z)).astype(o_ref.dtype)
        keep = jnp.where(g == s, 0.0, 1.0).astype(F32)
        return acc * keep, lsum * keep

    def stage(cur_ref, nxt_ref, carry):
        s, g, acc, lsum = carry
        wrap = g == s
        s2 = jnp.where(wrap, s + 1, s)
        g2 = jnp.where(wrap, 0, g + 1)

        @pl.when(wrap)
        def _():
            set_query_operand(s2, s2 & 1)

        nxt_ref[...] = pair_scores(s2, g2, s2 & 1)
        acc, lsum = consume(g, cur_ref[...], acc, lsum)
        acc, lsum = finish_step(s, g, acc, lsum)
        return s2, g2, acc, lsum

    def two_stages(t, carry):
        return stage(s1_ref, s0_ref, stage(s0_ref, s1_ref, carry))

    n_pairs = n_steps * (n_steps + 1) // 2
    set_query_operand(0, 0)
    s0_ref[...] = pair_scores(0, 0, 0)
    zero = jnp.zeros((R, D), F32)
    carry = (jnp.int32(0), jnp.int32(0), zero, zero)
    carry = lax.fori_loop(0, (n_pairs - 1) // 2, two_stages, carry)
    last_ref = s0_ref
    if (n_pairs - 1) % 2:
        carry = stage(s0_ref, s1_ref, carry)
        last_ref = s1_ref
    s, g, acc, lsum = carry
    acc, lsum = consume(g, last_ref[...], acc, lsum)
    finish_step(s, g, acc, lsum)


def _moba(rel_bias, p3, n_heads, col_q, col_k, col_v, col_z, group=4, qblocks=4):
    b, s, _ = p3.shape
    BLK, D = MOBA_BLOCK, LANES
    nb = s // BLK
    nbp = -(-nb // 16) * 16
    assert nbp + 16 <= D, "block-visibility columns must fit in the spare contraction columns"
    assert group == qblocks and nb % group == 0, "step s visits key groups 0..s"
    n_near = _n_near_tiles()
    rows = qblocks * BLK
    seq = lambda off: pl.BlockSpec((1, s, D), lambda hi, bi: (bi, 0, off + hi))
    return pl.pallas_call(
        functools.partial(_moba_kernel, n_near=n_near, nbp=nbp, group=group, qblocks=qblocks),
        grid=(n_heads, b),
        in_specs=[pl.BlockSpec(memory_space=pltpu.SMEM), seq(col_q), seq(col_k), seq(col_v), seq(col_z)],
        out_specs=seq(0),
        out_shape=jax.ShapeDtypeStruct((b, s, n_heads * D), BF16),
        scratch_shapes=[pltpu.VMEM((nb, D), F32), pltpu.VMEM((n_near + 1, BLK, BLK), F32),
                        pltpu.VMEM((nb, 16, D), BF16), pltpu.VMEM((s, D), BF16), pltpu.VMEM((2, rows, 2 * D), BF16),
                        pltpu.VMEM((rows, group * BLK), F32), pltpu.VMEM((rows, group * BLK), F32),
                        pltpu.SMEM((1,), jnp.int32)],
        compiler_params=_cparams(("arbitrary", "arbitrary")),
        name="moba_attention",
    )(rel_bias, p3, p3, p3, p3)


def _merge_kernel(ya_ref, yb_ref, g0_ref, g1_ref, x_ref, w0_ref, w1_ref, wo_ref, b0_ref, b1_ref, nw_ref, o_ref):
    pa = jnp.dot(ya_ref[...], w0_ref[...], preferred_element_type=F32)
    pb = jnp.dot(yb_ref[...], w1_ref[...], preferred_element_type=F32)
    g0 = _sigmoid(g0_ref[...].astype(F32) + b0_ref[...])
    g1 = _sigmoid(g1_ref[...].astype(F32) + b1_ref[...])
    merged = (g0 * pa + g1 * pb).astype(BF16)
    out = jnp.dot(merged, wo_ref[...], preferred_element_type=F32)
    y = out * lax.rsqrt(jnp.mean(out * out, axis=-1, keepdims=True) + NORM_EPS) * nw_ref[...]
    o_ref[...] = x_ref[...] + y


def _merge(ya, yb, p2, x2d, w0, w1, wo, b_gate, norm_w, col_g0, col_g1, tm=512):
    t, d = x2d.shape
    row = lambda c: pl.BlockSpec((tm, d), lambda i: (i, c))
    full = pl.BlockSpec((d, d), lambda i: (0, 0))
    vec = lambda c: pl.BlockSpec((1, d), lambda i: (0, c))
    return pl.pallas_call(
        _merge_kernel,
        grid=(t // tm,),
        in_specs=[row(0), row(0), row(col_g0), row(col_g1), row(0), full, full, full, vec(0), vec(1), vec(0)],
        out_specs=row(0),
        out_shape=jax.ShapeDtypeStruct((t, d), F32),
        compiler_params=_cparams(("parallel",)),
        name="merge_out",
    )(ya, yb, p2, p2, x2d, w0, w1, wo, b_gate.reshape(1, -1), b_gate.reshape(1, -1), norm_w.reshape(1, d))


def kernel(x, norm_pre_w, w_in, b_gate, conv_w, dn_a_log, dn_dt_bias, dn_onorm_w, rel_bias, w_branch, w_out,
           norm_post_w):
    b, s, d = x.shape
    n_heads = dn_a_log.shape[0]
    hd = dn_onorm_w.shape[0]
    width = n_heads * hd
    assert hd == LANES and rel_bias.shape[1] == n_heads and width == d
    assert s % MOBA_BLOCK == 0 and s % DN_GROUP == 0 and 2 * n_heads <= LANES
    t = b * s

    c_za = 3 * width
    c_beta = 4 * width
    c_qb = c_beta + 2 * n_heads
    assert conv_w.shape[1] == c_za
    w16 = w_in.astype(BF16)
    w_qkv_a = w16[:, :c_za]
    w_rest = jnp.concatenate([w16[:, c_za:c_beta], w16[:, c_qb:]], axis=1)
    w_small = jnp.pad(w16[:, c_beta:c_qb], ((0, 0), (0, LANES - 2 * n_heads)))
    nblk = width // LANES
    col = lambda k: k * nblk

    x2d = x.reshape(t, d)
    norm_w = norm_pre_w.astype(F32)
    col_scale = jnp.concatenate([jnp.ones((1, width), F32), jnp.full((1, width), hd ** -0.5, F32),
                                 jnp.ones((1, w_rest.shape[1] - 2 * width), F32)], axis=1)
    qkv_a, rest, small = _in_proj(x2d, norm_w, w_qkv_a, w_rest, col_scale, w_small, conv_w.astype(F32), s)
    qkv_a3 = qkv_a.reshape(b, s, -1)
    rest3 = rest.reshape(b, s, -1)

    pad = (0, LANES - 2 * n_heads)
    a_log_row = jnp.pad(jnp.concatenate([jnp.zeros_like(dn_a_log), dn_a_log]), pad).reshape(1, LANES).astype(F32)
    dt_row = jnp.pad(jnp.concatenate([jnp.zeros_like(dn_dt_bias), dn_dt_bias]), pad).reshape(1, LANES).astype(F32)
    ya = _deltanet(qkv_a3, rest3, small.reshape(b, s, LANES), a_log_row, dt_row,
                   dn_onorm_w.reshape(1, hd).astype(F32), n_heads, col(0), col(1), col(2), col(0))
    yb = _moba(rel_bias.astype(F32), rest3, n_heads, col(1), col(2), col(3), col(4))

    out = _merge(ya.reshape(t, d), yb.reshape(t, d), rest, x2d,
                 w_branch[0].astype(BF16), w_branch[1].astype(BF16), w_out.astype(BF16),
                 b_gate.astype(F32), norm_post_w.astype(F32), 5, 6)
    return out.reshape(b, s, d)
```

```python
import functools
import math

import jax
import jax.numpy as jnp
from jax import lax
from jax.experimental import pallas as pl
from jax.experimental.pallas import tpu as pltpu

F32 = jnp.float32
BF16 = jnp.bfloat16

LANES = 128
DN_CHUNK = 64
DN_GROUP = 128
DN_CONV_HALO = 8
MOBA_BLOCK = 256
MOBA_TOPK = 3
MOBA_SHIFT_SLACK = 40.0
REL_BUCKETS = 32
REL_MAX_DIST = 2048
NORM_EPS = 1e-6
NEG_INF = -1e30
VMEM_LIMIT = 56 * 1024 * 1024
NT_DIMS = (((1,), (1,)), ((), ()))


def _cparams(sem):
    return pltpu.CompilerParams(dimension_semantics=sem, vmem_limit_bytes=VMEM_LIMIT)


def _sigmoid(x):
    return 0.5 * jnp.tanh(0.5 * x) + 0.5


def _silu(x):
    h = 0.5 * x
    return h * jnp.tanh(h) + h


def _normalise(x, w):
    y = x * lax.rsqrt(jnp.mean(x * x, axis=-1, keepdims=True) + NORM_EPS)
    return (y * w).astype(BF16)


def _in_proj_kernel(x_ref, xh_ref, nw_ref, wc_ref, wr_ref, cs_ref, ws_ref, cw_ref, oc_ref, or_ref, sm_ref,
                    h_ref, hh_ref, *, tiles_per_seq, row_chunks):
    i = pl.program_id(0)
    tm = oc_ref.shape[0]
    halo = hh_ref.shape[0]

    @pl.when(pl.program_id(1) == 0)
    def _():
        h = _normalise(x_ref[...], nw_ref[...])
        h_ref[...] = h
        sm_ref[...] = jnp.dot(h, ws_ref[...], preferred_element_type=F32)
        hh = _normalise(xh_ref[...], nw_ref[...])
        hh_ref[...] = jnp.where(i % tiles_per_seq == 0, jnp.zeros_like(hh), hh)

    wc = wc_ref[...]
    wr = wr_ref[...]
    cw = cw_ref[...]
    kw = cw.shape[0]
    base = halo - (kw - 1)
    rm = tm // row_chunks
    tail = jnp.dot(hh_ref[...], wc, preferred_element_type=F32)
    for r in range(row_chunks):
        rows = slice(r * rm, (r + 1) * rm)
        hr = h_ref[rows, :]
        acc_c = jnp.dot(hr, wc, preferred_element_type=F32)
        acc_r = jnp.dot(hr, wr, preferred_element_type=F32)
        or_ref[rows, :] = (acc_r * cs_ref[...]).astype(or_ref.dtype)
        ext = jnp.concatenate([tail, acc_c], axis=0)
        y = ext[base:base + rm] * cw[0:1]
        for kk in range(1, kw):
            y = y + ext[base + kk:base + kk + rm] * cw[kk:kk + 1]
        oc_ref[rows, :] = _silu(y).astype(oc_ref.dtype)
        tail = acc_c[rm - halo:rm]


def _in_proj(x2d, norm_w, w_conv, w_rest, col_scale, w_small, conv_w, seq_len, tm=1024, steps=4, halo=16,
             row_chunks=4):
    t, d = x2d.shape
    nc, nr, ns = w_conv.shape[1], w_rest.shape[1], w_small.shape[1]
    kw = conv_w.shape[0]
    assert nc % (steps * LANES) == 0 and nr % (steps * LANES) == 0
    assert seq_len % tm == 0 and tm % halo == 0 and kw - 1 <= halo and conv_w.shape[1] == nc
    tc, tr = nc // steps, nr // steps
    row = lambda w: pl.BlockSpec((tm, w), lambda i, j: (i, j))
    col = lambda r, w: pl.BlockSpec((r, w), lambda i, j: (0, j))
    fixed = lambda r, w: pl.BlockSpec((r, w), lambda i, j: (0, 0))
    return pl.pallas_call(
        functools.partial(_in_proj_kernel, tiles_per_seq=seq_len // tm, row_chunks=row_chunks),
        grid=(t // tm, steps),
        in_specs=[pl.BlockSpec((tm, d), lambda i, j: (i, 0)),
                  pl.BlockSpec((halo, d), lambda i, j: (jnp.maximum(i * (tm // halo) - 1, 0), 0)),
                  fixed(1, d), col(d, tc), col(d, tr), col(1, tr), fixed(d, ns), col(kw, tc)],
        out_specs=[row(tc), row(tr), pl.BlockSpec((tm, ns), lambda i, j: (i, 0))],
        out_shape=[jax.ShapeDtypeStruct((t, nc), BF16), jax.ShapeDtypeStruct((t, nr), BF16),
                   jax.ShapeDtypeStruct((t, ns), F32)],
        scratch_shapes=[pltpu.VMEM((tm, d), BF16), pltpu.VMEM((halo, d), BF16)],
        compiler_params=_cparams(("parallel", "arbitrary")),
        name="in_proj",
    )(x2d, x2d, norm_w.reshape(1, d), w_conv, w_rest, col_scale, w_small, conv_w)


def _deltanet_kernel(q_ref, k_ref, v_ref, z_ref, sm_ref, alog_ref, dtb_ref, onw_ref,
                     o_ref, state_ref, vnew_ref, gct_ref, *, n_heads, heads_per_step):
    hg = pl.program_id(1)
    g = pl.program_id(2)
    G, C, D, NH = DN_GROUP, DN_CHUNK, LANES, heads_per_step
    lc = C.bit_length() - 1

    @pl.when(g == 0)
    def _():
        state_ref[...] = jnp.zeros_like(state_ref)

    def l2n(x):
        return x * lax.rsqrt(jnp.sum(x * x, axis=-1, keepdims=True) + NORM_EPS)

    qf = q_ref[0].astype(F32)
    kf = k_ref[0].astype(F32)
    vf = v_ref[0].astype(F32)

    cs = sm_ref[0]
    lane = lax.broadcasted_iota(jnp.int32, (G, LANES), 1)
    row = lax.broadcasted_iota(jnp.int32, (G, LANES), 0)
    beta_all = _sigmoid(cs)
    xg = cs + dtb_ref[...]
    softplus = jnp.maximum(xg, 0.0) + jnp.log1p(jnp.exp(-jnp.abs(xg)))
    g_all = -jnp.exp(alog_ref[...]) * softplus
    pos = row & (C - 1)
    gc_all = g_all
    shift = 1
    while shift < C:
        gc_all = gc_all + jnp.where(pos >= shift, pltpu.roll(gc_all, shift, 0), 0.0)
        shift *= 2
    gl_all = jnp.concatenate([jnp.broadcast_to(gc_all[c * C + C - 1:c * C + C], (C, LANES))
                              for c in range(G // C)], axis=0)
    gct_ref[...] = gc_all.T

    ri = lax.broadcasted_iota(jnp.int32, (G, G), 0)
    ci = lax.broadcasted_iota(jnp.int32, (G, G), 1)
    same = (ri >> lc) == (ci >> lc)
    tri_incl = same & (ri >= ci)
    tri_strict = same & (ri > ci)
    eye = jnp.where(ri == ci, 1.0, 0.0)
    ct = lax.broadcasted_iota(jnp.int32, (D, G), 1) >> lc

    def pair_mask(s):
        ls = s.bit_length() - 1
        return ((ri >> (ls + 1)) == (ci >> (ls + 1))) & (((ri >> ls) & 1) == 1) & (((ci >> ls) & 1) == 0)

    pair_masks = {}
    s = 1
    while s < C:
        pair_masks[s] = pair_mask(s)
        s *= 2

    heads = range(NH)
    sls = [slice(hl * D, (hl + 1) * D) for hl in heads]
    hidx = [hg * NH + hl for hl in heads]

    def col_of(x, lane_idx):
        return jnp.sum(jnp.where(lane == lane_idx, x, 0.0), axis=1, keepdims=True)

    q = [l2n(qf[:, sl]) * (D ** -0.5) for sl in sls]
    k = [l2n(kf[:, sl]) for sl in sls]
    beta_col = [col_of(beta_all, h) for h in hidx]
    gc_col = [col_of(gc_all, h + n_heads) for h in hidx]
    gl_col = [col_of(gl_all, h + n_heads) for h in hidx]
    gc_row = [gct_ref[pl.ds(h + n_heads, 1), :] for h in hidx]
    decay = [jnp.where(tri_incl, jnp.exp(jnp.where(tri_incl, gc_col[a] - gc_row[a], 0.0)), 0.0) for a in heads]
    kb = [k[a] * beta_col[a] for a in heads]
    vb = [vf[:, sls[a]] * beta_col[a] for a in heads]
    k16 = [x.astype(BF16) for x in k]
    kk = [lax.dot_general(kb[a].astype(BF16), k16[a], NT_DIMS, preferred_element_type=F32) for a in heads]
    qk = [lax.dot_general(q[a].astype(BF16), k16[a], NT_DIMS, preferred_element_type=F32) for a in heads]
    lower = [jnp.where(tri_strict, kk[a] * decay[a], 0.0) for a in heads]
    attn = [jnp.where(tri_incl, qk[a] * decay[a], 0.0).astype(BF16) for a in heads]

    tinv = [eye - jnp.where(pair_masks[1], lower[a], 0.0) for a in heads]
    s = 2
    while s < C:
        t16 = [x.astype(BF16) for x in tinv]
        md = [jnp.dot(jnp.where(pair_masks[s], lower[a], 0.0).astype(BF16), t16[a], preferred_element_type=F32)
              for a in heads]
        tinv = [tinv[a] - jnp.dot(t16[a], md[a].astype(BF16), preferred_element_type=F32) for a in heads]
        s *= 2

    egc = [jnp.exp(x) for x in gc_col]
    uw = [jnp.dot(tinv[a].astype(BF16), jnp.concatenate([vb[a], kb[a] * egc[a]], axis=1).astype(BF16),
                  preferred_element_type=F32) for a in heads]
    q_dec = [q[a] * egc[a] for a in heads]
    k_dec_t = [(k[a] * jnp.exp(gl_col[a] - gc_col[a])).T for a in heads]
    egl = [jnp.exp(x) for x in gl_col]

    vnew_ref[...] = jnp.zeros_like(vnew_ref)
    outs = [[] for _ in heads]
    for c in range(G // C):
        r0 = c * C
        st = [state_ref[a] for a in heads]
        ws_qs = [jnp.dot(jnp.concatenate([uw[a][r0:r0 + C, D:], q_dec[a][r0:r0 + C]], axis=0).astype(BF16),
                         st[a].astype(BF16), preferred_element_type=F32) for a in heads]
        for a in heads:
            vnew_ref[a, r0:r0 + C, :] = (uw[a][r0:r0 + C, :D] - ws_qs[a][:C]).astype(BF16)
        vn = [vnew_ref[a] for a in heads]
        for a in heads:
            outs[a].append(ws_qs[a][C:] + jnp.dot(attn[a][r0:r0 + C], vn[a], preferred_element_type=F32))
        for a in heads:
            kt_c = jnp.where(ct == c, k_dec_t[a], 0.0).astype(BF16)
            e_c = jnp.concatenate([egl[a][r0:r0 + C]] * (D // C), axis=0)
            state_ref[a] = st[a] * e_c + jnp.dot(kt_c, vn[a], preferred_element_type=F32)

    for a in heads:
        o = jnp.concatenate(outs[a], axis=0)
        o = o * lax.rsqrt(jnp.mean(o * o, axis=-1, keepdims=True) + NORM_EPS) * onw_ref[...]
        z = z_ref[0, :, sls[a]].astype(F32)
        o_ref[0, :, sls[a]] = (o * _silu(z)).astype(o_ref.dtype)


def _deltanet(qkv3, z3, small3, a_log_row, dt_row, onorm_w, n_heads, col_q, col_k, col_v, col_z, heads_per_step=8):
    b, s, _ = qkv3.shape
    G, D, NH = DN_GROUP, LANES, heads_per_step
    assert n_heads % NH == 0 and all(c % NH == 0 for c in (col_q, col_k, col_v, col_z))
    blk = lambda off: pl.BlockSpec((1, G, NH * D), lambda bi, hi, gi: (bi, gi, off // NH + hi))
    row_spec = pl.BlockSpec((1, LANES), lambda bi, hi, gi: (0, 0))
    return pl.pallas_call(
        functools.partial(_deltanet_kernel, n_heads=n_heads, heads_per_step=NH),
        grid=(b, n_heads // NH, s // G),
        in_specs=[blk(col_q), blk(col_k), blk(col_v), blk(col_z),
                  pl.BlockSpec((1, G, LANES), lambda bi, hi, gi: (bi, gi, 0)),
                  row_spec, row_spec, row_spec],
        out_specs=pl.BlockSpec((1, G, NH * D), lambda bi, hi, gi: (bi, gi, hi)),
        out_shape=jax.ShapeDtypeStruct((b, s, n_heads * D), BF16),
        scratch_shapes=[pltpu.VMEM((NH, D, D), F32),
                        pltpu.VMEM((NH, G, D), BF16), pltpu.VMEM((LANES, G), F32)],
        compiler_params=_cparams(("parallel", "parallel", "arbitrary")),
        name="gated_deltanet",
    )(qkv3, qkv3, qkv3, z3, small3, a_log_row, dt_row, onorm_w)


def _t5_bucket_f32(dist):
    dist = jnp.maximum(dist, 0)
    max_exact = REL_BUCKETS // 2
    d = jnp.maximum(dist, 1).astype(F32)
    large = max_exact + (jnp.log(d / max_exact) / math.log(REL_MAX_DIST / max_exact)
                         * (REL_BUCKETS - max_exact)).astype(jnp.int32)
    large = jnp.minimum(large, REL_BUCKETS - 1)
    return jnp.where(dist < max_exact, dist, large)


def _n_near_tiles():
    return -(-(REL_MAX_DIST + MOBA_BLOCK - 1) // MOBA_BLOCK)


def _moba_kernel(rb_ref, q_ref, k_ref, v_ref, z_ref, o_ref, kmean_ref, bias_ref, etab_ref, xt_ref, qa_ref,
                 s0_ref, s1_ref, wide_ref, *, n_near, nbp, group, qblocks):
    hh = pl.program_id(0)
    bb = pl.program_id(1)
    BLK, D, U, QB = MOBA_BLOCK, LANES, group, qblocks
    R = QB * BLK
    nb = k_ref.shape[1] // BLK
    n_steps = nb // QB
    col_shift = nbp

    @pl.when(bb == 0)
    def _():
        qr = lax.broadcasted_iota(jnp.int32, (BLK, BLK), 0)
        kc = lax.broadcasted_iota(jnp.int32, (BLK, BLK), 1)
        for t in range(n_near):
            dist = t * BLK + qr - kc
            bucket = _t5_bucket_f32(dist)
            val = jnp.full((BLK, BLK), rb_ref[0, hh], F32)
            for bk in range(1, REL_BUCKETS):
                val = jnp.where(bucket == bk, rb_ref[bk, hh], val)
            if t == 0:
                val = jnp.where(dist >= 0, val, NEG_INF)
            bias_ref[t] = val
        bias_ref[n_near] = jnp.full((BLK, BLK), rb_ref[REL_BUCKETS - 1, hh], F32)
        lane = lax.broadcasted_iota(jnp.int32, (16, D), 1)
        for j in range(nb):
            etab_ref[j] = jnp.where((lane == j) | (lane == col_shift), 1.0, 0.0).astype(BF16)

    ones_dd = jnp.ones((D, D), BF16)

    def norms(it, carry):
        k2, q2 = carry
        for u in range(U):
            j = it * U + u
            rows = pl.ds(pl.multiple_of(j * BLK, BLK), BLK)
            kj = k_ref[0, rows, :].astype(F32)
            qj = q_ref[0, rows, :].astype(F32)
            kmean_ref[pl.ds(j, 1), :] = jnp.mean(kj, axis=0, keepdims=True)
            k2 = jnp.maximum(k2, jnp.dot((kj * kj).astype(BF16), ones_dd, preferred_element_type=F32))
            q2 = jnp.maximum(q2, jnp.dot((qj * qj).astype(BF16), ones_dd, preferred_element_type=F32))
        return k2, q2
    k2, q2 = lax.fori_loop(0, nb // U, norms, (jnp.zeros((BLK, D), F32), jnp.zeros((BLK, D), F32)))
    rb_abs = jnp.abs(rb_ref[0, hh])
    for bk in range(1, REL_BUCKETS):
        rb_abs = jnp.maximum(rb_abs, jnp.abs(rb_ref[bk, hh]))
    bound = jnp.sqrt(jnp.max(q2) * jnp.max(k2)) * 1.02 + rb_abs + 1e-3
    wide_ref[0] = jnp.where(bound > MOBA_SHIFT_SLACK, 1, 0).astype(jnp.int32)

    kmean16 = kmean_ref[...].astype(BF16)
    jrow = lax.broadcasted_iota(jnp.int32, (nb, BLK), 0)
    jrow_f = jrow.astype(F32)

    def visibility(it, carry):
        for u in range(U):
            ib = it * U + u
            rows = pl.ds(pl.multiple_of(ib * BLK, BLK), BLK)
            gate = lax.dot_general(kmean16, q_ref[0, rows, :], NT_DIMS, preferred_element_type=F32)
            past = jrow < ib
            sc = jnp.where(past, gate, NEG_INF)
            visible = jrow == ib
            for _ in range(MOBA_TOPK):
                top = jnp.max(sc, axis=0, keepdims=True)
                first = jnp.min(jnp.where(sc == top, jrow_f, float(nb)), axis=0, keepdims=True)
                pick = jrow_f == first
                visible = visible | (pick & past)
                sc = jnp.where(pick, jnp.finfo(F32).min, sc)
            cols = jnp.concatenate([jnp.where(visible, 0.0, NEG_INF), jnp.zeros((D - nb, BLK), F32)], axis=0)
            xt_ref[rows, :] = cols.T.astype(BF16)
        return carry
    lax.fori_loop(0, nb // U, visibility, 0)

    def set_query_operand(s, slot):
        rows = pl.ds(pl.multiple_of(s * R, R), R)
        qa_ref[slot, :, 0:D] = q_ref[0, rows, :]
        qa_ref[slot, :, D:2 * D] = xt_ref[rows, :]

    def pair_scores(s, g, slot):
        j0 = g * U
        row0 = pl.multiple_of(j0 * BLK, U * BLK)
        e_rows = []
        for u in range(U):
            e_rows += [etab_ref[j0 + u]] * (BLK // 16)
        ka = jnp.concatenate([k_ref[0, pl.ds(row0, U * BLK), :], jnp.concatenate(e_rows, axis=0)], axis=1)
        sc = lax.dot_general(qa_ref[slot], ka, NT_DIMS, preferred_element_type=F32)
        bias = jnp.concatenate(
            [jnp.concatenate([bias_ref[jnp.clip(s * QB + a - (j0 + u), 0, n_near)] for u in range(U)], axis=1)
             for a in range(QB)], axis=0)
        return sc + bias

    def lane_chunks(x):
        return [x[:, c * D:(c + 1) * D] for c in range(x.shape[1] // D)]

    @pl.when(wide_ref[0] == 1)
    def _():
        def step_max(s, carry):
            set_query_operand(s, 0)

            def max_step(g, mm):
                for c in lane_chunks(pair_scores(s, g, 0)):
                    mm = jnp.maximum(mm, c)
                return mm
            mm = lax.fori_loop(0, s + 1, max_step, jnp.full((R, D), jnp.finfo(F32).min, F32))
            m = jnp.max(mm, axis=1, keepdims=True)
            rows = pl.ds(pl.multiple_of(s * R, R), R)
            lane_q = lax.broadcasted_iota(jnp.int32, (R, D), 1)
            xt_ref[rows, :] = jnp.where(lane_q == col_shift, -m, xt_ref[rows, :].astype(F32)).astype(BF16)
            return carry
        lax.fori_loop(0, n_steps, step_max, 0)

    def consume(g, sc, acc, lsum):
        p = jnp.exp(sc)
        for c in lane_chunks(p):
            lsum = lsum + c
        vg = v_ref[0, pl.ds(pl.multiple_of(g * (U * BLK), U * BLK), U * BLK), :]
        return acc + jnp.dot(p.astype(BF16), vg, preferred_element_type=F32), lsum

    def finish_step(s, g, acc, lsum):
        @pl.when(g == s)
        def _():
            rows = pl.ds(pl.multiple_of(s * R, R), R)
            out = acc / jnp.sum(lsum, axis=1, keepdims=True)
            z = z_ref[0, rows, :].astype(F32)
            o_ref[0, rows, :] = (out * _silu(z)).astype(o_ref.dtype)
        keep = jnp.where(g == s, 0.0, 1.0).astype(F32)
        return acc * keep, lsum * keep

    def stage(cur_ref, nxt_ref, carry):
        s, g, acc, lsum = carry
        wrap = g == s
        s2 = jnp.where(wrap, s + 1, s)
        g2 = jnp.where(wrap, 0, g + 1)

        @pl.when(wrap)
        def _():
            set_query_operand(s2, s2 & 1)

        nxt_ref[...] = pair_scores(s2, g2, s2 & 1)
        acc, lsum = consume(g, cur_ref[...], acc, lsum)
        acc, lsum = finish_step(s, g, acc, lsum)
        return s2, g2, acc, lsum

    def two_stages(t, carry):
        return stage(s1_ref, s0_ref, stage(s0_ref, s1_ref, carry))

    n_pairs = n_steps * (n_steps + 1) // 2
    set_query_operand(0, 0)
    s0_ref[...] = pair_scores(0, 0, 0)
    zero = jnp.zeros((R, D), F32)
    carry = (jnp.int32(0), jnp.int32(0), zero, zero)
    carry = lax.fori_loop(0, (n_pairs - 1) // 2, two_stages, carry)
    last_ref = s0_ref
    if (n_pairs - 1) % 2:
        carry = stage(s0_ref, s1_ref, carry)
        last_ref = s1_ref
    s, g, acc, lsum = carry
    acc, lsum = consume(g, last_ref[...], acc, lsum)
    finish_step(s, g, acc, lsum)


def _moba(rel_bias, p3, n_heads, col_q, col_k, col_v, col_z, group=4, qblocks=4):
    b, s, _ = p3.shape
    BLK, D = MOBA_BLOCK, LANES
    nb = s // BLK
    nbp = -(-nb // 16) * 16
    assert nbp + 16 <= D, "block-visibility columns must fit in the spare contraction columns"
    assert group == qblocks and nb % group == 0, "step s visits key groups 0..s"
    n_near = _n_near_tiles()
    rows = qblocks * BLK
    seq = lambda off: pl.BlockSpec((1, s, D), lambda hi, bi: (bi, 0, off + hi))
    return pl.pallas_call(
        functools.partial(_moba_kernel, n_near=n_near, nbp=nbp, group=group, qblocks=qblocks),
        grid=(n_heads, b),
        in_specs=[pl.BlockSpec(memory_space=pltpu.SMEM), seq(col_q), seq(col_k), seq(col_v), seq(col_z)],
        out_specs=seq(0),
        out_shape=jax.ShapeDtypeStruct((b, s, n_heads * D), BF16),
        scratch_shapes=[pltpu.VMEM((nb, D), F32), pltpu.VMEM((n_near + 1, BLK, BLK), F32),
                        pltpu.VMEM((nb, 16, D), BF16), pltpu.VMEM((s, D), BF16), pltpu.VMEM((2, rows, 2 * D), BF16),
                        pltpu.VMEM((rows, group * BLK), F32), pltpu.VMEM((rows, group * BLK), F32),
                        pltpu.SMEM((1,), jnp.int32)],
        compiler_params=_cparams(("arbitrary", "arbitrary")),
        name="moba_attention",
    )(rel_bias, p3, p3, p3, p3)


def _merge_kernel(ya_ref, yb_ref, g0_ref, g1_ref, x_ref, w0_ref, w1_ref, wo_ref, b0_ref, b1_ref, nw_ref, o_ref):
    pa = jnp.dot(ya_ref[...], w0_ref[...], preferred_element_type=F32)
    pb = jnp.dot(yb_ref[...], w1_ref[...], preferred_element_type=F32)
    g0 = _sigmoid(g0_ref[...].astype(F32) + b0_ref[...])
    g1 = _sigmoid(g1_ref[...].astype(F32) + b1_ref[...])
    merged = (g0 * pa + g1 * pb).astype(BF16)
    out = jnp.dot(merged, wo_ref[...], preferred_element_type=F32)
    y = out * lax.rsqrt(jnp.mean(out * out, axis=-1, keepdims=True) + NORM_EPS) * nw_ref[...]
    o_ref[...] = x_ref[...] + y


def _merge(ya, yb, p2, x2d, w0, w1, wo, b_gate, norm_w, col_g0, col_g1, tm=512):
    t, d = x2d.shape
    row = lambda c: pl.BlockSpec((tm, d), lambda i: (i, c))
    full = pl.BlockSpec((d, d), lambda i: (0, 0))
    vec = lambda c: pl.BlockSpec((1, d), lambda i: (0, c))
    return pl.pallas_call(
        _merge_kernel,
        grid=(t // tm,),
        in_specs=[row(0), row(0), row(col_g0), row(col_g1), row(0), full, full, full, vec(0), vec(1), vec(0)],
        out_specs=row(0),
        out_shape=jax.ShapeDtypeStruct((t, d), F32),
        compiler_params=_cparams(("parallel",)),
        name="merge_out",
    )(ya, yb, p2, p2, x2d, w0, w1, wo, b_gate.reshape(1, -1), b_gate.reshape(1, -1), norm_w.reshape(1, d))


def kernel(x, norm_pre_w, w_in, b_gate, conv_w, dn_a_log, dn_dt_bias, dn_onorm_w, rel_bias, w_branch, w_out,
           norm_post_w):
    b, s, d = x.shape
    n_heads = dn_a_log.shape[0]
    hd = dn_onorm_w.shape[0]
    width = n_heads * hd
    assert hd == LANES and rel_bias.shape[1] == n_heads and width == d
    assert s % MOBA_BLOCK == 0 and s % DN_GROUP == 0 and 2 * n_heads <= LANES
    t = b * s

    c_za = 3 * width
    c_beta = 4 * width
    c_qb = c_beta + 2 * n_heads
    assert conv_w.shape[1] == c_za
    w16 = w_in.astype(BF16)
    w_qkv_a = w16[:, :c_za]
    w_rest = jnp.concatenate([w16[:, c_za:c_beta], w16[:, c_qb:]], axis=1)
    w_small = jnp.pad(w16[:, c_beta:c_qb], ((0, 0), (0, LANES - 2 * n_heads)))
    nblk = width // LANES
    col = lambda k: k * nblk

    x2d = x.reshape(t, d)
    norm_w = norm_pre_w.astype(F32)
    col_scale = jnp.concatenate([jnp.ones((1, width), F32), jnp.full((1, width), hd ** -0.5, F32),
                                 jnp.ones((1, w_rest.shape[1] - 2 * width), F32)], axis=1)
    qkv_a, rest, small = _in_proj(x2d, norm_w, w_qkv_a, w_rest, col_scale, w_small, conv_w.astype(F32), s)
    qkv_a3 = qkv_a.reshape(b, s, -1)
    rest3 = rest.reshape(b, s, -1)

    pad = (0, LANES - 2 * n_heads)
    a_log_row = jnp.pad(jnp.concatenate([jnp.zeros_like(dn_a_log), dn_a_log]), pad).reshape(1, LANES).astype(F32)
    dt_row = jnp.pad(jnp.concatenate([jnp.zeros_like(dn_dt_bias), dn_dt_bias]), pad).reshape(1, LANES).astype(F32)
    ya = _deltanet(qkv_a3, rest3, small.reshape(b, s, LANES), a_log_row, dt_row,
                   dn_onorm_w.reshape(1, hd).astype(F32), n_heads, col(0), col(1), col(2), col(0))
    yb = _moba(rel_bias.astype(F32), rest3, n_heads, col(1), col(2), col(3), col(4))

    out = _merge(ya.reshape(t, d), yb.reshape(t, d), rest, x2d,
                 w_branch[0].astype(BF16), w_branch[1].astype(BF16), w_out.astype(BF16),
                 b_gate.astype(F32), norm_post_w.astype(F32), 5, 6)
    return out.reshape(b, s, d)
```

```python
import functools
import math

import jax
import jax.numpy as jnp
from jax import lax
from jax.experimental import pallas as pl
from jax.experimental.pallas import tpu as pltpu

F32 = jnp.float32
BF16 = jnp.bfloat16

LANES = 128
DN_CHUNK = 64
DN_GROUP = 128
DN_CONV_HALO = 8
MOBA_BLOCK = 256
MOBA_TOPK = 3
MOBA_SHIFT_SLACK = 40.0
REL_BUCKETS = 32
REL_MAX_DIST = 2048
NORM_EPS = 1e-6
NEG_INF = -1e30
VMEM_LIMIT = 56 * 1024 * 1024
NT_DIMS = (((1,), (1,)), ((), ()))


def _cparams(sem):
    return pltpu.CompilerParams(dimension_semantics=sem, vmem_limit_bytes=VMEM_LIMIT)


def _sigmoid(x):
    return 0.5 * jnp.tanh(0.5 * x) + 0.5


def _silu(x):
    h = 0.5 * x
    return h * jnp.tanh(h) + h


def _normalise(x, w):
    y = x * lax.rsqrt(jnp.mean(x * x, axis=-1, keepdims=True) + NORM_EPS)
    return (y * w).astype(BF16)


def _in_proj_kernel(x_ref, xh_ref, nw_ref, wc_ref, wr_ref, cs_ref, ws_ref, cw_ref, oc_ref, or_ref, sm_ref,
                    h_ref, hh_ref, *, tiles_per_seq, row_chunks):
    i = pl.program_id(0)
    tm = oc_ref.shape[0]
    halo = hh_ref.shape[0]

    @pl.when(pl.program_id(1) == 0)
    def _():
        h = _normalise(x_ref[...], nw_ref[...])
        h_ref[...] = h
        sm_ref[...] = jnp.dot(h, ws_ref[...], preferred_element_type=F32)
        hh = _normalise(xh_ref[...], nw_ref[...])
        hh_ref[...] = jnp.where(i % tiles_per_seq == 0, jnp.zeros_like(hh), hh)

    wc = wc_ref[...]
    wr = wr_ref[...]
    cw = cw_ref[...]
    kw = cw.shape[0]
    base = halo - (kw - 1)
    rm = tm // row_chunks
    tail = jnp.dot(hh_ref[...], wc, preferred_element_type=F32)
    for r in range(row_chunks):
        rows = slice(r * rm, (r + 1) * rm)
        hr = h_ref[rows, :]
        acc_c = jnp.dot(hr, wc, preferred_element_type=F32)
        acc_r = jnp.dot(hr, wr, preferred_element_type=F32)
        or_ref[rows, :] = (acc_r * cs_ref[...]).astype(or_ref.dtype)
        ext = jnp.concatenate([tail, acc_c], axis=0)
        y = ext[base:base + rm] * cw[0:1]
        for kk in range(1, kw):
            y = y + ext[base + kk:base + kk + rm] * cw[kk:kk + 1]
        oc_ref[rows, :] = _silu(y).astype(oc_ref.dtype)
        tail = acc_c[rm - halo:rm]


def _in_proj(x2d, norm_w, w_conv, w_rest, col_scale, w_small, conv_w, seq_len, tm=1024, steps=4, halo=16,
             row_chunks=4):
    t, d = x2d.shape
    nc, nr, ns = w_conv.shape[1], w_rest.shape[1], w_small.shape[1]
    kw = conv_w.shape[0]
    assert nc % (steps * LANES) == 0 and nr % (steps * LANES) == 0
    assert seq_len % tm == 0 and tm % halo == 0 and kw - 1 <= halo and conv_w.shape[1] == nc
    tc, tr = nc // steps, nr // steps
    row = lambda w: pl.BlockSpec((tm, w), lambda i, j: (i, j))
    col = lambda r, w: pl.BlockSpec((r, w), lambda i, j: (0, j))
    fixed = lambda r, w: pl.BlockSpec((r, w), lambda i, j: (0, 0))
    return pl.pallas_call(
        functools.partial(_in_proj_kernel, tiles_per_seq=seq_len // tm, row_chunks=row_chunks),
        grid=(t // tm, steps),
        in_specs=[pl.BlockSpec((tm, d), lambda i, j: (i, 0)),
                  pl.BlockSpec((halo, d), lambda i, j: (jnp.maximum(i * (tm // halo) - 1, 0), 0)),
                  fixed(1, d), col(d, tc), col(d, tr), col(1, tr), fixed(d, ns), col(kw, tc)],
        out_specs=[row(tc), row(tr), pl.BlockSpec((tm, ns), lambda i, j: (i, 0))],
        out_shape=[jax.ShapeDtypeStruct((t, nc), BF16), jax.ShapeDtypeStruct((t, nr), BF16),
                   jax.ShapeDtypeStruct((t, ns), F32)],
        scratch_shapes=[pltpu.VMEM((tm, d), BF16), pltpu.VMEM((halo, d), BF16)],
        compiler_params=_cparams(("parallel", "arbitrary")),
        name="in_proj",
    )(x2d, x2d, norm_w.reshape(1, d), w_conv, w_rest, col_scale, w_small, conv_w)


def _deltanet_kernel(q_ref, k_ref, v_ref, z_ref, sm_ref, alog_ref, dtb_ref, onw_ref,
                     o_ref, state_ref, vnew_ref, gct_ref, *, n_heads, heads_per_step):
    hg = pl.program_id(1)
    g = pl.program_id(2)
    G, C, D, NH = DN_GROUP, DN_CHUNK, LANES, heads_per_step
    lc = C.bit_length() - 1

    @pl.when(g == 0)
    def _():
        state_ref[...] = jnp.zeros_like(state_ref)

    def l2n(x):
        return x * lax.rsqrt(jnp.sum(x * x, axis=-1, keepdims=True) + NORM_EPS)

    NB = q_ref.shape[0]
    qf = [q_ref[bi].astype(F32) for bi in range(NB)]
    kf = [k_ref[bi].astype(F32) for bi in range(NB)]
    vf = [v_ref[bi].astype(F32) for bi in range(NB)]

    lane = lax.broadcasted_iota(jnp.int32, (G, LANES), 1)
    row = lax.broadcasted_iota(jnp.int32, (G, LANES), 0)
    pos = row & (C - 1)
    beta_all, gc_all, gl_all = [], [], []
    for bi in range(NB):
        cs = sm_ref[bi]
        beta_all.append(_sigmoid(cs))
        xg = cs + dtb_ref[...]
        softplus = jnp.maximum(xg, 0.0) + jnp.log1p(jnp.exp(-jnp.abs(xg)))
        gc = -jnp.exp(alog_ref[...]) * softplus
        shift = 1
        while shift < C:
            gc = gc + jnp.where(pos >= shift, pltpu.roll(gc, shift, 0), 0.0)
            shift *= 2
        gc_all.append(gc)
        gl_all.append(jnp.concatenate([jnp.broadcast_to(gc[c * C + C - 1:c * C + C], (C, LANES))
                                       for c in range(G // C)], axis=0))
        gct_ref[bi] = gc.T

    ri = lax.broadcasted_iota(jnp.int32, (G, G), 0)
    ci = lax.broadcasted_iota(jnp.int32, (G, G), 1)
    same = (ri >> lc) == (ci >> lc)
    tri_incl = same & (ri >= ci)
    tri_strict = same & (ri > ci)
    eye = jnp.where(ri == ci, 1.0, 0.0)
    ct = lax.broadcasted_iota(jnp.int32, (D, G), 1) >> lc

    def pair_mask(s):
        ls = s.bit_length() - 1
        return ((ri >> (ls + 1)) == (ci >> (ls + 1))) & (((ri >> ls) & 1) == 1) & (((ci >> ls) & 1) == 0)

    pair_masks = {}
    s = 1
    while s < C:
        pair_masks[s] = pair_mask(s)
        s *= 2

    heads = range(NB * NH)
    bis = [a // NH for a in heads]
    sls = [slice((a % NH) * D, (a % NH + 1) * D) for a in heads]
    hidx = [hg * NH + a % NH for a in heads]

    def col_of(x, lane_idx):
        return jnp.sum(jnp.where(lane == lane_idx, x, 0.0), axis=1, keepdims=True)

    q = [l2n(qf[bis[a]][:, sls[a]]) * (D ** -0.5) for a in heads]
    k = [l2n(kf[bis[a]][:, sls[a]]) for a in heads]
    beta_col = [col_of(beta_all[bis[a]], hidx[a]) for a in heads]
    gc_col = [col_of(gc_all[bis[a]], hidx[a] + n_heads) for a in heads]
    gl_col = [col_of(gl_all[bis[a]], hidx[a] + n_heads) for a in heads]
    gc_row = [gct_ref[bis[a], pl.ds(hidx[a] + n_heads, 1), :] for a in heads]
    decay = [jnp.where(tri_incl, jnp.exp(jnp.where(tri_incl, gc_col[a] - gc_row[a], 0.0)), 0.0) for a in heads]
    kb = [k[a] * beta_col[a] for a in heads]
    vb = [vf[bis[a]][:, sls[a]] * beta_col[a] for a in heads]
    k16 = [x.astype(BF16) for x in k]
    kk = [lax.dot_general(kb[a].astype(BF16), k16[a], NT_DIMS, preferred_element_type=F32) for a in heads]
    qk = [lax.dot_general(q[a].astype(BF16), k16[a], NT_DIMS, preferred_element_type=F32) for a in heads]
    lower = [jnp.where(tri_strict, kk[a] * decay[a], 0.0) for a in heads]
    attn = [jnp.where(tri_incl, qk[a] * decay[a], 0.0).astype(BF16) for a in heads]

    tinv = [eye - jnp.where(pair_masks[1], lower[a], 0.0) for a in heads]
    s = 2
    while s < C:
        t16 = [x.astype(BF16) for x in tinv]
        md = [jnp.dot(jnp.where(pair_masks[s], lower[a], 0.0).astype(BF16), t16[a], preferred_element_type=F32)
              for a in heads]
        tinv = [tinv[a] - jnp.dot(t16[a], md[a].astype(BF16), preferred_element_type=F32) for a in heads]
        s *= 2

    egc = [jnp.exp(x) for x in gc_col]
    uw = [jnp.dot(tinv[a].astype(BF16), jnp.concatenate([vb[a], kb[a] * egc[a]], axis=1).astype(BF16),
                  preferred_element_type=F32) for a in heads]
    q_dec = [q[a] * egc[a] for a in heads]
    k_dec_t = [(k[a] * jnp.exp(gl_col[a] - gc_col[a])).T for a in heads]
    egl = [jnp.exp(x) for x in gl_col]

    vnew_ref[...] = jnp.zeros_like(vnew_ref)
    outs = [[] for _ in heads]
    for c in range(G // C):
        r0 = c * C
        st = [state_ref[a] for a in heads]
        ws_qs = [jnp.dot(jnp.concatenate([uw[a][r0:r0 + C, D:], q_dec[a][r0:r0 + C]], axis=0).astype(BF16),
                         st[a].astype(BF16), preferred_element_type=F32) for a in heads]
        for a in heads:
            vnew_ref[a, r0:r0 + C, :] = (uw[a][r0:r0 + C, :D] - ws_qs[a][:C]).astype(BF16)
        vn = [vnew_ref[a] for a in heads]
        for a in heads:
            outs[a].append(ws_qs[a][C:] + jnp.dot(attn[a][r0:r0 + C], vn[a], preferred_element_type=F32))
        for a in heads:
            kt_c = jnp.where(ct == c, k_dec_t[a], 0.0).astype(BF16)
            e_c = jnp.concatenate([egl[a][r0:r0 + C]] * (D // C), axis=0)
            state_ref[a] = st[a] * e_c + jnp.dot(kt_c, vn[a], preferred_element_type=F32)

    for a in heads:
        o = jnp.concatenate(outs[a], axis=0)
        o = o * lax.rsqrt(jnp.mean(o * o, axis=-1, keepdims=True) + NORM_EPS) * onw_ref[...]
        z = z_ref[bis[a], :, sls[a]].astype(F32)
        o_ref[bis[a], :, sls[a]] = (o * _silu(z)).astype(o_ref.dtype)


def _deltanet(qkv3, z3, small3, a_log_row, dt_row, onorm_w, n_heads, col_q, col_k, col_v, col_z, heads_per_step=8,
              batch_rows_per_step=2):
    b, s, _ = qkv3.shape
    G, D, NH = DN_GROUP, LANES, heads_per_step
    NB = batch_rows_per_step if b % batch_rows_per_step == 0 else 1
    assert n_heads % NH == 0 and all(c % NH == 0 for c in (col_q, col_k, col_v, col_z))
    blk = lambda off: pl.BlockSpec((NB, G, NH * D), lambda bi, hi, gi: (bi, gi, off // NH + hi))
    row_spec = pl.BlockSpec((1, LANES), lambda bi, hi, gi: (0, 0))
    return pl.pallas_call(
        functools.partial(_deltanet_kernel, n_heads=n_heads, heads_per_step=NH),
        grid=(b // NB, n_heads // NH, s // G),
        in_specs=[blk(col_q), blk(col_k), blk(col_v), blk(col_z),
                  pl.BlockSpec((NB, G, LANES), lambda bi, hi, gi: (bi, gi, 0)),
                  row_spec, row_spec, row_spec],
        out_specs=pl.BlockSpec((NB, G, NH * D), lambda bi, hi, gi: (bi, gi, hi)),
        out_shape=jax.ShapeDtypeStruct((b, s, n_heads * D), BF16),
        scratch_shapes=[pltpu.VMEM((NB * NH, D, D), F32),
                        pltpu.VMEM((NB * NH, G, D), BF16), pltpu.VMEM((NB, LANES, G), F32)],
        compiler_params=_cparams(("parallel", "parallel", "arbitrary")),
        name="gated_deltanet",
    )(qkv3, qkv3, qkv3, z3, small3, a_log_row, dt_row, onorm_w)


def _t5_bucket_f32(dist):
    dist = jnp.maximum(dist, 0)
    max_exact = REL_BUCKETS // 2
    d = jnp.maximum(dist, 1).astype(F32)
    large = max_exact + (jnp.log(d / max_exact) / math.log(REL_MAX_DIST / max_exact)
                         * (REL_BUCKETS - max_exact)).astype(jnp.int32)
    large = jnp.minimum(large, REL_BUCKETS - 1)
    return jnp.where(dist < max_exact, dist, large)


def _n_near_tiles():
    return -(-(REL_MAX_DIST + MOBA_BLOCK - 1) // MOBA_BLOCK)


def _moba_kernel(rb_ref, q_ref, k_ref, v_ref, z_ref, o_ref, kmean_ref, bias_ref, etab_ref, xt_ref, qa_ref,
                 s0_ref, s1_ref, wide_ref, *, n_near, nbp, group, qblocks):
    hh = pl.program_id(0)
    bb = pl.program_id(1)
    BLK, D, U, QB = MOBA_BLOCK, LANES, group, qblocks
    R = QB * BLK
    nb = k_ref.shape[1] // BLK
    n_steps = nb // QB
    col_shift = nbp

    @pl.when(bb == 0)
    def _():
        qr = lax.broadcasted_iota(jnp.int32, (BLK, BLK), 0)
        kc = lax.broadcasted_iota(jnp.int32, (BLK, BLK), 1)
        for t in range(n_near):
            dist = t * BLK + qr - kc
            bucket = _t5_bucket_f32(dist)
            val = jnp.full((BLK, BLK), rb_ref[0, hh], F32)
            for bk in range(1, REL_BUCKETS):
                val = jnp.where(bucket == bk, rb_ref[bk, hh], val)
            if t == 0:
                val = jnp.where(dist >= 0, val, NEG_INF)
            bias_ref[t] = val
        bias_ref[n_near] = jnp.full((BLK, BLK), rb_ref[REL_BUCKETS - 1, hh], F32)
        lane = lax.broadcasted_iota(jnp.int32, (16, D), 1)
        for j in range(nb):
            etab_ref[j] = jnp.where((lane == j) | (lane == col_shift), 1.0, 0.0).astype(BF16)

    ones_dd = jnp.ones((D, D), BF16)

    def norms(it, carry):
        k2, q2 = carry
        for u in range(U):
            j = it * U + u
            rows = pl.ds(pl.multiple_of(j * BLK, BLK), BLK)
            kj = k_ref[0, rows, :].astype(F32)
            qj = q_ref[0, rows, :].astype(F32)
            kmean_ref[pl.ds(j, 1), :] = jnp.mean(kj, axis=0, keepdims=True)
            k2 = jnp.maximum(k2, jnp.dot((kj * kj).astype(BF16), ones_dd, preferred_element_type=F32))
            q2 = jnp.maximum(q2, jnp.dot((qj * qj).astype(BF16), ones_dd, preferred_element_type=F32))
        return k2, q2
    k2, q2 = lax.fori_loop(0, nb // U, norms, (jnp.zeros((BLK, D), F32), jnp.zeros((BLK, D), F32)))
    rb_abs = jnp.abs(rb_ref[0, hh])
    for bk in range(1, REL_BUCKETS):
        rb_abs = jnp.maximum(rb_abs, jnp.abs(rb_ref[bk, hh]))
    bound = jnp.sqrt(jnp.max(q2) * jnp.max(k2)) * 1.02 + rb_abs + 1e-3
    wide_ref[0] = jnp.where(bound > MOBA_SHIFT_SLACK, 1, 0).astype(jnp.int32)

    kmean16 = kmean_ref[...].astype(BF16)
    jrow = lax.broadcasted_iota(jnp.int32, (nb, BLK), 0)
    jrow_f = jrow.astype(F32)

    def visibility(it, carry):
        for u in range(U):
            ib = it * U + u
            rows = pl.ds(pl.multiple_of(ib * BLK, BLK), BLK)
            gate = lax.dot_general(kmean16, q_ref[0, rows, :], NT_DIMS, preferred_element_type=F32)
            past = jrow < ib
            sc = jnp.where(past, gate, NEG_INF)
            visible = jrow == ib
            for _ in range(MOBA_TOPK):
                top = jnp.max(sc, axis=0, keepdims=True)
                first = jnp.min(jnp.where(sc == top, jrow_f, float(nb)), axis=0, keepdims=True)
                pick = jrow_f == first
                visible = visible | (pick & past)
                sc = jnp.where(pick, jnp.finfo(F32).min, sc)
            cols = jnp.concatenate([jnp.where(visible, 0.0, NEG_INF), jnp.zeros((D - nb, BLK), F32)], axis=0)
            xt_ref[rows, :] = cols.T.astype(BF16)
        return carry
    lax.fori_loop(0, nb // U, visibility, 0)

    def set_query_operand(s, slot):
        rows = pl.ds(pl.multiple_of(s * R, R), R)
        qa_ref[slot, :, 0:D] = q_ref[0, rows, :]
        qa_ref[slot, :, D:2 * D] = xt_ref[rows, :]

    def key_operand(g):
        j0 = g * U
        row0 = pl.multiple_of(j0 * BLK, U * BLK)
        e_rows = []
        for u in range(U):
            e_rows += [etab_ref[j0 + u]] * (BLK // 16)
        return jnp.concatenate([k_ref[0, pl.ds(row0, U * BLK), :], jnp.concatenate(e_rows, axis=0)], axis=1)

    def block_scores(s, g, slot, a, ka):
        sc = lax.dot_general(qa_ref[slot, a * BLK:(a + 1) * BLK, :], ka, NT_DIMS, preferred_element_type=F32)
        bias = jnp.concatenate([bias_ref[jnp.clip(s * QB + a - (g * U + u), 0, n_near)] for u in range(U)], axis=1)
        return sc + bias

    def pair_scores(s, g, slot):
        ka = key_operand(g)
        return jnp.concatenate([block_scores(s, g, slot, a, ka) for a in range(QB)], axis=0)

    def lane_chunks(x):
        return [x[:, c * D:(c + 1) * D] for c in range(x.shape[1] // D)]

    @pl.when(wide_ref[0] == 1)
    def _():
        def step_max(s, carry):
            set_query_operand(s, 0)

            def max_step(g, mm):
                for c in lane_chunks(pair_scores(s, g, 0)):
                    mm = jnp.maximum(mm, c)
                return mm
            mm = lax.fori_loop(0, s + 1, max_step, jnp.full((R, D), jnp.finfo(F32).min, F32))
            m = jnp.max(mm, axis=1, keepdims=True)
            rows = pl.ds(pl.multiple_of(s * R, R), R)
            lane_q = lax.broadcasted_iota(jnp.int32, (R, D), 1)
            xt_ref[rows, :] = jnp.where(lane_q == col_shift, -m, xt_ref[rows, :].astype(F32)).astype(BF16)
            return carry
        lax.fori_loop(0, n_steps, step_max, 0)

    def value_operand(g):
        return v_ref[0, pl.ds(pl.multiple_of(g * (U * BLK), U * BLK), U * BLK), :]

    def produce(s, g, slot, nxt_ref):
        p = jnp.exp(pair_scores(s, g, slot))
        nxt_ref[...] = p.astype(BF16)
        chunks = lane_chunks(p)
        rs = chunks[0]
        for c in chunks[1:]:
            rs = rs + c
        return rs

    def finish_step(s, g, acc, lsum):
        @pl.when(g == s)
        def _():
            rows = pl.ds(pl.multiple_of(s * R, R), R)
            out = acc / jnp.sum(lsum, axis=1, keepdims=True)
            z = z_ref[0, rows, :].astype(F32)
            o_ref[0, rows, :] = (out * _silu(z)).astype(o_ref.dtype)
        return acc * jnp.where(g == s, 0.0, 1.0).astype(F32)

    def stage(cur_ref, nxt_ref, carry):
        s, g, acc, lsum, lsum_done = carry
        wrap = g == s
        s2 = jnp.where(wrap, s + 1, s)
        g2 = jnp.where(wrap, 0, g + 1)

        @pl.when(wrap)
        def _():
            set_query_operand(s2, s2 & 1)

        rs = produce(s2, g2, s2 & 1, nxt_ref)
        acc = acc + jnp.dot(cur_ref[...], value_operand(g), preferred_element_type=F32)
        lsum_done = jnp.where(wrap, lsum, lsum_done)
        lsum = jnp.where(wrap, rs, lsum + rs)
        acc = finish_step(s, g, acc, lsum_done)
        return s2, g2, acc, lsum, lsum_done

    def two_stages(t, carry):
        return stage(s1_ref, s0_ref, stage(s0_ref, s1_ref, carry))

    n_pairs = n_steps * (n_steps + 1) // 2
    set_query_operand(0, 0)
    zero = jnp.zeros((R, D), F32)
    carry = (jnp.int32(0), jnp.int32(0), zero, produce(0, 0, 0, s0_ref), zero)
    carry = lax.fori_loop(0, (n_pairs - 1) // 2, two_stages, carry)
    last_ref = s0_ref
    if (n_pairs - 1) % 2:
        carry = stage(s0_ref, s1_ref, carry)
        last_ref = s1_ref
    s, g, acc, lsum, _ = carry
    acc = acc + jnp.dot(last_ref[...], value_operand(g), preferred_element_type=F32)
    finish_step(s, g, acc, lsum)


def _moba(rel_bias, p3, n_heads, col_q, col_k, col_v, col_z, group=4, qblocks=4):
    b, s, _ = p3.shape
    BLK, D = MOBA_BLOCK, LANES
    nb = s // BLK
    nbp = -(-nb // 16) * 16
    assert nbp + 16 <= D, "block-visibility columns must fit in the spare contraction columns"
    assert group == qblocks and nb % group == 0, "step s visits key groups 0..s"
    n_near = _n_near_tiles()
    rows = qblocks * BLK
    seq = lambda off: pl.BlockSpec((1, s, D), lambda hi, bi: (bi, 0, off + hi))
    return pl.pallas_call(
        functools.partial(_moba_kernel, n_near=n_near, nbp=nbp, group=group, qblocks=qblocks),
        grid=(n_heads, b),
        in_specs=[pl.BlockSpec(memory_space=pltpu.SMEM), seq(col_q), seq(col_k), seq(col_v), seq(col_z)],
        out_specs=seq(0),
        out_shape=jax.ShapeDtypeStruct((b, s, n_heads * D), BF16),
        scratch_shapes=[pltpu.VMEM((nb, D), F32), pltpu.VMEM((n_near + 1, BLK, BLK), F32),
                        pltpu.VMEM((nb, 16, D), BF16), pltpu.VMEM((s, D), BF16), pltpu.VMEM((2, rows, 2 * D), BF16),
                        pltpu.VMEM((rows, group * BLK), BF16), pltpu.VMEM((rows, group * BLK), BF16),
                        pltpu.SMEM((1,), jnp.int32)],
        compiler_params=_cparams(("arbitrary", "arbitrary")),
        name="moba_attention",
    )(rel_bias, p3, p3, p3, p3)


def _merge_kernel(ya_ref, yb_ref, g0_ref, g1_ref, x_ref, w0_ref, w1_ref, wo_ref, b0_ref, b1_ref, nw_ref, o_ref):
    pa = jnp.dot(ya_ref[...], w0_ref[...], preferred_element_type=F32)
    pb = jnp.dot(yb_ref[...], w1_ref[...], preferred_element_type=F32)
    g0 = _sigmoid(g0_ref[...].astype(F32) + b0_ref[...])
    g1 = _sigmoid(g1_ref[...].astype(F32) + b1_ref[...])
    merged = (g0 * pa + g1 * pb).astype(BF16)
    out = jnp.dot(merged, wo_ref[...], preferred_element_type=F32)
    y = out * lax.rsqrt(jnp.mean(out * out, axis=-1, keepdims=True) + NORM_EPS) * nw_ref[...]
    o_ref[...] = x_ref[...] + y


def _merge(ya, yb, p2, x2d, w0, w1, wo, b_gate, norm_w, col_g0, col_g1, tm=512):
    t, d = x2d.shape
    row = lambda c: pl.BlockSpec((tm, d), lambda i: (i, c))
    full = pl.BlockSpec((d, d), lambda i: (0, 0))
    vec = lambda c: pl.BlockSpec((1, d), lambda i: (0, c))
    return pl.pallas_call(
        _merge_kernel,
        grid=(t // tm,),
        in_specs=[row(0), row(0), row(col_g0), row(col_g1), row(0), full, full, full, vec(0), vec(1), vec(0)],
        out_specs=row(0),
        out_shape=jax.ShapeDtypeStruct((t, d), F32),
        compiler_params=_cparams(("parallel",)),
        name="merge_out",
    )(ya, yb, p2, p2, x2d, w0, w1, wo, b_gate.reshape(1, -1), b_gate.reshape(1, -1), norm_w.reshape(1, d))


def kernel(x, norm_pre_w, w_in, b_gate, conv_w, dn_a_log, dn_dt_bias, dn_onorm_w, rel_bias, w_branch, w_out,
           norm_post_w):
    b, s, d = x.shape
    n_heads = dn_a_log.shape[0]
    hd = dn_onorm_w.shape[0]
    width = n_heads * hd
    assert hd == LANES and rel_bias.shape[1] == n_heads and width == d
    assert s % MOBA_BLOCK == 0 and s % DN_GROUP == 0 and 2 * n_heads <= LANES
    t = b * s

    c_za = 3 * width
    c_beta = 4 * width
    c_qb = c_beta + 2 * n_heads
    assert conv_w.shape[1] == c_za
    w_qkv_a = w_in[:, :c_za].astype(BF16)
    w_rest = jnp.concatenate([w_in[:, c_za:c_beta], w_in[:, c_qb:]], axis=1).astype(BF16)
    w_small = jnp.pad(w_in[:, c_beta:c_qb].astype(BF16), ((0, 0), (0, LANES - 2 * n_heads)))
    nblk = width // LANES
    col = lambda k: k * nblk

    x2d = x.reshape(t, d)
    norm_w = norm_pre_w.astype(F32)
    col_scale = jnp.concatenate([jnp.ones((1, width), F32), jnp.full((1, width), hd ** -0.5, F32),
                                 jnp.ones((1, w_rest.shape[1] - 2 * width), F32)], axis=1)
    qkv_a, rest, small = _in_proj(x2d, norm_w, w_qkv_a, w_rest, col_scale, w_small, conv_w.astype(F32), s)
    qkv_a3 = qkv_a.reshape(b, s, -1)
    rest3 = rest.reshape(b, s, -1)

    pad = (0, LANES - 2 * n_heads)
    a_log_row = jnp.pad(jnp.concatenate([jnp.zeros_like(dn_a_log), dn_a_log]), pad).reshape(1, LANES).astype(F32)
    dt_row = jnp.pad(jnp.concatenate([jnp.zeros_like(dn_dt_bias), dn_dt_bias]), pad).reshape(1, LANES).astype(F32)
    ya = _deltanet(qkv_a3, rest3, small.reshape(b, s, LANES), a_log_row, dt_row,
                   dn_onorm_w.reshape(1, hd).astype(F32), n_heads, col(0), col(1), col(2), col(0))
    yb = _moba(rel_bias.astype(F32), rest3, n_heads, col(1), col(2), col(3), col(4))

    out = _merge(ya.reshape(t, d), yb.reshape(t, d), rest, x2d,
                 w_branch[0].astype(BF16), w_branch[1].astype(BF16), w_out.astype(BF16),
                 b_gate.astype(F32), norm_post_w.astype(F32), 5, 6)
    return out.reshape(b, s, d)
```

```python
import functools
import math

import jax
import jax.numpy as jnp
from jax import lax
from jax.experimental import pallas as pl
from jax.experimental.pallas import tpu as pltpu

F32 = jnp.float32
BF16 = jnp.bfloat16

LANES = 128
DN_CHUNK = 64
DN_GROUP = 128
DN_CONV_HALO = 8
MOBA_BLOCK = 256
MOBA_TOPK = 3
MOBA_SHIFT_SLACK = 40.0
REL_BUCKETS = 32
REL_MAX_DIST = 2048
NORM_EPS = 1e-6
NEG_INF = -1e30
VMEM_LIMIT = 56 * 1024 * 1024
NT_DIMS = (((1,), (1,)), ((), ()))


def _cparams(sem):
    return pltpu.CompilerParams(dimension_semantics=sem, vmem_limit_bytes=VMEM_LIMIT)


def _sigmoid(x):
    return 0.5 * jnp.tanh(0.5 * x) + 0.5


def _silu(x):
    h = 0.5 * x
    return h * jnp.tanh(h) + h


def _normalise(x, w):
    y = x * lax.rsqrt(jnp.mean(x * x, axis=-1, keepdims=True) + NORM_EPS)
    return (y * w).astype(BF16)


def _in_proj_kernel(x_ref, xh_ref, nw_ref, wc_ref, wr_ref, cs_ref, ws_ref, cw_ref, oc_ref, or_ref, sm_ref,
                    h_ref, hh_ref, *, tiles_per_seq, row_chunks):
    i = pl.program_id(0)
    tm = oc_ref.shape[0]
    halo = hh_ref.shape[0]

    @pl.when(pl.program_id(1) == 0)
    def _():
        h = _normalise(x_ref[...], nw_ref[...])
        h_ref[...] = h
        sm_ref[...] = jnp.dot(h, ws_ref[...], preferred_element_type=F32)
        hh = _normalise(xh_ref[...], nw_ref[...])
        hh_ref[...] = jnp.where(i % tiles_per_seq == 0, jnp.zeros_like(hh), hh)

    wc = wc_ref[...]
    wr = wr_ref[...]
    cw = cw_ref[...]
    kw = cw.shape[0]
    base = halo - (kw - 1)
    rm = tm // row_chunks
    tail = jnp.dot(hh_ref[...], wc, preferred_element_type=F32)
    for r in range(row_chunks):
        rows = slice(r * rm, (r + 1) * rm)
        hr = h_ref[rows, :]
        acc_c = jnp.dot(hr, wc, preferred_element_type=F32)
        acc_r = jnp.dot(hr, wr, preferred_element_type=F32)
        or_ref[rows, :] = (acc_r * cs_ref[...]).astype(or_ref.dtype)
        ext = jnp.concatenate([tail, acc_c], axis=0)
        y = ext[base:base + rm] * cw[0:1]
        for kk in range(1, kw):
            y = y + ext[base + kk:base + kk + rm] * cw[kk:kk + 1]
        oc_ref[rows, :] = _silu(y).astype(oc_ref.dtype)
        tail = acc_c[rm - halo:rm]


def _in_proj(x2d, norm_w, w_conv, w_rest, col_scale, w_small, conv_w, seq_len, tm=1024, steps=4, halo=16,
             row_chunks=4):
    t, d = x2d.shape
    nc, nr, ns = w_conv.shape[1], w_rest.shape[1], w_small.shape[1]
    kw = conv_w.shape[0]
    assert nc % (steps * LANES) == 0 and nr % (steps * LANES) == 0
    assert seq_len % tm == 0 and tm % halo == 0 and kw - 1 <= halo and conv_w.shape[1] == nc
    tc, tr = nc // steps, nr // steps
    row = lambda w: pl.BlockSpec((tm, w), lambda i, j: (i, j))
    col = lambda r, w: pl.BlockSpec((r, w), lambda i, j: (0, j))
    fixed = lambda r, w: pl.BlockSpec((r, w), lambda i, j: (0, 0))
    return pl.pallas_call(
        functools.partial(_in_proj_kernel, tiles_per_seq=seq_len // tm, row_chunks=row_chunks),
        grid=(t // tm, steps),
        in_specs=[pl.BlockSpec((tm, d), lambda i, j: (i, 0)),
                  pl.BlockSpec((halo, d), lambda i, j: (jnp.maximum(i * (tm // halo) - 1, 0), 0)),
                  fixed(1, d), col(d, tc), col(d, tr), col(1, tr), fixed(d, ns), col(kw, tc)],
        out_specs=[row(tc), row(tr), pl.BlockSpec((tm, ns), lambda i, j: (i, 0))],
        out_shape=[jax.ShapeDtypeStruct((t, nc), BF16), jax.ShapeDtypeStruct((t, nr), BF16),
                   jax.ShapeDtypeStruct((t, ns), F32)],
        scratch_shapes=[pltpu.VMEM((tm, d), BF16), pltpu.VMEM((halo, d), BF16)],
        compiler_params=_cparams(("parallel", "arbitrary")),
        name="in_proj",
    )(x2d, x2d, norm_w.reshape(1, d), w_conv, w_rest, col_scale, w_small, conv_w)


def _deltanet_kernel(q_ref, k_ref, v_ref, z_ref, sm_ref, alog_ref, dtb_ref, onw_ref,
                     o_ref, state_ref, vnew_ref, gct_ref, *, n_heads, heads_per_step):
    hg = pl.program_id(1)
    g = pl.program_id(2)
    G, C, D, NH = DN_GROUP, DN_CHUNK, LANES, heads_per_step
    lc = C.bit_length() - 1

    @pl.when(g == 0)
    def _():
        state_ref[...] = jnp.zeros_like(state_ref)

    def l2n(x):
        return x * lax.rsqrt(jnp.sum(x * x, axis=-1, keepdims=True) + NORM_EPS)

    NB = q_ref.shape[0]
    qf = [q_ref[bi].astype(F32) for bi in range(NB)]
    kf = [k_ref[bi].astype(F32) for bi in range(NB)]
    vf = [v_ref[bi].astype(F32) for bi in range(NB)]

    lane = lax.broadcasted_iota(jnp.int32, (G, LANES), 1)
    row = lax.broadcasted_iota(jnp.int32, (G, LANES), 0)
    pos = row & (C - 1)
    beta_all, gc_all, gl_all = [], [], []
    for bi in range(NB):
        cs = sm_ref[bi]
        beta_all.append(_sigmoid(cs))
        xg = cs + dtb_ref[...]
        softplus = jnp.maximum(xg, 0.0) + jnp.log1p(jnp.exp(-jnp.abs(xg)))
        gc = -jnp.exp(alog_ref[...]) * softplus
        shift = 1
        while shift < C:
            gc = gc + jnp.where(pos >= shift, pltpu.roll(gc, shift, 0), 0.0)
            shift *= 2
        gc_all.append(gc)
        gl_all.append(jnp.concatenate([jnp.broadcast_to(gc[c * C + C - 1:c * C + C], (C, LANES))
                                       for c in range(G // C)], axis=0))
        gct_ref[bi] = gc.T

    ri = lax.broadcasted_iota(jnp.int32, (G, G), 0)
    ci = lax.broadcasted_iota(jnp.int32, (G, G), 1)
    same = (ri >> lc) == (ci >> lc)
    tri_incl = same & (ri >= ci)
    tri_strict = same & (ri > ci)
    eye = jnp.where(ri == ci, 1.0, 0.0)
    ct = lax.broadcasted_iota(jnp.int32, (D, G), 1) >> lc

    def pair_mask(s):
        ls = s.bit_length() - 1
        return ((ri >> (ls + 1)) == (ci >> (ls + 1))) & (((ri >> ls) & 1) == 1) & (((ci >> ls) & 1) == 0)

    pair_masks = {}
    s = 1
    while s < C:
        pair_masks[s] = pair_mask(s)
        s *= 2

    heads = range(NB * NH)
    bis = [a // NH for a in heads]
    sls = [slice((a % NH) * D, (a % NH + 1) * D) for a in heads]
    hidx = [hg * NH + a % NH for a in heads]

    def col_of(x, lane_idx):
        return jnp.sum(jnp.where(lane == lane_idx, x, 0.0), axis=1, keepdims=True)

    q = [l2n(qf[bis[a]][:, sls[a]]) * (D ** -0.5) for a in heads]
    k = [l2n(kf[bis[a]][:, sls[a]]) for a in heads]
    beta_col = [col_of(beta_all[bis[a]], hidx[a]) for a in heads]
    gc_col = [col_of(gc_all[bis[a]], hidx[a] + n_heads) for a in heads]
    gl_col = [col_of(gl_all[bis[a]], hidx[a] + n_heads) for a in heads]
    gc_row = [gct_ref[bis[a], pl.ds(hidx[a] + n_heads, 1), :] for a in heads]
    decay = [jnp.where(tri_incl, jnp.exp(jnp.where(tri_incl, gc_col[a] - gc_row[a], 0.0)), 0.0) for a in heads]
    kb = [k[a] * beta_col[a] for a in heads]
    vb = [vf[bis[a]][:, sls[a]] * beta_col[a] for a in heads]
    k16 = [x.astype(BF16) for x in k]
    kk = [lax.dot_general(kb[a].astype(BF16), k16[a], NT_DIMS, preferred_element_type=F32) for a in heads]
    qk = [lax.dot_general(q[a].astype(BF16), k16[a], NT_DIMS, preferred_element_type=F32) for a in heads]
    lower = [jnp.where(tri_strict, kk[a] * decay[a], 0.0) for a in heads]
    attn = [jnp.where(tri_incl, qk[a] * decay[a], 0.0).astype(BF16) for a in heads]

    tinv = [eye - jnp.where(pair_masks[1], lower[a], 0.0) for a in heads]
    s = 2
    while s < C:
        t16 = [x.astype(BF16) for x in tinv]
        md = [jnp.dot(jnp.where(pair_masks[s], lower[a], 0.0).astype(BF16), t16[a], preferred_element_type=F32)
              for a in heads]
        tinv = [tinv[a] - jnp.dot(t16[a], md[a].astype(BF16), preferred_element_type=F32) for a in heads]
        s *= 2

    egc = [jnp.exp(x) for x in gc_col]
    uw = [jnp.dot(tinv[a].astype(BF16), jnp.concatenate([vb[a], kb[a] * egc[a]], axis=1).astype(BF16),
                  preferred_element_type=F32) for a in heads]
    q_dec = [q[a] * egc[a] for a in heads]
    k_dec_t = [(k[a] * jnp.exp(gl_col[a] - gc_col[a])).T for a in heads]
    egl = [jnp.exp(x) for x in gl_col]

    vnew_ref[...] = jnp.zeros_like(vnew_ref)
    outs = [[] for _ in heads]
    for c in range(G // C):
        r0 = c * C
        st = [state_ref[a] for a in heads]
        ws_qs = [jnp.dot(jnp.concatenate([uw[a][r0:r0 + C, D:], q_dec[a][r0:r0 + C]], axis=0).astype(BF16),
                         st[a].astype(BF16), preferred_element_type=F32) for a in heads]
        for a in heads:
            vnew_ref[a, r0:r0 + C, :] = (uw[a][r0:r0 + C, :D] - ws_qs[a][:C]).astype(BF16)
        vn = [vnew_ref[a] for a in heads]
        for a in heads:
            outs[a].append(ws_qs[a][C:] + jnp.dot(attn[a][r0:r0 + C], vn[a], preferred_element_type=F32))
        for a in heads:
            kt_c = jnp.where(ct == c, k_dec_t[a], 0.0).astype(BF16)
            e_c = jnp.concatenate([egl[a][r0:r0 + C]] * (D // C), axis=0)
            state_ref[a] = st[a] * e_c + jnp.dot(kt_c, vn[a], preferred_element_type=F32)

    for a in heads:
        o = jnp.concatenate(outs[a], axis=0)
        o = o * lax.rsqrt(jnp.mean(o * o, axis=-1, keepdims=True) + NORM_EPS) * onw_ref[...]
        z = z_ref[bis[a], :, sls[a]].astype(F32)
        o_ref[bis[a], :, sls[a]] = (o * _silu(z)).astype(o_ref.dtype)


def _deltanet(qkv3, z3, small3, a_log_row, dt_row, onorm_w, n_heads, col_q, col_k, col_v, col_z, heads_per_step=8,
              batch_rows_per_step=2):
    b, s, _ = qkv3.shape
    G, D, NH = DN_GROUP, LANES, heads_per_step
    NB = batch_rows_per_step if b % batch_rows_per_step == 0 else 1
    assert n_heads % NH == 0 and all(c % NH == 0 for c in (col_q, col_k, col_v, col_z))
    blk = lambda off: pl.BlockSpec((NB, G, NH * D), lambda bi, hi, gi: (bi, gi, off // NH + hi))
    row_spec = pl.BlockSpec((1, LANES), lambda bi, hi, gi: (0, 0))
    return pl.pallas_call(
        functools.partial(_deltanet_kernel, n_heads=n_heads, heads_per_step=NH),
        grid=(b // NB, n_heads // NH, s // G),
        in_specs=[blk(col_q), blk(col_k), blk(col_v), blk(col_z),
                  pl.BlockSpec((NB, G, LANES), lambda bi, hi, gi: (bi, gi, 0)),
                  row_spec, row_spec, row_spec],
        out_specs=pl.BlockSpec((NB, G, NH * D), lambda bi, hi, gi: (bi, gi, hi)),
        out_shape=jax.ShapeDtypeStruct((b, s, n_heads * D), BF16),
        scratch_shapes=[pltpu.VMEM((NB * NH, D, D), F32),
                        pltpu.VMEM((NB * NH, G, D), BF16), pltpu.VMEM((NB, LANES, G), F32)],
        compiler_params=_cparams(("parallel", "parallel", "arbitrary")),
        name="gated_deltanet",
    )(qkv3, qkv3, qkv3, z3, small3, a_log_row, dt_row, onorm_w)


def _t5_bucket_f32(dist):
    dist = jnp.maximum(dist, 0)
    max_exact = REL_BUCKETS // 2
    d = jnp.maximum(dist, 1).astype(F32)
    large = max_exact + (jnp.log(d / max_exact) / math.log(REL_MAX_DIST / max_exact)
                         * (REL_BUCKETS - max_exact)).astype(jnp.int32)
    large = jnp.minimum(large, REL_BUCKETS - 1)
    return jnp.where(dist < max_exact, dist, large)


def _n_near_tiles():
    return -(-(REL_MAX_DIST + MOBA_BLOCK - 1) // MOBA_BLOCK)


def _moba_kernel(rb_ref, q_ref, k_ref, v_ref, z_ref, o_ref, kmean_ref, bias_ref, etab_ref, xt_ref, qa_ref,
                 s0_ref, s1_ref, wide_ref, *, n_near, nbp, group, qblocks):
    hh = pl.program_id(0)
    bb = pl.program_id(1)
    BLK, D, U, QB = MOBA_BLOCK, LANES, group, qblocks
    R = QB * BLK
    nb = k_ref.shape[1] // BLK
    n_steps = nb // QB
    col_shift = nbp

    @pl.when(bb == 0)
    def _():
        qr = lax.broadcasted_iota(jnp.int32, (BLK, BLK), 0)
        kc = lax.broadcasted_iota(jnp.int32, (BLK, BLK), 1)
        for t in range(n_near):
            dist = t * BLK + qr - kc
            bucket = _t5_bucket_f32(dist)
            val = jnp.full((BLK, BLK), rb_ref[0, hh], F32)
            for bk in range(1, REL_BUCKETS):
                val = jnp.where(bucket == bk, rb_ref[bk, hh], val)
            if t == 0:
                val = jnp.where(dist >= 0, val, NEG_INF)
            bias_ref[t] = val
        bias_ref[n_near] = jnp.full((BLK, BLK), rb_ref[REL_BUCKETS - 1, hh], F32)
        lane = lax.broadcasted_iota(jnp.int32, (16, D), 1)
        for j in range(nb):
            etab_ref[j] = jnp.where((lane == j) | (lane == col_shift), 1.0, 0.0).astype(BF16)

    ones_dd = jnp.ones((D, D), BF16)

    def norms(it, carry):
        k2, q2 = carry
        for u in range(U):
            j = it * U + u
            rows = pl.ds(pl.multiple_of(j * BLK, BLK), BLK)
            kj = k_ref[0, rows, :].astype(F32)
            qj = q_ref[0, rows, :].astype(F32)
            kmean_ref[pl.ds(j, 1), :] = jnp.mean(kj, axis=0, keepdims=True)
            k2 = jnp.maximum(k2, jnp.dot((kj * kj).astype(BF16), ones_dd, preferred_element_type=F32))
            q2 = jnp.maximum(q2, jnp.dot((qj * qj).astype(BF16), ones_dd, preferred_element_type=F32))
        return k2, q2
    k2, q2 = lax.fori_loop(0, nb // U, norms, (jnp.zeros((BLK, D), F32), jnp.zeros((BLK, D), F32)))
    rb_abs = jnp.abs(rb_ref[0, hh])
    for bk in range(1, REL_BUCKETS):
        rb_abs = jnp.maximum(rb_abs, jnp.abs(rb_ref[bk, hh]))
    bound = jnp.sqrt(jnp.max(q2) * jnp.max(k2)) * 1.02 + rb_abs + 1e-3
    wide_ref[0] = jnp.where(bound > MOBA_SHIFT_SLACK, 1, 0).astype(jnp.int32)

    kmean16 = kmean_ref[...].astype(BF16)
    jrow = lax.broadcasted_iota(jnp.int32, (nb, BLK), 0)
    jrow_f = jrow.astype(F32)

    def visibility(it, carry):
        for u in range(U):
            ib = it * U + u
            rows = pl.ds(pl.multiple_of(ib * BLK, BLK), BLK)
            gate = lax.dot_general(kmean16, q_ref[0, rows, :], NT_DIMS, preferred_element_type=F32)
            past = jrow < ib
            sc = jnp.where(past, gate, NEG_INF)
            visible = jrow == ib
            for _ in range(MOBA_TOPK):
                top = jnp.max(sc, axis=0, keepdims=True)
                first = jnp.min(jnp.where(sc == top, jrow_f, float(nb)), axis=0, keepdims=True)
                pick = jrow_f == first
                visible = visible | (pick & past)
                sc = jnp.where(pick, jnp.finfo(F32).min, sc)
            cols = jnp.concatenate([jnp.where(visible, 0.0, NEG_INF), jnp.zeros((D - nb, BLK), F32)], axis=0)
            xt_ref[rows, :] = cols.T.astype(BF16)
        return carry
    lax.fori_loop(0, nb // U, visibility, 0)

    def set_query_operand(s, slot):
        rows = pl.ds(pl.multiple_of(s * R, R), R)
        qa_ref[slot, :, 0:D] = q_ref[0, rows, :]
        qa_ref[slot, :, D:2 * D] = xt_ref[rows, :]

    def key_operand(g):
        j0 = g * U
        row0 = pl.multiple_of(j0 * BLK, U * BLK)
        e_rows = []
        for u in range(U):
            e_rows += [etab_ref[j0 + u]] * (BLK // 16)
        return jnp.concatenate([k_ref[0, pl.ds(row0, U * BLK), :], jnp.concatenate(e_rows, axis=0)], axis=1)

    def block_scores(s, g, slot, a, ka):
        sc = lax.dot_general(qa_ref[slot, a * BLK:(a + 1) * BLK, :], ka, NT_DIMS, preferred_element_type=F32)
        bias = jnp.concatenate([bias_ref[jnp.clip(s * QB + a - (g * U + u), 0, n_near)] for u in range(U)], axis=1)
        return sc + bias

    def pair_scores(s, g, slot):
        ka = key_operand(g)
        return jnp.concatenate([block_scores(s, g, slot, a, ka) for a in range(QB)], axis=0)

    def lane_chunks(x):
        return [x[:, c * D:(c + 1) * D] for c in range(x.shape[1] // D)]

    @pl.when(wide_ref[0] == 1)
    def _():
        def step_max(s, carry):
            set_query_operand(s, 0)

            def max_step(g, mm):
                for c in lane_chunks(pair_scores(s, g, 0)):
                    mm = jnp.maximum(mm, c)
                return mm
            mm = lax.fori_loop(0, s + 1, max_step, jnp.full((R, D), jnp.finfo(F32).min, F32))
            m = jnp.max(mm, axis=1, keepdims=True)
            rows = pl.ds(pl.multiple_of(s * R, R), R)
            lane_q = lax.broadcasted_iota(jnp.int32, (R, D), 1)
            xt_ref[rows, :] = jnp.where(lane_q == col_shift, -m, xt_ref[rows, :].astype(F32)).astype(BF16)
            return carry
        lax.fori_loop(0, n_steps, step_max, 0)

    ones_cols = jnp.ones((U * BLK, D), BF16)

    def value_operand(g):
        vg = v_ref[0, pl.ds(pl.multiple_of(g * (U * BLK), U * BLK), U * BLK), :]
        return jnp.concatenate([vg, ones_cols], axis=1)

    def consume(vg, sc, acc):
        return acc + jnp.dot(jnp.exp(sc).astype(BF16), vg, preferred_element_type=F32)

    def finish_step(s, g, acc):
        @pl.when(g == s)
        def _():
            rows = pl.ds(pl.multiple_of(s * R, R), R)
            out = acc[:, 0:D] / acc[:, D:D + 1]
            z = z_ref[0, rows, :].astype(F32)
            o_ref[0, rows, :] = (out * _silu(z)).astype(o_ref.dtype)
        return acc * jnp.where(g == s, 0.0, 1.0).astype(F32)

    def stage(cur_ref, nxt_ref, carry):
        s, g, acc = carry
        wrap = g == s
        s2 = jnp.where(wrap, s + 1, s)
        g2 = jnp.where(wrap, 0, g + 1)

        @pl.when(wrap)
        def _():
            set_query_operand(s2, s2 & 1)

        nxt_ref[...] = pair_scores(s2, g2, s2 & 1)
        acc = consume(value_operand(g), cur_ref[...], acc)
        return s2, g2, finish_step(s, g, acc)

    def two_stages(t, carry):
        return stage(s1_ref, s0_ref, stage(s0_ref, s1_ref, carry))

    n_pairs = n_steps * (n_steps + 1) // 2
    set_query_operand(0, 0)
    s0_ref[...] = pair_scores(0, 0, 0)
    carry = (jnp.int32(0), jnp.int32(0), jnp.zeros((R, 2 * D), F32))
    carry = lax.fori_loop(0, (n_pairs - 1) // 2, two_stages, carry)
    last_ref = s0_ref
    if (n_pairs - 1) % 2:
        carry = stage(s0_ref, s1_ref, carry)
        last_ref = s1_ref
    s, g, acc = carry
    finish_step(s, g, consume(value_operand(g), last_ref[...], acc))


def _moba(rel_bias, p3, n_heads, col_q, col_k, col_v, col_z, group=4, qblocks=4):
    b, s, _ = p3.shape
    BLK, D = MOBA_BLOCK, LANES
    nb = s // BLK
    nbp = -(-nb // 16) * 16
    assert nbp + 16 <= D, "block-visibility columns must fit in the spare contraction columns"
    assert group == qblocks and nb % group == 0, "step s visits key groups 0..s"
    n_near = _n_near_tiles()
    rows = qblocks * BLK
    seq = lambda off: pl.BlockSpec((1, s, D), lambda hi, bi: (bi, 0, off + hi))
    return pl.pallas_call(
        functools.partial(_moba_kernel, n_near=n_near, nbp=nbp, group=group, qblocks=qblocks),
        grid=(n_heads, b),
        in_specs=[pl.BlockSpec(memory_space=pltpu.SMEM), seq(col_q), seq(col_k), seq(col_v), seq(col_z)],
        out_specs=seq(0),
        out_shape=jax.ShapeDtypeStruct((b, s, n_heads * D), BF16),
        scratch_shapes=[pltpu.VMEM((nb, D), F32), pltpu.VMEM((n_near + 1, BLK, BLK), F32),
                        pltpu.VMEM((nb, 16, D), BF16), pltpu.VMEM((s, D), BF16), pltpu.VMEM((2, rows, 2 * D), BF16),
                        pltpu.VMEM((rows, group * BLK), F32), pltpu.VMEM((rows, group * BLK), F32),
                        pltpu.SMEM((1,), jnp.int32)],
        compiler_params=_cparams(("arbitrary", "arbitrary")),
        name="moba_attention",
    )(rel_bias, p3, p3, p3, p3)


def _merge_kernel(ya_ref, yb_ref, g0_ref, g1_ref, x_ref, w0_ref, w1_ref, wo_ref, b0_ref, b1_ref, nw_ref, o_ref):
    pa = jnp.dot(ya_ref[...], w0_ref[...], preferred_element_type=F32)
    pb = jnp.dot(yb_ref[...], w1_ref[...], preferred_element_type=F32)
    g0 = _sigmoid(g0_ref[...].astype(F32) + b0_ref[...])
    g1 = _sigmoid(g1_ref[...].astype(F32) + b1_ref[...])
    merged = (g0 * pa + g1 * pb).astype(BF16)
    out = jnp.dot(merged, wo_ref[...], preferred_element_type=F32)
    y = out * lax.rsqrt(jnp.mean(out * out, axis=-1, keepdims=True) + NORM_EPS) * nw_ref[...]
    o_ref[...] = x_ref[...] + y


def _merge(ya, yb, p2, x2d, w0, w1, wo, b_gate, norm_w, col_g0, col_g1, tm=512):
    t, d = x2d.shape
    row = lambda c: pl.BlockSpec((tm, d), lambda i: (i, c))
    full = pl.BlockSpec((d, d), lambda i: (0, 0))
    vec = lambda c: pl.BlockSpec((1, d), lambda i: (0, c))
    return pl.pallas_call(
        _merge_kernel,
        grid=(t // tm,),
        in_specs=[row(0), row(0), row(col_g0), row(col_g1), row(0), full, full, full, vec(0), vec(1), vec(0)],
        out_specs=row(0),
        out_shape=jax.ShapeDtypeStruct((t, d), F32),
        compiler_params=_cparams(("parallel",)),
        name="merge_out",
    )(ya, yb, p2, p2, x2d, w0, w1, wo, b_gate.reshape(1, -1), b_gate.reshape(1, -1), norm_w.reshape(1, d))


def kernel(x, norm_pre_w, w_in, b_gate, conv_w, dn_a_log, dn_dt_bias, dn_onorm_w, rel_bias, w_branch, w_out,
           norm_post_w):
    b, s, d = x.shape
    n_heads = dn_a_log.shape[0]
    hd = dn_onorm_w.shape[0]
    width = n_heads * hd
    assert hd == LANES and rel_bias.shape[1] == n_heads and width == d
    assert s % MOBA_BLOCK == 0 and s % DN_GROUP == 0 and 2 * n_heads <= LANES
    t = b * s

    c_za = 3 * width
    c_beta = 4 * width
    c_qb = c_beta + 2 * n_heads
    assert conv_w.shape[1] == c_za
    w16 = w_in.astype(BF16)
    w_qkv_a = w16[:, :c_za]
    w_rest = jnp.concatenate([w16[:, c_za:c_beta], w16[:, c_qb:]], axis=1)
    w_small = jnp.pad(w16[:, c_beta:c_qb], ((0, 0), (0, LANES - 2 * n_heads)))
    nblk = width // LANES
    col = lambda k: k * nblk

    x2d = x.reshape(t, d)
    norm_w = norm_pre_w.astype(F32)
    col_scale = jnp.concatenate([jnp.ones((1, width), F32), jnp.full((1, width), hd ** -0.5, F32),
                                 jnp.ones((1, w_rest.shape[1] - 2 * width), F32)], axis=1)
    qkv_a, rest, small = _in_proj(x2d, norm_w, w_qkv_a, w_rest, col_scale, w_small, conv_w.astype(F32), s)
    qkv_a3 = qkv_a.reshape(b, s, -1)
    rest3 = rest.reshape(b, s, -1)

    pad = (0, LANES - 2 * n_heads)
    a_log_row = jnp.pad(jnp.concatenate([jnp.zeros_like(dn_a_log), dn_a_log]), pad).reshape(1, LANES).astype(F32)
    dt_row = jnp.pad(jnp.concatenate([jnp.zeros_like(dn_dt_bias), dn_dt_bias]), pad).reshape(1, LANES).astype(F32)
    ya = _deltanet(qkv_a3, rest3, small.reshape(b, s, LANES), a_log_row, dt_row,
                   dn_onorm_w.reshape(1, hd).astype(F32), n_heads, col(0), col(1), col(2), col(0))
    yb = _moba(rel_bias.astype(F32), rest3, n_heads, col(1), col(2), col(3), col(4))

    out = _merge(ya.reshape(t, d), yb.reshape(t, d), rest, x2d,
                 w_branch[0].astype(BF16), w_branch[1].astype(BF16), w_out.astype(BF16),
                 b_gate.astype(F32), norm_post_w.astype(F32), 5, 6)
    return out.reshape(b, s, d)
```

```python
import functools
import math

import jax
import jax.numpy as jnp
from jax import lax
from jax.experimental import pallas as pl
from jax.experimental.pallas import tpu as pltpu

F32 = jnp.float32
BF16 = jnp.bfloat16

LANES = 128
DN_CHUNK = 64
DN_GROUP = 128
DN_CONV_HALO = 8
MOBA_BLOCK = 256
MOBA_TOPK = 3
MOBA_SHIFT_SLACK = 40.0
REL_BUCKETS = 32
REL_MAX_DIST = 2048
NORM_EPS = 1e-6
NEG_INF = -1e30
VMEM_LIMIT = 56 * 1024 * 1024
NT_DIMS = (((1,), (1,)), ((), ()))


def _cparams(sem):
    return pltpu.CompilerParams(dimension_semantics=sem, vmem_limit_bytes=VMEM_LIMIT)


def _sigmoid(x):
    return 0.5 * jnp.tanh(0.5 * x) + 0.5


def _silu(x):
    h = 0.5 * x
    return h * jnp.tanh(h) + h


def _normalise(x, w):
    y = x * lax.rsqrt(jnp.mean(x * x, axis=-1, keepdims=True) + NORM_EPS)
    return (y * w).astype(BF16)


def _in_proj_kernel(x_ref, xh_ref, nw_ref, wc_ref, wr_ref, cs_ref, ws_ref, cw_ref, oc_ref, or_ref, sm_ref,
                    h_ref, hh_ref, *, tiles_per_seq, row_chunks):
    i = pl.program_id(0)
    tm = oc_ref.shape[0]
    halo = hh_ref.shape[0]

    @pl.when(pl.program_id(1) == 0)
    def _():
        h = _normalise(x_ref[...], nw_ref[...])
        h_ref[...] = h
        sm_ref[...] = jnp.dot(h, ws_ref[...], preferred_element_type=F32)
        hh = _normalise(xh_ref[...], nw_ref[...])
        hh_ref[...] = jnp.where(i % tiles_per_seq == 0, jnp.zeros_like(hh), hh)

    wc = wc_ref[...]
    wr = wr_ref[...]
    cw = cw_ref[...]
    kw = cw.shape[0]
    base = halo - (kw - 1)
    rm = tm // row_chunks
    tail = jnp.dot(hh_ref[...], wc, preferred_element_type=F32)
    for r in range(row_chunks):
        rows = slice(r * rm, (r + 1) * rm)
        hr = h_ref[rows, :]
        acc_c = jnp.dot(hr, wc, preferred_element_type=F32)
        acc_r = jnp.dot(hr, wr, preferred_element_type=F32)
        or_ref[rows, :] = (acc_r * cs_ref[...]).astype(or_ref.dtype)
        ext = jnp.concatenate([tail, acc_c], axis=0)
        y = ext[base:base + rm] * cw[0:1]
        for kk in range(1, kw):
            y = y + ext[base + kk:base + kk + rm] * cw[kk:kk + 1]
        oc_ref[rows, :] = _silu(y).astype(oc_ref.dtype)
        tail = acc_c[rm - halo:rm]


def _in_proj(x2d, norm_w, w_conv, w_rest, col_scale, w_small, conv_w, seq_len, tm=1024, steps=4, halo=16,
             row_chunks=4):
    t, d = x2d.shape
    nc, nr, ns = w_conv.shape[1], w_rest.shape[1], w_small.shape[1]
    kw = conv_w.shape[0]
    assert nc % (steps * LANES) == 0 and nr % (steps * LANES) == 0
    assert seq_len % tm == 0 and tm % halo == 0 and kw - 1 <= halo and conv_w.shape[1] == nc
    tc, tr = nc // steps, nr // steps
    row = lambda w: pl.BlockSpec((tm, w), lambda i, j: (i, j))
    col = lambda r, w: pl.BlockSpec((r, w), lambda i, j: (0, j))
    fixed = lambda r, w: pl.BlockSpec((r, w), lambda i, j: (0, 0))
    return pl.pallas_call(
        functools.partial(_in_proj_kernel, tiles_per_seq=seq_len // tm, row_chunks=row_chunks),
        grid=(t // tm, steps),
        in_specs=[pl.BlockSpec((tm, d), lambda i, j: (i, 0)),
                  pl.BlockSpec((halo, d), lambda i, j: (jnp.maximum(i * (tm // halo) - 1, 0), 0)),
                  fixed(1, d), col(d, tc), col(d, tr), col(1, tr), fixed(d, ns), col(kw, tc)],
        out_specs=[row(tc), row(tr), pl.BlockSpec((tm, ns), lambda i, j: (i, 0))],
        out_shape=[jax.ShapeDtypeStruct((t, nc), BF16), jax.ShapeDtypeStruct((t, nr), BF16),
                   jax.ShapeDtypeStruct((t, ns), F32)],
        scratch_shapes=[pltpu.VMEM((tm, d), BF16), pltpu.VMEM((halo, d), BF16)],
        compiler_params=_cparams(("parallel", "arbitrary")),
        name="in_proj",
    )(x2d, x2d, norm_w.reshape(1, d), w_conv, w_rest, col_scale, w_small, conv_w)


def _deltanet_kernel(q_ref, k_ref, v_ref, z_ref, sm_ref, alog_ref, dtb_ref, onw_ref,
                     o_ref, state_ref, vnew_ref, gct_ref, *, n_heads, heads_per_step):
    hg = pl.program_id(1)
    g = pl.program_id(2)
    G, C, D, NH = DN_GROUP, DN_CHUNK, LANES, heads_per_step
    lc = C.bit_length() - 1

    @pl.when(g == 0)
    def _():
        state_ref[...] = jnp.zeros_like(state_ref)

    def l2n(x):
        return x * lax.rsqrt(jnp.sum(x * x, axis=-1, keepdims=True) + NORM_EPS)

    NB = q_ref.shape[0]
    qf = [q_ref[bi].astype(F32) for bi in range(NB)]
    kf = [k_ref[bi].astype(F32) for bi in range(NB)]
    vf = [v_ref[bi].astype(F32) for bi in range(NB)]

    lane = lax.broadcasted_iota(jnp.int32, (G, LANES), 1)
    row = lax.broadcasted_iota(jnp.int32, (G, LANES), 0)
    pos = row & (C - 1)
    beta_all, gc_all, gl_all = [], [], []
    for bi in range(NB):
        cs = sm_ref[bi]
        beta_all.append(_sigmoid(cs))
        xg = cs + dtb_ref[...]
        softplus = jnp.maximum(xg, 0.0) + jnp.log1p(jnp.exp(-jnp.abs(xg)))
        gc = -jnp.exp(alog_ref[...]) * softplus
        shift = 1
        while shift < C:
            gc = gc + jnp.where(pos >= shift, pltpu.roll(gc, shift, 0), 0.0)
            shift *= 2
        gc_all.append(gc)
        gl_all.append(jnp.concatenate([jnp.broadcast_to(gc[c * C + C - 1:c * C + C], (C, LANES))
                                       for c in range(G // C)], axis=0))
        gct_ref[bi] = gc.T

    ri = lax.broadcasted_iota(jnp.int32, (G, G), 0)
    ci = lax.broadcasted_iota(jnp.int32, (G, G), 1)
    same = (ri >> lc) == (ci >> lc)
    tri_incl = same & (ri >= ci)
    tri_strict = same & (ri > ci)
    eye = jnp.where(ri == ci, 1.0, 0.0)
    ct = lax.broadcasted_iota(jnp.int32, (D, G), 1) >> lc

    def pair_mask(s):
        ls = s.bit_length() - 1
        return ((ri >> (ls + 1)) == (ci >> (ls + 1))) & (((ri >> ls) & 1) == 1) & (((ci >> ls) & 1) == 0)

    pair_masks = {}
    s = 1
    while s < C:
        pair_masks[s] = pair_mask(s)
        s *= 2

    heads = range(NB * NH)
    bis = [a // NH for a in heads]
    sls = [slice((a % NH) * D, (a % NH + 1) * D) for a in heads]
    hidx = [hg * NH + a % NH for a in heads]

    def col_of(x, lane_idx):
        return jnp.sum(jnp.where(lane == lane_idx, x, 0.0), axis=1, keepdims=True)

    q = [l2n(qf[bis[a]][:, sls[a]]) * (D ** -0.5) for a in heads]
    k = [l2n(kf[bis[a]][:, sls[a]]) for a in heads]
    beta_col = [col_of(beta_all[bis[a]], hidx[a]) for a in heads]
    gc_col = [col_of(gc_all[bis[a]], hidx[a] + n_heads) for a in heads]
    gl_col = [col_of(gl_all[bis[a]], hidx[a] + n_heads) for a in heads]
    gc_row = [gct_ref[bis[a], pl.ds(hidx[a] + n_heads, 1), :] for a in heads]
    decay = [jnp.where(tri_incl, jnp.exp(jnp.where(tri_incl, gc_col[a] - gc_row[a], 0.0)), 0.0) for a in heads]
    kb = [k[a] * beta_col[a] for a in heads]
    vb = [vf[bis[a]][:, sls[a]] * beta_col[a] for a in heads]
    k16 = [x.astype(BF16) for x in k]
    kk = [lax.dot_general(kb[a].astype(BF16), k16[a], NT_DIMS, preferred_element_type=F32) for a in heads]
    qk = [lax.dot_general(q[a].astype(BF16), k16[a], NT_DIMS, preferred_element_type=F32) for a in heads]
    lower = [jnp.where(tri_strict, kk[a] * decay[a], 0.0) for a in heads]
    attn = [jnp.where(tri_incl, qk[a] * decay[a], 0.0).astype(BF16) for a in heads]

    tinv = [eye - jnp.where(pair_masks[1], lower[a], 0.0) for a in heads]
    s = 2
    while s < C:
        t16 = [x.astype(BF16) for x in tinv]
        md = [jnp.dot(jnp.where(pair_masks[s], lower[a], 0.0).astype(BF16), t16[a], preferred_element_type=F32)
              for a in heads]
        tinv = [tinv[a] - jnp.dot(t16[a], md[a].astype(BF16), preferred_element_type=F32) for a in heads]
        s *= 2

    egc = [jnp.exp(x) for x in gc_col]
    uw = [jnp.dot(tinv[a].astype(BF16), jnp.concatenate([vb[a], kb[a] * egc[a]], axis=1).astype(BF16),
                  preferred_element_type=F32) for a in heads]
    q_dec = [q[a] * egc[a] for a in heads]
    k_dec_t = [(k[a] * jnp.exp(gl_col[a] - gc_col[a])).T for a in heads]
    egl = [jnp.exp(x) for x in gl_col]

    vnew_ref[...] = jnp.zeros_like(vnew_ref)
    outs = [[] for _ in heads]
    for c in range(G // C):
        r0 = c * C
        st = [state_ref[a] for a in heads]
        ws_qs = [jnp.dot(jnp.concatenate([uw[a][r0:r0 + C, D:], q_dec[a][r0:r0 + C]], axis=0).astype(BF16),
                         st[a].astype(BF16), preferred_element_type=F32) for a in heads]
        for a in heads:
            vnew_ref[a, r0:r0 + C, :] = (uw[a][r0:r0 + C, :D] - ws_qs[a][:C]).astype(BF16)
        vn = [vnew_ref[a] for a in heads]
        for a in heads:
            outs[a].append(ws_qs[a][C:] + jnp.dot(attn[a][r0:r0 + C], vn[a], preferred_element_type=F32))
        for a in heads:
            kt_c = jnp.where(ct == c, k_dec_t[a], 0.0).astype(BF16)
            e_c = jnp.concatenate([egl[a][r0:r0 + C]] * (D // C), axis=0)
            state_ref[a] = st[a] * e_c + jnp.dot(kt_c, vn[a], preferred_element_type=F32)

    for a in heads:
        o = jnp.concatenate(outs[a], axis=0)
        o = o * lax.rsqrt(jnp.mean(o * o, axis=-1, keepdims=True) + NORM_EPS) * onw_ref[...]
        z = z_ref[bis[a], :, sls[a]].astype(F32)
        o_ref[bis[a], :, sls[a]] = (o * _silu(z)).astype(o_ref.dtype)


def _deltanet(qkv3, z3, small3, a_log_row, dt_row, onorm_w, n_heads, col_q, col_k, col_v, col_z, heads_per_step=8,
              batch_rows_per_step=2):
    b, s, _ = qkv3.shape
    G, D, NH = DN_GROUP, LANES, heads_per_step
    NB = batch_rows_per_step if b % batch_rows_per_step == 0 else 1
    assert n_heads % NH == 0 and all(c % NH == 0 for c in (col_q, col_k, col_v, col_z))
    blk = lambda off: pl.BlockSpec((NB, G, NH * D), lambda bi, hi, gi: (bi, gi, off // NH + hi))
    row_spec = pl.BlockSpec((1, LANES), lambda bi, hi, gi: (0, 0))
    return pl.pallas_call(
        functools.partial(_deltanet_kernel, n_heads=n_heads, heads_per_step=NH),
        grid=(b // NB, n_heads // NH, s // G),
        in_specs=[blk(col_q), blk(col_k), blk(col_v), blk(col_z),
                  pl.BlockSpec((NB, G, LANES), lambda bi, hi, gi: (bi, gi, 0)),
                  row_spec, row_spec, row_spec],
        out_specs=pl.BlockSpec((NB, G, NH * D), lambda bi, hi, gi: (bi, gi, hi)),
        out_shape=jax.ShapeDtypeStruct((b, s, n_heads * D), BF16),
        scratch_shapes=[pltpu.VMEM((NB * NH, D, D), F32),
                        pltpu.VMEM((NB * NH, G, D), BF16), pltpu.VMEM((NB, LANES, G), F32)],
        compiler_params=_cparams(("parallel", "parallel", "arbitrary")),
        name="gated_deltanet",
    )(qkv3, qkv3, qkv3, z3, small3, a_log_row, dt_row, onorm_w)


def _t5_bucket_f32(dist):
    dist = jnp.maximum(dist, 0)
    max_exact = REL_BUCKETS // 2
    d = jnp.maximum(dist, 1).astype(F32)
    large = max_exact + (jnp.log(d / max_exact) / math.log(REL_MAX_DIST / max_exact)
                         * (REL_BUCKETS - max_exact)).astype(jnp.int32)
    large = jnp.minimum(large, REL_BUCKETS - 1)
    return jnp.where(dist < max_exact, dist, large)


def _n_near_tiles():
    return -(-(REL_MAX_DIST + MOBA_BLOCK - 1) // MOBA_BLOCK)


def _moba_kernel(rb_ref, q_ref, k_ref, v_ref, z_ref, o_ref, kmean_ref, bias_ref, kt_ref, xt_ref, qa_ref,
                 s0_ref, s1_ref, wide_ref, *, n_near, nbp, group, qblocks):
    hh = pl.program_id(0)
    bb = pl.program_id(1)
    BLK, D, U, QB = MOBA_BLOCK, LANES, group, qblocks
    R = QB * BLK
    nb = k_ref.shape[1] // BLK
    n_steps = nb // QB
    col_shift = nbp

    @pl.when(bb == 0)
    def _():
        qr = lax.broadcasted_iota(jnp.int32, (BLK, BLK), 0)
        kc = lax.broadcasted_iota(jnp.int32, (BLK, BLK), 1)
        for t in range(n_near):
            dist = t * BLK + qr - kc
            bucket = _t5_bucket_f32(dist)
            val = jnp.full((BLK, BLK), rb_ref[0, hh], F32)
            for bk in range(1, REL_BUCKETS):
                val = jnp.where(bucket == bk, rb_ref[bk, hh], val)
            if t == 0:
                val = jnp.where(dist >= 0, val, NEG_INF)
            bias_ref[t] = val
        bias_ref[n_near] = jnp.full((BLK, BLK), rb_ref[REL_BUCKETS - 1, hh], F32)
        ri = lax.broadcasted_iota(jnp.int32, (D, U * BLK), 0)
        ci = lax.broadcasted_iota(jnp.int32, (D, U * BLK), 1)
        for g in range(nb // U):
            hot = (ri == g * U + ci // BLK) | (ri == col_shift)
            kt_ref[g, D:2 * D, :] = jnp.where(hot, 1.0, 0.0).astype(BF16)

    ones_dd = jnp.ones((D, D), BF16)

    def norms(it, carry):
        k2, q2 = carry
        for u in range(U):
            j = it * U + u
            rows = pl.ds(pl.multiple_of(j * BLK, BLK), BLK)
            kj = k_ref[0, rows, :].astype(F32)
            qj = q_ref[0, rows, :].astype(F32)
            kmean_ref[pl.ds(j, 1), :] = jnp.mean(kj, axis=0, keepdims=True)
            kt_ref[it, 0:D, u * BLK:(u + 1) * BLK] = kj.T.astype(BF16)
            k2 = jnp.maximum(k2, jnp.dot((kj * kj).astype(BF16), ones_dd, preferred_element_type=F32))
            q2 = jnp.maximum(q2, jnp.dot((qj * qj).astype(BF16), ones_dd, preferred_element_type=F32))
        return k2, q2
    k2, q2 = lax.fori_loop(0, nb // U, norms, (jnp.zeros((BLK, D), F32), jnp.zeros((BLK, D), F32)))
    rb_abs = jnp.abs(rb_ref[0, hh])
    for bk in range(1, REL_BUCKETS):
        rb_abs = jnp.maximum(rb_abs, jnp.abs(rb_ref[bk, hh]))
    bound = jnp.sqrt(jnp.max(q2) * jnp.max(k2)) * 1.02 + rb_abs + 1e-3
    wide_ref[0] = jnp.where(bound > MOBA_SHIFT_SLACK, 1, 0).astype(jnp.int32)

    kmean16 = kmean_ref[...].astype(BF16)
    jrow = lax.broadcasted_iota(jnp.int32, (nb, BLK), 0)
    jrow_f = jrow.astype(F32)

    def visibility(it, carry):
        for u in range(U):
            ib = it * U + u
            rows = pl.ds(pl.multiple_of(ib * BLK, BLK), BLK)
            gate = lax.dot_general(kmean16, q_ref[0, rows, :], NT_DIMS, preferred_element_type=F32)
            past = jrow < ib
            sc = jnp.where(past, gate, NEG_INF)
            visible = jrow == ib
            for _ in range(MOBA_TOPK):
                top = jnp.max(sc, axis=0, keepdims=True)
                first = jnp.min(jnp.where(sc == top, jrow_f, float(nb)), axis=0, keepdims=True)
                pick = jrow_f == first
                visible = visible | (pick & past)
                sc = jnp.where(pick, jnp.finfo(F32).min, sc)
            cols = jnp.concatenate([jnp.where(visible, 0.0, NEG_INF), jnp.zeros((D - nb, BLK), F32)], axis=0)
            xt_ref[rows, :] = cols.T.astype(BF16)
        return carry
    lax.fori_loop(0, nb // U, visibility, 0)

    def set_query_operand(s, slot):
        rows = pl.ds(pl.multiple_of(s * R, R), R)
        qa_ref[slot, :, 0:D] = q_ref[0, rows, :]
        qa_ref[slot, :, D:2 * D] = xt_ref[rows, :]

    def pair_scores(s, g, slot):
        sc = jnp.dot(qa_ref[slot], kt_ref[g], preferred_element_type=F32)
        bias = jnp.concatenate(
            [jnp.concatenate([bias_ref[jnp.clip(s * QB + a - (g * U + u), 0, n_near)] for u in range(U)], axis=1)
             for a in range(QB)], axis=0)
        return sc + bias

    def lane_chunks(x):
        return [x[:, c * D:(c + 1) * D] for c in range(x.shape[1] // D)]

    @pl.when(wide_ref[0] == 1)
    def _():
        def step_max(s, carry):
            set_query_operand(s, 0)

            def max_step(g, mm):
                for c in lane_chunks(pair_scores(s, g, 0)):
                    mm = jnp.maximum(mm, c)
                return mm
            mm = lax.fori_loop(0, s + 1, max_step, jnp.full((R, D), jnp.finfo(F32).min, F32))
            m = jnp.max(mm, axis=1, keepdims=True)
            rows = pl.ds(pl.multiple_of(s * R, R), R)
            lane_q = lax.broadcasted_iota(jnp.int32, (R, D), 1)
            xt_ref[rows, :] = jnp.where(lane_q == col_shift, -m, xt_ref[rows, :].astype(F32)).astype(BF16)
            return carry
        lax.fori_loop(0, n_steps, step_max, 0)

    ones_cols = jnp.ones((U * BLK, D), BF16)

    def value_operand(g):
        vg = v_ref[0, pl.ds(pl.multiple_of(g * (U * BLK), U * BLK), U * BLK), :]
        return jnp.concatenate([vg, ones_cols], axis=1)

    def consume(vg, sc, acc):
        return acc + jnp.dot(jnp.exp(sc).astype(BF16), vg, preferred_element_type=F32)

    def finish_step(s, g, acc):
        @pl.when(g == s)
        def _():
            rows = pl.ds(pl.multiple_of(s * R, R), R)
            out = acc[:, 0:D] / acc[:, D:D + 1]
            z = z_ref[0, rows, :].astype(F32)
            o_ref[0, rows, :] = (out * _silu(z)).astype(o_ref.dtype)
        return acc * jnp.where(g == s, 0.0, 1.0).astype(F32)

    def stage(cur_ref, nxt_ref, carry):
        s, g, acc = carry
        wrap = g == s
        s2 = jnp.where(wrap, s + 1, s)
        g2 = jnp.where(wrap, 0, g + 1)

        @pl.when(wrap)
        def _():
            set_query_operand(s2, s2 & 1)

        nxt_ref[...] = pair_scores(s2, g2, s2 & 1)
        acc = consume(value_operand(g), cur_ref[...], acc)
        return s2, g2, finish_step(s, g, acc)

    def two_stages(t, carry):
        return stage(s1_ref, s0_ref, stage(s0_ref, s1_ref, carry))

    n_pairs = n_steps * (n_steps + 1) // 2
    set_query_operand(0, 0)
    s0_ref[...] = pair_scores(0, 0, 0)
    carry = (jnp.int32(0), jnp.int32(0), jnp.zeros((R, 2 * D), F32))
    carry = lax.fori_loop(0, (n_pairs - 1) // 2, two_stages, carry)
    last_ref = s0_ref
    if (n_pairs - 1) % 2:
        carry = stage(s0_ref, s1_ref, carry)
        last_ref = s1_ref
    s, g, acc = carry
    finish_step(s, g, consume(value_operand(g), last_ref[...], acc))


def _moba(rel_bias, p3, n_heads, col_q, col_k, col_v, col_z, group=4, qblocks=4):
    b, s, _ = p3.shape
    BLK, D = MOBA_BLOCK, LANES
    nb = s // BLK
    nbp = -(-nb // 16) * 16
    assert nbp + 16 <= D, "block-visibility columns must fit in the spare contraction columns"
    assert group == qblocks and nb % group == 0, "step s visits key groups 0..s"
    n_near = _n_near_tiles()
    rows = qblocks * BLK
    seq = lambda off: pl.BlockSpec((1, s, D), lambda hi, bi: (bi, 0, off + hi))
    return pl.pallas_call(
        functools.partial(_moba_kernel, n_near=n_near, nbp=nbp, group=group, qblocks=qblocks),
        grid=(n_heads, b),
        in_specs=[pl.BlockSpec(memory_space=pltpu.SMEM), seq(col_q), seq(col_k), seq(col_v), seq(col_z)],
        out_specs=seq(0),
        out_shape=jax.ShapeDtypeStruct((b, s, n_heads * D), BF16),
        scratch_shapes=[pltpu.VMEM((nb, D), F32), pltpu.VMEM((n_near + 1, BLK, BLK), F32),
                        pltpu.VMEM((nb // group, 2 * D, group * BLK), BF16), pltpu.VMEM((s, D), BF16),
                        pltpu.VMEM((2, rows, 2 * D), BF16),
                        pltpu.VMEM((rows, group * BLK), F32), pltpu.VMEM((rows, group * BLK), F32),
                        pltpu.SMEM((1,), jnp.int32)],
        compiler_params=_cparams(("arbitrary", "arbitrary")),
        name="moba_attention",
    )(rel_bias, p3, p3, p3, p3)


def _merge_kernel(ya_ref, yb_ref, g0_ref, g1_ref, x_ref, w0_ref, w1_ref, wo_ref, b0_ref, b1_ref, nw_ref, o_ref):
    pa = jnp.dot(ya_ref[...], w0_ref[...], preferred_element_type=F32)
    pb = jnp.dot(yb_ref[...], w1_ref[...], preferred_element_type=F32)
    g0 = _sigmoid(g0_ref[...].astype(F32) + b0_ref[...])
    g1 = _sigmoid(g1_ref[...].astype(F32) + b1_ref[...])
    merged = (g0 * pa + g1 * pb).astype(BF16)
    out = jnp.dot(merged, wo_ref[...], preferred_element_type=F32)
    y = out * lax.rsqrt(jnp.mean(out * out, axis=-1, keepdims=True) + NORM_EPS) * nw_ref[...]
    o_ref[...] = x_ref[...] + y


def _merge(ya, yb, p2, x2d, w0, w1, wo, b_gate, norm_w, col_g0, col_g1, tm=512):
    t, d = x2d.shape
    row = lambda c: pl.BlockSpec((tm, d), lambda i: (i, c))
    full = pl.BlockSpec((d, d), lambda i: (0, 0))
    vec = lambda c: pl.BlockSpec((1, d), lambda i: (0, c))
    return pl.pallas_call(
        _merge_kernel,
        grid=(t // tm,),
        in_specs=[row(0), row(0), row(col_g0), row(col_g1), row(0), full, full, full, vec(0), vec(1), vec(0)],
        out_specs=row(0),
        out_shape=jax.ShapeDtypeStruct((t, d), F32),
        compiler_params=_cparams(("parallel",)),
        name="merge_out",
    )(ya, yb, p2, p2, x2d, w0, w1, wo, b_gate.reshape(1, -1), b_gate.reshape(1, -1), norm_w.reshape(1, d))


def kernel(x, norm_pre_w, w_in, b_gate, conv_w, dn_a_log, dn_dt_bias, dn_onorm_w, rel_bias, w_branch, w_out,
           norm_post_w):
    b, s, d = x.shape
    n_heads = dn_a_log.shape[0]
    hd = dn_onorm_w.shape[0]
    width = n_heads * hd
    assert hd == LANES and rel_bias.shape[1] == n_heads and width == d
    assert s % MOBA_BLOCK == 0 and s % DN_GROUP == 0 and 2 * n_heads <= LANES
    t = b * s

    c_za = 3 * width
    c_beta = 4 * width
    c_qb = c_beta + 2 * n_heads
    assert conv_w.shape[1] == c_za
    w16 = w_in.astype(BF16)
    w_qkv_a = w16[:, :c_za]
    w_rest = jnp.concatenate([w16[:, c_za:c_beta], w16[:, c_qb:]], axis=1)
    w_small = jnp.pad(w16[:, c_beta:c_qb], ((0, 0), (0, LANES - 2 * n_heads)))
    nblk = width // LANES
    col = lambda k: k * nblk

    x2d = x.reshape(t, d)
    norm_w = norm_pre_w.astype(F32)
    col_scale = jnp.concatenate([jnp.ones((1, width), F32), jnp.full((1, width), hd ** -0.5, F32),
                                 jnp.ones((1, w_rest.shape[1] - 2 * width), F32)], axis=1)
    qkv_a, rest, small = _in_proj(x2d, norm_w, w_qkv_a, w_rest, col_scale, w_small, conv_w.astype(F32), s)
    qkv_a3 = qkv_a.reshape(b, s, -1)
    rest3 = rest.reshape(b, s, -1)

    pad = (0, LANES - 2 * n_heads)
    a_log_row = jnp.pad(jnp.concatenate([jnp.zeros_like(dn_a_log), dn_a_log]), pad).reshape(1, LANES).astype(F32)
    dt_row = jnp.pad(jnp.concatenate([jnp.zeros_like(dn_dt_bias), dn_dt_bias]), pad).reshape(1, LANES).astype(F32)
    ya = _deltanet(qkv_a3, rest3, small.reshape(b, s, LANES), a_log_row, dt_row,
                   dn_onorm_w.reshape(1, hd).astype(F32), n_heads, col(0), col(1), col(2), col(0))
    yb = _moba(rel_bias.astype(F32), rest3, n_heads, col(1), col(2), col(3), col(4))

    out = _merge(ya.reshape(t, d), yb.reshape(t, d), rest, x2d,
                 w_branch[0].astype(BF16), w_branch[1].astype(BF16), w_out.astype(BF16),
                 b_gate.astype(F32), norm_post_w.astype(F32), 5, 6)
    return out.reshape(b, s, d)
```

```python
import functools
import math

import jax
import jax.numpy as jnp
from jax import lax
from jax.experimental import pallas as pl
from jax.experimental.pallas import tpu as pltpu

F32 = jnp.float32
BF16 = jnp.bfloat16

LANES = 128
DN_CHUNK = 64
DN_GROUP = 128
DN_CONV_HALO = 8
MOBA_BLOCK = 256
MOBA_TOPK = 3
MOBA_SHIFT_SLACK = 40.0
REL_BUCKETS = 32
REL_MAX_DIST = 2048
NORM_EPS = 1e-6
NEG_INF = -1e30
VMEM_LIMIT = 56 * 1024 * 1024
NT_DIMS = (((1,), (1,)), ((), ()))


def _cparams(sem):
    return pltpu.CompilerParams(dimension_semantics=sem, vmem_limit_bytes=VMEM_LIMIT)


def _sigmoid(x):
    return 0.5 * jnp.tanh(0.5 * x) + 0.5


def _silu(x):
    h = 0.5 * x
    return h * jnp.tanh(h) + h


def _normalise(x, w):
    y = x * lax.rsqrt(jnp.mean(x * x, axis=-1, keepdims=True) + NORM_EPS)
    return (y * w).astype(BF16)


def _in_proj_kernel(x_ref, xh_ref, nw_ref, wc_ref, wr_ref, cs_ref, ws_ref, cw_ref, oc_ref, or_ref, sm_ref,
                    h_ref, hh_ref, *, tiles_per_seq, row_chunks):
    i = pl.program_id(0)
    tm = oc_ref.shape[0]
    halo = hh_ref.shape[0]

    @pl.when(pl.program_id(1) == 0)
    def _():
        h = _normalise(x_ref[...], nw_ref[...])
        h_ref[...] = h
        sm_ref[...] = jnp.dot(h, ws_ref[...], preferred_element_type=F32)
        hh = _normalise(xh_ref[...], nw_ref[...])
        hh_ref[...] = jnp.where(i % tiles_per_seq == 0, jnp.zeros_like(hh), hh)

    wc = wc_ref[...]
    wr = wr_ref[...]
    cw = cw_ref[...]
    kw = cw.shape[0]
    base = halo - (kw - 1)
    rm = tm // row_chunks
    tail = jnp.dot(hh_ref[...], wc, preferred_element_type=F32)
    for r in range(row_chunks):
        rows = slice(r * rm, (r + 1) * rm)
        hr = h_ref[rows, :]
        acc_c = jnp.dot(hr, wc, preferred_element_type=F32)
        acc_r = jnp.dot(hr, wr, preferred_element_type=F32)
        or_ref[rows, :] = (acc_r * cs_ref[...]).astype(or_ref.dtype)
        ext = jnp.concatenate([tail, acc_c], axis=0)
        y = ext[base:base + rm] * cw[0:1]
        for kk in range(1, kw):
            y = y + ext[base + kk:base + kk + rm] * cw[kk:kk + 1]
        oc_ref[rows, :] = _silu(y).astype(oc_ref.dtype)
        tail = acc_c[rm - halo:rm]


def _in_proj(x2d, norm_w, w_conv, w_rest, col_scale, w_small, conv_w, seq_len, tm=1024, steps=4, halo=16,
             row_chunks=4):
    t, d = x2d.shape
    nc, nr, ns = w_conv.shape[1], w_rest.shape[1], w_small.shape[1]
    kw = conv_w.shape[0]
    assert nc % (steps * LANES) == 0 and nr % (steps * LANES) == 0
    assert seq_len % tm == 0 and tm % halo == 0 and kw - 1 <= halo and conv_w.shape[1] == nc
    tc, tr = nc // steps, nr // steps
    row = lambda w: pl.BlockSpec((tm, w), lambda i, j: (i, j))
    col = lambda r, w: pl.BlockSpec((r, w), lambda i, j: (0, j))
    fixed = lambda r, w: pl.BlockSpec((r, w), lambda i, j: (0, 0))
    return pl.pallas_call(
        functools.partial(_in_proj_kernel, tiles_per_seq=seq_len // tm, row_chunks=row_chunks),
        grid=(t // tm, steps),
        in_specs=[pl.BlockSpec((tm, d), lambda i, j: (i, 0)),
                  pl.BlockSpec((halo, d), lambda i, j: (jnp.maximum(i * (tm // halo) - 1, 0), 0)),
                  fixed(1, d), col(d, tc), col(d, tr), col(1, tr), fixed(d, ns), col(kw, tc)],
        out_specs=[row(tc), row(tr), pl.BlockSpec((tm, ns), lambda i, j: (i, 0))],
        out_shape=[jax.ShapeDtypeStruct((t, nc), BF16), jax.ShapeDtypeStruct((t, nr), BF16),
                   jax.ShapeDtypeStruct((t, ns), F32)],
        scratch_shapes=[pltpu.VMEM((tm, d), BF16), pltpu.VMEM((halo, d), BF16)],
        compiler_params=_cparams(("parallel", "arbitrary")),
        name="in_proj",
    )(x2d, x2d, norm_w.reshape(1, d), w_conv, w_rest, col_scale, w_small, conv_w)


def _deltanet_kernel(q_ref, k_ref, v_ref, z_ref, sm_ref, alog_ref, dtb_ref, onw_ref,
                     o_ref, state_ref, vnew_ref, gct_ref, *, n_heads, heads_per_step):
    hg = pl.program_id(1)
    g = pl.program_id(2)
    G, C, D, NH = DN_GROUP, DN_CHUNK, LANES, heads_per_step
    lc = C.bit_length() - 1

    @pl.when(g == 0)
    def _():
        state_ref[...] = jnp.zeros_like(state_ref)

    def l2n(x):
        return x * lax.rsqrt(jnp.sum(x * x, axis=-1, keepdims=True) + NORM_EPS)

    NB = q_ref.shape[0]
    qf = [q_ref[bi].astype(F32) for bi in range(NB)]
    kf = [k_ref[bi].astype(F32) for bi in range(NB)]
    vf = [v_ref[bi].astype(F32) for bi in range(NB)]

    lane = lax.broadcasted_iota(jnp.int32, (G, LANES), 1)
    row = lax.broadcasted_iota(jnp.int32, (G, LANES), 0)
    pos = row & (C - 1)
    beta_all, gc_all, gl_all = [], [], []
    for bi in range(NB):
        cs = sm_ref[bi]
        beta_all.append(_sigmoid(cs))
        xg = cs + dtb_ref[...]
        softplus = jnp.maximum(xg, 0.0) + jnp.log1p(jnp.exp(-jnp.abs(xg)))
        gc = -jnp.exp(alog_ref[...]) * softplus
        shift = 1
        while shift < C:
            gc = gc + jnp.where(pos >= shift, pltpu.roll(gc, shift, 0), 0.0)
            shift *= 2
        gc_all.append(gc)
        gl_all.append(jnp.concatenate([jnp.broadcast_to(gc[c * C + C - 1:c * C + C], (C, LANES))
                                       for c in range(G // C)], axis=0))
        gct_ref[bi] = gc.T

    ri = lax.broadcasted_iota(jnp.int32, (G, G), 0)
    ci = lax.broadcasted_iota(jnp.int32, (G, G), 1)
    same = (ri >> lc) == (ci >> lc)
    tri_incl = same & (ri >= ci)
    tri_strict = same & (ri > ci)
    eye = jnp.where(ri == ci, 1.0, 0.0)
    ct = lax.broadcasted_iota(jnp.int32, (D, G), 1) >> lc

    def pair_mask(s):
        ls = s.bit_length() - 1
        return ((ri >> (ls + 1)) == (ci >> (ls + 1))) & (((ri >> ls) & 1) == 1) & (((ci >> ls) & 1) == 0)

    pair_masks = {}
    s = 1
    while s < C:
        pair_masks[s] = pair_mask(s)
        s *= 2

    heads = range(NB * NH)
    bis = [a // NH for a in heads]
    sls = [slice((a % NH) * D, (a % NH + 1) * D) for a in heads]
    hidx = [hg * NH + a % NH for a in heads]

    def col_of(x, lane_idx):
        return jnp.sum(jnp.where(lane == lane_idx, x, 0.0), axis=1, keepdims=True)

    q = [l2n(qf[bis[a]][:, sls[a]]) * (D ** -0.5) for a in heads]
    k = [l2n(kf[bis[a]][:, sls[a]]) for a in heads]
    beta_col = [col_of(beta_all[bis[a]], hidx[a]) for a in heads]
    gc_col = [col_of(gc_all[bis[a]], hidx[a] + n_heads) for a in heads]
    gl_col = [col_of(gl_all[bis[a]], hidx[a] + n_heads) for a in heads]
    gc_row = [gct_ref[bis[a], pl.ds(hidx[a] + n_heads, 1), :] for a in heads]
    decay = [jnp.where(tri_incl, jnp.exp(jnp.where(tri_incl, gc_col[a] - gc_row[a], 0.0)), 0.0) for a in heads]
    kb = [k[a] * beta_col[a] for a in heads]
    vb = [vf[bis[a]][:, sls[a]] * beta_col[a] for a in heads]
    k_t = [x.T for x in k]
    k_t16 = [x.astype(BF16) for x in k_t]
    kk = [jnp.dot(kb[a].astype(BF16), k_t16[a], preferred_element_type=F32) for a in heads]
    qk = [jnp.dot(q[a].astype(BF16), k_t16[a], preferred_element_type=F32) for a in heads]
    lower = [jnp.where(tri_strict, kk[a] * decay[a], 0.0) for a in heads]
    attn = [jnp.where(tri_incl, qk[a] * decay[a], 0.0).astype(BF16) for a in heads]

    tinv = [eye - jnp.where(pair_masks[1], lower[a], 0.0) for a in heads]
    s = 2
    while s < C:
        t16 = [x.astype(BF16) for x in tinv]
        md = [jnp.dot(jnp.where(pair_masks[s], lower[a], 0.0).astype(BF16), t16[a], preferred_element_type=F32)
              for a in heads]
        tinv = [tinv[a] - jnp.dot(t16[a], md[a].astype(BF16), preferred_element_type=F32) for a in heads]
        s *= 2

    egc = [jnp.exp(x) for x in gc_col]
    uw = [jnp.dot(tinv[a].astype(BF16), jnp.concatenate([vb[a], kb[a] * egc[a]], axis=1).astype(BF16),
                  preferred_element_type=F32) for a in heads]
    q_dec = [q[a] * egc[a] for a in heads]
    gl_row = [jnp.concatenate([jnp.broadcast_to(gc_row[a][:, c * C + C - 1:c * C + C], (1, C)) for c in range(G // C)],
                              axis=1) for a in heads]
    k_dec_t = [k_t[a] * jnp.exp(gl_row[a] - gc_row[a]) for a in heads]
    egl = [jnp.exp(x) for x in gl_col]

    vnew_ref[...] = jnp.zeros_like(vnew_ref)
    outs = [[] for _ in heads]
    for c in range(G // C):
        r0 = c * C
        st = [state_ref[a] for a in heads]
        ws_qs = [jnp.dot(jnp.concatenate([uw[a][r0:r0 + C, D:], q_dec[a][r0:r0 + C]], axis=0).astype(BF16),
                         st[a].astype(BF16), preferred_element_type=F32) for a in heads]
        for a in heads:
            vnew_ref[a, r0:r0 + C, :] = (uw[a][r0:r0 + C, :D] - ws_qs[a][:C]).astype(BF16)
        vn = [vnew_ref[a] for a in heads]
        for a in heads:
            outs[a].append(ws_qs[a][C:] + jnp.dot(attn[a][r0:r0 + C], vn[a], preferred_element_type=F32))
        for a in heads:
            kt_c = jnp.where(ct == c, k_dec_t[a], 0.0).astype(BF16)
            e_c = jnp.concatenate([egl[a][r0:r0 + C]] * (D // C), axis=0)
            state_ref[a] = st[a] * e_c + jnp.dot(kt_c, vn[a], preferred_element_type=F32)

    for a in heads:
        o = jnp.concatenate(outs[a], axis=0)
        o = o * lax.rsqrt(jnp.mean(o * o, axis=-1, keepdims=True) + NORM_EPS) * onw_ref[...]
        z = z_ref[bis[a], :, sls[a]].astype(F32)
        o_ref[bis[a], :, sls[a]] = (o * _silu(z)).astype(o_ref.dtype)


def _deltanet(qkv3, z3, small3, a_log_row, dt_row, onorm_w, n_heads, col_q, col_k, col_v, col_z, heads_per_step=8,
              batch_rows_per_step=2):
    b, s, _ = qkv3.shape
    G, D, NH = DN_GROUP, LANES, heads_per_step
    NB = batch_rows_per_step if b % batch_rows_per_step == 0 else 1
    assert n_heads % NH == 0 and all(c % NH == 0 for c in (col_q, col_k, col_v, col_z))
    blk = lambda off: pl.BlockSpec((NB, G, NH * D), lambda bi, hi, gi: (bi, gi, off // NH + hi))
    row_spec = pl.BlockSpec((1, LANES), lambda bi, hi, gi: (0, 0))
    return pl.pallas_call(
        functools.partial(_deltanet_kernel, n_heads=n_heads, heads_per_step=NH),
        grid=(b // NB, n_heads // NH, s // G),
        in_specs=[blk(col_q), blk(col_k), blk(col_v), blk(col_z),
                  pl.BlockSpec((NB, G, LANES), lambda bi, hi, gi: (bi, gi, 0)),
                  row_spec, row_spec, row_spec],
        out_specs=pl.BlockSpec((NB, G, NH * D), lambda bi, hi, gi: (bi, gi, hi)),
        out_shape=jax.ShapeDtypeStruct((b, s, n_heads * D), BF16),
        scratch_shapes=[pltpu.VMEM((NB * NH, D, D), F32),
                        pltpu.VMEM((NB * NH, G, D), BF16), pltpu.VMEM((NB, LANES, G), F32)],
        compiler_params=_cparams(("parallel", "parallel", "arbitrary")),
        name="gated_deltanet",
    )(qkv3, qkv3, qkv3, z3, small3, a_log_row, dt_row, onorm_w)


def _t5_bucket_f32(dist):
    dist = jnp.maximum(dist, 0)
    max_exact = REL_BUCKETS // 2
    d = jnp.maximum(dist, 1).astype(F32)
    large = max_exact + (jnp.log(d / max_exact) / math.log(REL_MAX_DIST / max_exact)
                         * (REL_BUCKETS - max_exact)).astype(jnp.int32)
    large = jnp.minimum(large, REL_BUCKETS - 1)
    return jnp.where(dist < max_exact, dist, large)


def _n_near_tiles():
    return -(-(REL_MAX_DIST + MOBA_BLOCK - 1) // MOBA_BLOCK)


def _moba_kernel(rb_ref, q_ref, k_ref, v_ref, z_ref, o_ref, kmean_ref, bias_ref, kt_ref, xt_ref, qa_ref,
                 s0_ref, s1_ref, wide_ref, *, n_near, nbp, group, qblocks):
    hh = pl.program_id(0)
    bb = pl.program_id(1)
    BLK, D, U, QB = MOBA_BLOCK, LANES, group, qblocks
    R = QB * BLK
    nb = k_ref.shape[1] // BLK
    n_steps = nb // QB
    col_shift = nbp

    @pl.when(bb == 0)
    def _():
        qr = lax.broadcasted_iota(jnp.int32, (BLK, BLK), 0)
        kc = lax.broadcasted_iota(jnp.int32, (BLK, BLK), 1)
        for t in range(n_near):
            dist = t * BLK + qr - kc
            bucket = _t5_bucket_f32(dist)
            val = jnp.full((BLK, BLK), rb_ref[0, hh], F32)
            for bk in range(1, REL_BUCKETS):
                val = jnp.where(bucket == bk, rb_ref[bk, hh], val)
            if t == 0:
                val = jnp.where(dist >= 0, val, NEG_INF)
            bias_ref[t] = val
        bias_ref[n_near] = jnp.full((BLK, BLK), rb_ref[REL_BUCKETS - 1, hh], F32)
        ri = lax.broadcasted_iota(jnp.int32, (D, U * BLK), 0)
        ci = lax.broadcasted_iota(jnp.int32, (D, U * BLK), 1)
        for g in range(nb // U):
            hot = (ri == g * U + ci // BLK) | (ri == col_shift)
            kt_ref[g, D:2 * D, :] = jnp.where(hot, 1.0, 0.0).astype(BF16)

    ones_dd = jnp.ones((D, D), BF16)

    def norms(it, carry):
        k2, q2 = carry
        for u in range(U):
            j = it * U + u
            rows = pl.ds(pl.multiple_of(j * BLK, BLK), BLK)
            kj = k_ref[0, rows, :].astype(F32)
            qj = q_ref[0, rows, :].astype(F32)
            kmean_ref[pl.ds(j, 1), :] = jnp.mean(kj, axis=0, keepdims=True)
            kt_ref[it, 0:D, u * BLK:(u + 1) * BLK] = kj.T.astype(BF16)
            k2 = jnp.maximum(k2, jnp.dot((kj * kj).astype(BF16), ones_dd, preferred_element_type=F32))
            q2 = jnp.maximum(q2, jnp.dot((qj * qj).astype(BF16), ones_dd, preferred_element_type=F32))
        return k2, q2
    k2, q2 = lax.fori_loop(0, nb // U, norms, (jnp.zeros((BLK, D), F32), jnp.zeros((BLK, D), F32)))
    rb_abs = jnp.abs(rb_ref[0, hh])
    for bk in range(1, REL_BUCKETS):
        rb_abs = jnp.maximum(rb_abs, jnp.abs(rb_ref[bk, hh]))
    bound = jnp.sqrt(jnp.max(q2) * jnp.max(k2)) * 1.02 + rb_abs + 1e-3
    wide_ref[0] = jnp.where(bound > MOBA_SHIFT_SLACK, 1, 0).astype(jnp.int32)

    kmean16 = kmean_ref[...].astype(BF16)
    jrow = lax.broadcasted_iota(jnp.int32, (nb, BLK), 0)
    jrow_f = jrow.astype(F32)

    def visibility(it, carry):
        for u in range(U):
            ib = it * U + u
            rows = pl.ds(pl.multiple_of(ib * BLK, BLK), BLK)
            gate = lax.dot_general(kmean16, q_ref[0, rows, :], NT_DIMS, preferred_element_type=F32)
            past = jrow < ib
            sc = jnp.where(past, gate, NEG_INF)
            visible = jrow == ib
            for _ in range(MOBA_TOPK):
                top = jnp.max(sc, axis=0, keepdims=True)
                first = jnp.min(jnp.where(sc == top, jrow_f, float(nb)), axis=0, keepdims=True)
                pick = jrow_f == first
                visible = visible | (pick & past)
                sc = jnp.where(pick, jnp.finfo(F32).min, sc)
            cols = jnp.concatenate([jnp.where(visible, 0.0, NEG_INF), jnp.zeros((D - nb, BLK), F32)], axis=0)
            xt_ref[rows, :] = cols.T.astype(BF16)
        return carry
    lax.fori_loop(0, nb // U, visibility, 0)

    def set_query_operand(s, slot):
        rows = pl.ds(pl.multiple_of(s * R, R), R)
        qa_ref[slot, :, 0:D] = q_ref[0, rows, :]
        qa_ref[slot, :, D:2 * D] = xt_ref[rows, :]

    def pair_scores(s, g, slot):
        sc = jnp.dot(qa_ref[slot], kt_ref[g], preferred_element_type=F32)
        base = s * QB - g * U
        tiles = {d: bias_ref[jnp.clip(base + d, 0, n_near)] for d in range(1 - U, QB)}
        bias = jnp.concatenate([jnp.concatenate([tiles[a - u] for u in range(U)], axis=1) for a in range(QB)], axis=0)
        return sc + bias

    def lane_chunks(x):
        return [x[:, c * D:(c + 1) * D] for c in range(x.shape[1] // D)]

    @pl.when(wide_ref[0] == 1)
    def _():
        def step_max(s, carry):
            set_query_operand(s, 0)

            def max_step(g, mm):
                for c in lane_chunks(pair_scores(s, g, 0)):
                    mm = jnp.maximum(mm, c)
                return mm
            mm = lax.fori_loop(0, s + 1, max_step, jnp.full((R, D), jnp.finfo(F32).min, F32))
            m = jnp.max(mm, axis=1, keepdims=True)
            rows = pl.ds(pl.multiple_of(s * R, R), R)
            lane_q = lax.broadcasted_iota(jnp.int32, (R, D), 1)
            xt_ref[rows, :] = jnp.where(lane_q == col_shift, -m, xt_ref[rows, :].astype(F32)).astype(BF16)
            return carry
        lax.fori_loop(0, n_steps, step_max, 0)

    ones_cols = jnp.ones((U * BLK, D), BF16)

    def value_operand(g):
        vg = v_ref[0, pl.ds(pl.multiple_of(g * (U * BLK), U * BLK), U * BLK), :]
        return jnp.concatenate([vg, ones_cols], axis=1)

    def consume(vg, sc, acc):
        return acc + jnp.dot(jnp.exp(sc).astype(BF16), vg, preferred_element_type=F32)

    def finish_step(s, g, acc):
        @pl.when(g == s)
        def _():
            rows = pl.ds(pl.multiple_of(s * R, R), R)
            out = acc[:, 0:D] / acc[:, D:D + 1]
            z = z_ref[0, rows, :].astype(F32)
            o_ref[0, rows, :] = (out * _silu(z)).astype(o_ref.dtype)
        return acc * jnp.where(g == s, 0.0, 1.0).astype(F32)

    def stage(cur_ref, nxt_ref, carry):
        s, g, acc = carry
        wrap = g == s
        s2 = jnp.where(wrap, s + 1, s)
        g2 = jnp.where(wrap, 0, g + 1)

        @pl.when(wrap)
        def _():
            set_query_operand(s2, s2 & 1)

        nxt_ref[...] = pair_scores(s2, g2, s2 & 1)
        acc = consume(value_operand(g), cur_ref[...], acc)
        return s2, g2, finish_step(s, g, acc)

    def two_stages(t, carry):
        return stage(s1_ref, s0_ref, stage(s0_ref, s1_ref, carry))

    n_pairs = n_steps * (n_steps + 1) // 2
    set_query_operand(0, 0)
    s0_ref[...] = pair_scores(0, 0, 0)
    carry = (jnp.int32(0), jnp.int32(0), jnp.zeros((R, 2 * D), F32))
    carry = lax.fori_loop(0, (n_pairs - 1) // 2, two_stages, carry)
    last_ref = s0_ref
    if (n_pairs - 1) % 2:
        carry = stage(s0_ref, s1_ref, carry)
        last_ref = s1_ref
    s, g, acc = carry
    finish_step(s, g, consume(value_operand(g), last_ref[...], acc))


def _moba(rel_bias, p3, n_heads, col_q, col_k, col_v, col_z, group=4, qblocks=4):
    b, s, _ = p3.shape
    BLK, D = MOBA_BLOCK, LANES
    nb = s // BLK
    nbp = -(-nb // 16) * 16
    assert nbp + 16 <= D, "block-visibility columns must fit in the spare contraction columns"
    assert group == qblocks and nb % group == 0, "step s visits key groups 0..s"
    n_near = _n_near_tiles()
    rows = qblocks * BLK
    seq = lambda off: pl.BlockSpec((1, s, D), lambda hi, bi: (bi, 0, off + hi))
    return pl.pallas_call(
        functools.partial(_moba_kernel, n_near=n_near, nbp=nbp, group=group, qblocks=qblocks),
        grid=(n_heads, b),
        in_specs=[pl.BlockSpec(memory_space=pltpu.SMEM), seq(col_q), seq(col_k), seq(col_v), seq(col_z)],
        out_specs=seq(0),
        out_shape=jax.ShapeDtypeStruct((b, s, n_heads * D), BF16),
        scratch_shapes=[pltpu.VMEM((nb, D), F32), pltpu.VMEM((n_near + 1, BLK, BLK), F32),
                        pltpu.VMEM((nb // group, 2 * D, group * BLK), BF16), pltpu.VMEM((s, D), BF16),
                        pltpu.VMEM((2, rows, 2 * D), BF16),
                        pltpu.VMEM((rows, group * BLK), F32), pltpu.VMEM((rows, group * BLK), F32),
                        pltpu.SMEM((1,), jnp.int32)],
        compiler_params=_cparams(("arbitrary", "arbitrary")),
        name="moba_attention",
    )(rel_bias, p3, p3, p3, p3)


def _merge_kernel(ya_ref, yb_ref, g0_ref, g1_ref, x_ref, w0_ref, w1_ref, wo_ref, b0_ref, b1_ref, nw_ref, o_ref):
    pa = jnp.dot(ya_ref[...], w0_ref[...], preferred_element_type=F32)
    pb = jnp.dot(yb_ref[...], w1_ref[...], preferred_element_type=F32)
    g0 = _sigmoid(g0_ref[...].astype(F32) + b0_ref[...])
    g1 = _sigmoid(g1_ref[...].astype(F32) + b1_ref[...])
    merged = (g0 * pa + g1 * pb).astype(BF16)
    out = jnp.dot(merged, wo_ref[...], preferred_element_type=F32)
    y = out * lax.rsqrt(jnp.mean(out * out, axis=-1, keepdims=True) + NORM_EPS) * nw_ref[...]
    o_ref[...] = x_ref[...] + y


def _merge(ya, yb, p2, x2d, w0, w1, wo, b_gate, norm_w, col_g0, col_g1, tm=512):
    t, d = x2d.shape
    row = lambda c: pl.BlockSpec((tm, d), lambda i: (i, c))
    full = pl.BlockSpec((d, d), lambda i: (0, 0))
    vec = lambda c: pl.BlockSpec((1, d), lambda i: (0, c))
    return pl.pallas_call(
        _merge_kernel,
        grid=(t // tm,),
        in_specs=[row(0), row(0), row(col_g0), row(col_g1), row(0), full, full, full, vec(0), vec(1), vec(0)],
        out_specs=row(0),
        out_shape=jax.ShapeDtypeStruct((t, d), F32),
        compiler_params=_cparams(("parallel",)),
        name="merge_out",
    )(ya, yb, p2, p2, x2d, w0, w1, wo, b_gate.reshape(1, -1), b_gate.reshape(1, -1), norm_w.reshape(1, d))


def kernel(x, norm_pre_w, w_in, b_gate, conv_w, dn_a_log, dn_dt_bias, dn_onorm_w, rel_bias, w_branch, w_out,
           norm_post_w):
    b, s, d = x.shape
    n_heads = dn_a_log.shape[0]
    hd = dn_onorm_w.shape[0]
    width = n_heads * hd
    assert hd == LANES and rel_bias.shape[1] == n_heads and width == d
    assert s % MOBA_BLOCK == 0 and s % DN_GROUP == 0 and 2 * n_heads <= LANES
    t = b * s

    c_za = 3 * width
    c_beta = 4 * width
    c_qb = c_beta + 2 * n_heads
    assert conv_w.shape[1] == c_za
    w16 = w_in.astype(BF16)
    w_qkv_a = w16[:, :c_za]
    w_rest = jnp.concatenate([w16[:, c_za:c_beta], w16[:, c_qb:]], axis=1)
    w_small = jnp.pad(w16[:, c_beta:c_qb], ((0, 0), (0, LANES - 2 * n_heads)))
    nblk = width // LANES
    col = lambda k: k * nblk

    x2d = x.reshape(t, d)
    norm_w = norm_pre_w.astype(F32)
    col_scale = jnp.concatenate([jnp.ones((1, width), F32), jnp.full((1, width), hd ** -0.5, F32),
                                 jnp.ones((1, w_rest.shape[1] - 2 * width), F32)], axis=1)
    qkv_a, rest, small = _in_proj(x2d, norm_w, w_qkv_a, w_rest, col_scale, w_small, conv_w.astype(F32), s)
    qkv_a3 = qkv_a.reshape(b, s, -1)
    rest3 = rest.reshape(b, s, -1)

    pad = (0, LANES - 2 * n_heads)
    a_log_row = jnp.pad(jnp.concatenate([jnp.zeros_like(dn_a_log), dn_a_log]), pad).reshape(1, LANES).astype(F32)
    dt_row = jnp.pad(jnp.concatenate([jnp.zeros_like(dn_dt_bias), dn_dt_bias]), pad).reshape(1, LANES).astype(F32)
    ya = _deltanet(qkv_a3, rest3, small.reshape(b, s, LANES), a_log_row, dt_row,
                   dn_onorm_w.reshape(1, hd).astype(F32), n_heads, col(0), col(1), col(2), col(0))
    yb = _moba(rel_bias.astype(F32), rest3, n_heads, col(1), col(2), col(3), col(4))

    out = _merge(ya.reshape(t, d), yb.reshape(t, d), rest, x2d,
                 w_branch[0].astype(BF16), w_branch[1].astype(BF16), w_out.astype(BF16),
                 b_gate.astype(F32), norm_post_w.astype(F32), 5, 6)
    return out.reshape(b, s, d)
```

```python
import functools
import math

import jax
import jax.numpy as jnp
from jax import lax
from jax.experimental import pallas as pl
from jax.experimental.pallas import tpu as pltpu

F32 = jnp.float32
BF16 = jnp.bfloat16

LANES = 128
DN_CHUNK = 64
DN_GROUP = 128
DN_CONV_HALO = 8
MOBA_BLOCK = 256
MOBA_TOPK = 3
MOBA_SHIFT_SLACK = 40.0
REL_BUCKETS = 32
REL_MAX_DIST = 2048
NORM_EPS = 1e-6
NEG_INF = -1e30
VMEM_LIMIT = 56 * 1024 * 1024
NT_DIMS = (((1,), (1,)), ((), ()))


def _cparams(sem):
    return pltpu.CompilerParams(dimension_semantics=sem, vmem_limit_bytes=VMEM_LIMIT)


def _sigmoid(x):
    return 0.5 * jnp.tanh(0.5 * x) + 0.5


def _silu(x):
    h = 0.5 * x
    return h * jnp.tanh(h) + h


def _normalise(x, w):
    y = x * lax.rsqrt(jnp.mean(x * x, axis=-1, keepdims=True) + NORM_EPS)
    return (y * w).astype(BF16)


def _in_proj_kernel(x_ref, xh_ref, nw_ref, wc_ref, wr_ref, cs_ref, ws_ref, cw_ref, oc_ref, or_ref, sm_ref,
                    h_ref, hh_ref, *, tiles_per_seq, row_chunks):
    i = pl.program_id(0)
    tm = oc_ref.shape[0]
    halo = hh_ref.shape[0]

    @pl.when(pl.program_id(1) == 0)
    def _():
        h = _normalise(x_ref[...], nw_ref[...])
        h_ref[...] = h
        sm_ref[...] = jnp.dot(h, ws_ref[...], preferred_element_type=F32)
        hh = _normalise(xh_ref[...], nw_ref[...])
        hh_ref[...] = jnp.where(i % tiles_per_seq == 0, jnp.zeros_like(hh), hh)

    wc = wc_ref[...]
    wr = wr_ref[...]
    cw = cw_ref[...]
    kw = cw.shape[0]
    base = halo - (kw - 1)
    rm = tm // row_chunks
    tail = jnp.dot(hh_ref[...], wc, preferred_element_type=F32)
    for r in range(row_chunks):
        rows = slice(r * rm, (r + 1) * rm)
        hr = h_ref[rows, :]
        acc_c = jnp.dot(hr, wc, preferred_element_type=F32)
        acc_r = jnp.dot(hr, wr, preferred_element_type=F32)
        or_ref[rows, :] = (acc_r * cs_ref[...]).astype(or_ref.dtype)
        ext = jnp.concatenate([tail, acc_c], axis=0)
        y = ext[base:base + rm] * cw[0:1]
        for kk in range(1, kw):
            y = y + ext[base + kk:base + kk + rm] * cw[kk:kk + 1]
        oc_ref[rows, :] = _silu(y).astype(oc_ref.dtype)
        tail = acc_c[rm - halo:rm]


def _in_proj(x2d, norm_w, w_conv, w_rest, col_scale, w_small, conv_w, seq_len, tm=1024, steps=4, halo=16,
             row_chunks=8):
    t, d = x2d.shape
    nc, nr, ns = w_conv.shape[1], w_rest.shape[1], w_small.shape[1]
    kw = conv_w.shape[0]
    assert nc % (steps * LANES) == 0 and nr % (steps * LANES) == 0
    assert seq_len % tm == 0 and tm % halo == 0 and kw - 1 <= halo and conv_w.shape[1] == nc
    tc, tr = nc // steps, nr // steps
    row = lambda w: pl.BlockSpec((tm, w), lambda i, j: (i, j))
    col = lambda r, w: pl.BlockSpec((r, w), lambda i, j: (0, j))
    fixed = lambda r, w: pl.BlockSpec((r, w), lambda i, j: (0, 0))
    return pl.pallas_call(
        functools.partial(_in_proj_kernel, tiles_per_seq=seq_len // tm, row_chunks=row_chunks),
        grid=(t // tm, steps),
        in_specs=[pl.BlockSpec((tm, d), lambda i, j: (i, 0)),
                  pl.BlockSpec((halo, d), lambda i, j: (jnp.maximum(i * (tm // halo) - 1, 0), 0)),
                  fixed(1, d), col(d, tc), col(d, tr), col(1, tr), fixed(d, ns), col(kw, tc)],
        out_specs=[row(tc), row(tr), pl.BlockSpec((tm, ns), lambda i, j: (i, 0))],
        out_shape=[jax.ShapeDtypeStruct((t, nc), BF16), jax.ShapeDtypeStruct((t, nr), BF16),
                   jax.ShapeDtypeStruct((t, ns), F32)],
        scratch_shapes=[pltpu.VMEM((tm, d), BF16), pltpu.VMEM((halo, d), BF16)],
        compiler_params=_cparams(("parallel", "arbitrary")),
        name="in_proj",
    )(x2d, x2d, norm_w.reshape(1, d), w_conv, w_rest, col_scale, w_small, conv_w)


def _deltanet_kernel(q_ref, k_ref, v_ref, z_ref, sm_ref, alog_ref, dtb_ref, onw_ref,
                     o_ref, state_ref, vnew_ref, gct_ref, *, n_heads, heads_per_step):
    hg = pl.program_id(1)
    g = pl.program_id(2)
    G, C, D, NH = DN_GROUP, DN_CHUNK, LANES, heads_per_step
    lc = C.bit_length() - 1

    @pl.when(g == 0)
    def _():
        state_ref[...] = jnp.zeros_like(state_ref)

    def l2n(x):
        return x * lax.rsqrt(jnp.sum(x * x, axis=-1, keepdims=True) + NORM_EPS)

    NB = q_ref.shape[0]
    qf = [q_ref[bi].astype(F32) for bi in range(NB)]
    kf = [k_ref[bi].astype(F32) for bi in range(NB)]
    vf = [v_ref[bi].astype(F32) for bi in range(NB)]

    lane = lax.broadcasted_iota(jnp.int32, (G, LANES), 1)
    row = lax.broadcasted_iota(jnp.int32, (G, LANES), 0)
    pos = row & (C - 1)
    beta_all, gc_all, gl_all = [], [], []
    for bi in range(NB):
        cs = sm_ref[bi]
        beta_all.append(_sigmoid(cs))
        xg = cs + dtb_ref[...]
        softplus = jnp.maximum(xg, 0.0) + jnp.log1p(jnp.exp(-jnp.abs(xg)))
        gc = -jnp.exp(alog_ref[...]) * softplus
        shift = 1
        while shift < C:
            gc = gc + jnp.where(pos >= shift, pltpu.roll(gc, shift, 0), 0.0)
            shift *= 2
        gc_all.append(gc)
        gl_all.append(jnp.concatenate([jnp.broadcast_to(gc[c * C + C - 1:c * C + C], (C, LANES))
                                       for c in range(G // C)], axis=0))
        gct_ref[bi] = gc.T

    ri = lax.broadcasted_iota(jnp.int32, (G, G), 0)
    ci = lax.broadcasted_iota(jnp.int32, (G, G), 1)
    same = (ri >> lc) == (ci >> lc)
    tri_incl = same & (ri >= ci)
    tri_strict = same & (ri > ci)
    eye = jnp.where(ri == ci, 1.0, 0.0)
    ct = lax.broadcasted_iota(jnp.int32, (D, G), 1) >> lc

    def pair_mask(s):
        ls = s.bit_length() - 1
        return ((ri >> (ls + 1)) == (ci >> (ls + 1))) & (((ri >> ls) & 1) == 1) & (((ci >> ls) & 1) == 0)

    pair_masks = {}
    s = 1
    while s < C:
        pair_masks[s] = pair_mask(s)
        s *= 2

    heads = range(NB * NH)
    bis = [a // NH for a in heads]
    sls = [slice((a % NH) * D, (a % NH + 1) * D) for a in heads]
    hidx = [hg * NH + a % NH for a in heads]

    def col_of(x, lane_idx):
        return jnp.sum(jnp.where(lane == lane_idx, x, 0.0), axis=1, keepdims=True)

    q = [l2n(qf[bis[a]][:, sls[a]]) * (D ** -0.5) for a in heads]
    k = [l2n(kf[bis[a]][:, sls[a]]) for a in heads]
    beta_col = [col_of(beta_all[bis[a]], hidx[a]) for a in heads]
    gc_col = [col_of(gc_all[bis[a]], hidx[a] + n_heads) for a in heads]
    gl_col = [col_of(gl_all[bis[a]], hidx[a] + n_heads) for a in heads]
    gc_row = [gct_ref[bis[a], pl.ds(hidx[a] + n_heads, 1), :] for a in heads]
    decay = [jnp.where(tri_incl, jnp.exp(jnp.where(tri_incl, gc_col[a] - gc_row[a], 0.0)), 0.0) for a in heads]
    kb = [k[a] * beta_col[a] for a in heads]
    vb = [vf[bis[a]][:, sls[a]] * beta_col[a] for a in heads]
    k16 = [x.astype(BF16) for x in k]
    kk = [lax.dot_general(kb[a].astype(BF16), k16[a], NT_DIMS, preferred_element_type=F32) for a in heads]
    qk = [lax.dot_general(q[a].astype(BF16), k16[a], NT_DIMS, preferred_element_type=F32) for a in heads]
    lower = [jnp.where(tri_strict, kk[a] * decay[a], 0.0) for a in heads]
    attn = [jnp.where(tri_incl, qk[a] * decay[a], 0.0).astype(BF16) for a in heads]

    tinv = [eye - jnp.where(pair_masks[1], lower[a], 0.0) for a in heads]
    s = 2
    while s < C:
        t16 = [x.astype(BF16) for x in tinv]
        md = [jnp.dot(jnp.where(pair_masks[s], lower[a], 0.0).astype(BF16), t16[a], preferred_element_type=F32)
              for a in heads]
        tinv = [tinv[a] - jnp.dot(t16[a], md[a].astype(BF16), preferred_element_type=F32) for a in heads]
        s *= 2

    egc = [jnp.exp(x) for x in gc_col]
    uw = [jnp.dot(tinv[a].astype(BF16), jnp.concatenate([vb[a], kb[a] * egc[a]], axis=1).astype(BF16),
                  preferred_element_type=F32) for a in heads]
    q_dec = [q[a] * egc[a] for a in heads]
    k_dec_t = [(k[a] * jnp.exp(gl_col[a] - gc_col[a])).T for a in heads]
    egl = [jnp.exp(x) for x in gl_col]

    vnew_ref[...] = jnp.zeros_like(vnew_ref)
    outs = [[] for _ in heads]
    for c in range(G // C):
        r0 = c * C
        st = [state_ref[a] for a in heads]
        ws_qs = [jnp.dot(jnp.concatenate([uw[a][r0:r0 + C, D:], q_dec[a][r0:r0 + C]], axis=0).astype(BF16),
                         st[a].astype(BF16), preferred_element_type=F32) for a in heads]
        for a in heads:
            vnew_ref[a, r0:r0 + C, :] = (uw[a][r0:r0 + C, :D] - ws_qs[a][:C]).astype(BF16)
        vn = [vnew_ref[a] for a in heads]
        for a in heads:
            outs[a].append(ws_qs[a][C:] + jnp.dot(attn[a][r0:r0 + C], vn[a], preferred_element_type=F32))
        for a in heads:
            kt_c = jnp.where(ct == c, k_dec_t[a], 0.0).astype(BF16)
            e_c = jnp.concatenate([egl[a][r0:r0 + C]] * (D // C), axis=0)
            state_ref[a] = st[a] * e_c + jnp.dot(kt_c, vn[a], preferred_element_type=F32)

    for a in heads:
        o = jnp.concatenate(outs[a], axis=0)
        o = o * lax.rsqrt(jnp.mean(o * o, axis=-1, keepdims=True) + NORM_EPS) * onw_ref[...]
        z = z_ref[bis[a], :, sls[a]].astype(F32)
        o_ref[bis[a], :, sls[a]] = (o * _silu(z)).astype(o_ref.dtype)


def _deltanet(qkv3, z3, small3, a_log_row, dt_row, onorm_w, n_heads, col_q, col_k, col_v, col_z, heads_per_step=8,
              batch_rows_per_step=2):
    b, s, _ = qkv3.shape
    G, D, NH = DN_GROUP, LANES, heads_per_step
    NB = batch_rows_per_step if b % batch_rows_per_step == 0 else 1
    assert n_heads % NH == 0 and all(c % NH == 0 for c in (col_q, col_k, col_v, col_z))
    blk = lambda off: pl.BlockSpec((NB, G, NH * D), lambda bi, hi, gi: (bi, gi, off // NH + hi))
    row_spec = pl.BlockSpec((1, LANES), lambda bi, hi, gi: (0, 0))
    return pl.pallas_call(
        functools.partial(_deltanet_kernel, n_heads=n_heads, heads_per_step=NH),
        grid=(b // NB, n_heads // NH, s // G),
        in_specs=[blk(col_q), blk(col_k), blk(col_v), blk(col_z),
                  pl.BlockSpec((NB, G, LANES), lambda bi, hi, gi: (bi, gi, 0)),
                  row_spec, row_spec, row_spec],
        out_specs=pl.BlockSpec((NB, G, NH * D), lambda bi, hi, gi: (bi, gi, hi)),
        out_shape=jax.ShapeDtypeStruct((b, s, n_heads * D), BF16),
        scratch_shapes=[pltpu.VMEM((NB * NH, D, D), F32),
                        pltpu.VMEM((NB * NH, G, D), BF16), pltpu.VMEM((NB, LANES, G), F32)],
        compiler_params=_cparams(("parallel", "parallel", "arbitrary")),
        name="gated_deltanet",
    )(qkv3, qkv3, qkv3, z3, small3, a_log_row, dt_row, onorm_w)


def _t5_bucket_f32(dist):
    dist = jnp.maximum(dist, 0)
    max_exact = REL_BUCKETS // 2
    d = jnp.maximum(dist, 1).astype(F32)
    large = max_exact + (jnp.log(d / max_exact) / math.log(REL_MAX_DIST / max_exact)
                         * (REL_BUCKETS - max_exact)).astype(jnp.int32)
    large = jnp.minimum(large, REL_BUCKETS - 1)
    return jnp.where(dist < max_exact, dist, large)


def _n_near_tiles():
    return -(-(REL_MAX_DIST + MOBA_BLOCK - 1) // MOBA_BLOCK)


def _moba_kernel(rb_ref, q_ref, k_ref, v_ref, z_ref, o_ref, kmean_ref, bias_ref, kt_ref, xt_ref, qa_ref,
                 s0_ref, s1_ref, wide_ref, *, n_near, nbp, group, qblocks):
    hh = pl.program_id(0)
    bb = pl.program_id(1)
    BLK, D, U, QB = MOBA_BLOCK, LANES, group, qblocks
    R = QB * BLK
    nb = k_ref.shape[1] // BLK
    n_steps = nb // QB
    col_shift = nbp

    @pl.when(bb == 0)
    def _():
        qr = lax.broadcasted_iota(jnp.int32, (BLK, BLK), 0)
        kc = lax.broadcasted_iota(jnp.int32, (BLK, BLK), 1)
        for t in range(n_near):
            dist = t * BLK + qr - kc
            bucket = _t5_bucket_f32(dist)
            val = jnp.full((BLK, BLK), rb_ref[0, hh], F32)
            for bk in range(1, REL_BUCKETS):
                val = jnp.where(bucket == bk, rb_ref[bk, hh], val)
            if t == 0:
                val = jnp.where(dist >= 0, val, NEG_INF)
            bias_ref[t] = val
        bias_ref[n_near] = jnp.full((BLK, BLK), rb_ref[REL_BUCKETS - 1, hh], F32)
        ri = lax.broadcasted_iota(jnp.int32, (D, U * BLK), 0)
        ci = lax.broadcasted_iota(jnp.int32, (D, U * BLK), 1)
        for g in range(nb // U):
            hot = (ri == g * U + ci // BLK) | (ri == col_shift)
            kt_ref[g, D:2 * D, :] = jnp.where(hot, 1.0, 0.0).astype(BF16)

    ones_dd = jnp.ones((D, D), BF16)

    def norms(it, carry):
        k2, q2 = carry
        for u in range(U):
            j = it * U + u
            rows = pl.ds(pl.multiple_of(j * BLK, BLK), BLK)
            kj = k_ref[0, rows, :].astype(F32)
            qj = q_ref[0, rows, :].astype(F32)
            kmean_ref[pl.ds(j, 1), :] = jnp.mean(kj, axis=0, keepdims=True)
            kt_ref[it, 0:D, u * BLK:(u + 1) * BLK] = kj.T.astype(BF16)
            k2 = jnp.maximum(k2, jnp.dot((kj * kj).astype(BF16), ones_dd, preferred_element_type=F32))
            q2 = jnp.maximum(q2, jnp.dot((qj * qj).astype(BF16), ones_dd, preferred_element_type=F32))
        return k2, q2
    k2, q2 = lax.fori_loop(0, nb // U, norms, (jnp.zeros((BLK, D), F32), jnp.zeros((BLK, D), F32)))
    rb_abs = jnp.abs(rb_ref[0, hh])
    for bk in range(1, REL_BUCKETS):
        rb_abs = jnp.maximum(rb_abs, jnp.abs(rb_ref[bk, hh]))
    bound = jnp.sqrt(jnp.max(q2) * jnp.max(k2)) * 1.02 + rb_abs + 1e-3
    wide_ref[0] = jnp.where(bound > MOBA_SHIFT_SLACK, 1, 0).astype(jnp.int32)

    kmean16 = kmean_ref[...].astype(BF16)
    jrow = lax.broadcasted_iota(jnp.int32, (nb, BLK), 0)
    jrow_f = jrow.astype(F32)

    def visibility(it, carry):
        for u in range(U):
            ib = it * U + u
            rows = pl.ds(pl.multiple_of(ib * BLK, BLK), BLK)
            gate = lax.dot_general(kmean16, q_ref[0, rows, :], NT_DIMS, preferred_element_type=F32)
            past = jrow < ib
            sc = jnp.where(past, gate, NEG_INF)
            visible = jrow == ib
            for _ in range(MOBA_TOPK):
                top = jnp.max(sc, axis=0, keepdims=True)
                first = jnp.min(jnp.where(sc == top, jrow_f, float(nb)), axis=0, keepdims=True)
                pick = jrow_f == first
                visible = visible | (pick & past)
                sc = jnp.where(pick, jnp.finfo(F32).min, sc)
            cols = jnp.concatenate([jnp.where(visible, 0.0, NEG_INF), jnp.zeros((D - nb, BLK), F32)], axis=0)
            xt_ref[rows, :] = cols.T.astype(BF16)
        return carry
    lax.fori_loop(0, nb // U, visibility, 0)

    def set_query_operand(s, slot):
        rows = pl.ds(pl.multiple_of(s * R, R), R)
        qa_ref[slot, :, 0:D] = q_ref[0, rows, :]
        qa_ref[slot, :, D:2 * D] = xt_ref[rows, :]

    def pair_scores(s, g, slot):
        sc = jnp.dot(qa_ref[slot], kt_ref[g], preferred_element_type=F32)
        bias = jnp.concatenate(
            [jnp.concatenate([bias_ref[jnp.clip(s * QB + a - (g * U + u), 0, n_near)] for u in range(U)], axis=1)
             for a in range(QB)], axis=0)
        return sc + bias

    def lane_chunks(x):
        return [x[:, c * D:(c + 1) * D] for c in range(x.shape[1] // D)]

    @pl.when(wide_ref[0] == 1)
    def _():
        def step_max(s, carry):
            set_query_operand(s, 0)

            def max_step(g, mm):
                for c in lane_chunks(pair_scores(s, g, 0)):
                    mm = jnp.maximum(mm, c)
                return mm
            mm = lax.fori_loop(0, s + 1, max_step, jnp.full((R, D), jnp.finfo(F32).min, F32))
            m = jnp.max(mm, axis=1, keepdims=True)
            rows = pl.ds(pl.multiple_of(s * R, R), R)
            lane_q = lax.broadcasted_iota(jnp.int32, (R, D), 1)
            xt_ref[rows, :] = jnp.where(lane_q == col_shift, -m, xt_ref[rows, :].astype(F32)).astype(BF16)
            return carry
        lax.fori_loop(0, n_steps, step_max, 0)

    ones_cols = jnp.ones((U * BLK, D), BF16)

    def value_operand(g):
        vg = v_ref[0, pl.ds(pl.multiple_of(g * (U * BLK), U * BLK), U * BLK), :]
        return jnp.concatenate([vg, ones_cols], axis=1)

    def consume(vg, sc, acc):
        return acc + jnp.dot(jnp.exp(sc).astype(BF16), vg, preferred_element_type=F32)

    def finish_step(s, g, acc):
        @pl.when(g == s)
        def _():
            rows = pl.ds(pl.multiple_of(s * R, R), R)
            out = acc[:, 0:D] / acc[:, D:D + 1]
            z = z_ref[0, rows, :].astype(F32)
            o_ref[0, rows, :] = (out * _silu(z)).astype(o_ref.dtype)
        return acc * jnp.where(g == s, 0.0, 1.0).astype(F32)

    def stage(cur_ref, nxt_ref, carry):
        s, g, acc = carry
        wrap = g == s
        s2 = jnp.where(wrap, s + 1, s)
        g2 = jnp.where(wrap, 0, g + 1)

        @pl.when(wrap)
        def _():
            set_query_operand(s2, s2 & 1)

        nxt_ref[...] = pair_scores(s2, g2, s2 & 1)
        acc = consume(value_operand(g), cur_ref[...], acc)
        return s2, g2, finish_step(s, g, acc)

    def two_stages(t, carry):
        return stage(s1_ref, s0_ref, stage(s0_ref, s1_ref, carry))

    n_pairs = n_steps * (n_steps + 1) // 2
    set_query_operand(0, 0)
    s0_ref[...] = pair_scores(0, 0, 0)
    carry = (jnp.int32(0), jnp.int32(0), jnp.zeros((R, 2 * D), F32))
    carry = lax.fori_loop(0, (n_pairs - 1) // 2, two_stages, carry)
    last_ref = s0_ref
    if (n_pairs - 1) % 2:
        carry = stage(s0_ref, s1_ref, carry)
        last_ref = s1_ref
    s, g, acc = carry
    finish_step(s, g, consume(value_operand(g), last_ref[...], acc))


def _moba(rel_bias, p3, n_heads, col_q, col_k, col_v, col_z, group=4, qblocks=4):
    b, s, _ = p3.shape
    BLK, D = MOBA_BLOCK, LANES
    nb = s // BLK
    nbp = -(-nb // 16) * 16
    assert nbp + 16 <= D, "block-visibility columns must fit in the spare contraction columns"
    assert group == qblocks and nb % group == 0, "step s visits key groups 0..s"
    n_near = _n_near_tiles()
    rows = qblocks * BLK
    seq = lambda off: pl.BlockSpec((1, s, D), lambda hi, bi: (bi, 0, off + hi))
    return pl.pallas_call(
        functools.partial(_moba_kernel, n_near=n_near, nbp=nbp, group=group, qblocks=qblocks),
        grid=(n_heads, b),
        in_specs=[pl.BlockSpec(memory_space=pltpu.SMEM), seq(col_q), seq(col_k), seq(col_v), seq(col_z)],
        out_specs=seq(0),
        out_shape=jax.ShapeDtypeStruct((b, s, n_heads * D), BF16),
        scratch_shapes=[pltpu.VMEM((nb, D), F32), pltpu.VMEM((n_near + 1, BLK, BLK), F32),
                        pltpu.VMEM((nb // group, 2 * D, group * BLK), BF16), pltpu.VMEM((s, D), BF16),
                        pltpu.VMEM((2, rows, 2 * D), BF16),
                        pltpu.VMEM((rows, group * BLK), F32), pltpu.VMEM((rows, group * BLK), F32),
                        pltpu.SMEM((1,), jnp.int32)],
        compiler_params=_cparams(("arbitrary", "arbitrary")),
        name="moba_attention",
    )(rel_bias, p3, p3, p3, p3)


def _merge_kernel(ya_ref, yb_ref, g0_ref, g1_ref, x_ref, w0_ref, w1_ref, wo_ref, b0_ref, b1_ref, nw_ref, o_ref):
    pa = jnp.dot(ya_ref[...], w0_ref[...], preferred_element_type=F32)
    pb = jnp.dot(yb_ref[...], w1_ref[...], preferred_element_type=F32)
    g0 = _sigmoid(g0_ref[...].astype(F32) + b0_ref[...])
    g1 = _sigmoid(g1_ref[...].astype(F32) + b1_ref[...])
    merged = (g0 * pa + g1 * pb).astype(BF16)
    out = jnp.dot(merged, wo_ref[...], preferred_element_type=F32)
    y = out * lax.rsqrt(jnp.mean(out * out, axis=-1, keepdims=True) + NORM_EPS) * nw_ref[...]
    o_ref[...] = x_ref[...] + y


def _merge(ya, yb, p2, x2d, w0, w1, wo, b_gate, norm_w, col_g0, col_g1, tm=512):
    t, d = x2d.shape
    row = lambda c: pl.BlockSpec((tm, d), lambda i: (i, c))
    full = pl.BlockSpec((d, d), lambda i: (0, 0))
    vec = lambda c: pl.BlockSpec((1, d), lambda i: (0, c))
    return pl.pallas_call(
        _merge_kernel,
        grid=(t // tm,),
        in_specs=[row(0), row(0), row(col_g0), row(col_g1), row(0), full, full, full, vec(0), vec(1), vec(0)],
        out_specs=row(0),
        out_shape=jax.ShapeDtypeStruct((t, d), F32),
        compiler_params=_cparams(("parallel",)),
        name="merge_out",
    )(ya, yb, p2, p2, x2d, w0, w1, wo, b_gate.reshape(1, -1), b_gate.reshape(1, -1), norm_w.reshape(1, d))


def kernel(x, norm_pre_w, w_in, b_gate, conv_w, dn_a_log, dn_dt_bias, dn_onorm_w, rel_bias, w_branch, w_out,
           norm_post_w):
    b, s, d = x.shape
    n_heads = dn_a_log.shape[0]
    hd = dn_onorm_w.shape[0]
    width = n_heads * hd
    assert hd == LANES and rel_bias.shape[1] == n_heads and width == d
    assert s % MOBA_BLOCK == 0 and s % DN_GROUP == 0 and 2 * n_heads <= LANES
    t = b * s

    c_za = 3 * width
    c_beta = 4 * width
    c_qb = c_beta + 2 * n_heads
    assert conv_w.shape[1] == c_za
    w16 = w_in.astype(BF16)
    w_qkv_a = w16[:, :c_za]
    w_rest = jnp.concatenate([w16[:, c_za:c_beta], w16[:, c_qb:]], axis=1)
    w_small = jnp.pad(w16[:, c_beta:c_qb], ((0, 0), (0, LANES - 2 * n_heads)))
    nblk = width // LANES
    col = lambda k: k * nblk

    x2d = x.reshape(t, d)
    norm_w = norm_pre_w.astype(F32)
    col_scale = jnp.concatenate([jnp.ones((1, width), F32), jnp.full((1, width), hd ** -0.5, F32),
                                 jnp.ones((1, w_rest.shape[1] - 2 * width), F32)], axis=1)
    qkv_a, rest, small = _in_proj(x2d, norm_w, w_qkv_a, w_rest, col_scale, w_small, conv_w.astype(F32), s)
    qkv_a3 = qkv_a.reshape(b, s, -1)
    rest3 = rest.reshape(b, s, -1)

    pad = (0, LANES - 2 * n_heads)
    a_log_row = jnp.pad(jnp.concatenate([jnp.zeros_like(dn_a_log), dn_a_log]), pad).reshape(1, LANES).astype(F32)
    dt_row = jnp.pad(jnp.concatenate([jnp.zeros_like(dn_dt_bias), dn_dt_bias]), pad).reshape(1, LANES).astype(F32)
    ya = _deltanet(qkv_a3, rest3, small.reshape(b, s, LANES), a_log_row, dt_row,
                   dn_onorm_w.reshape(1, hd).astype(F32), n_heads, col(0), col(1), col(2), col(0))
    yb = _moba(rel_bias.astype(F32), rest3, n_heads, col(1), col(2), col(3), col(4))

    out = _merge(ya.reshape(t, d), yb.reshape(t, d), rest, x2d,
                 w_branch[0].astype(BF16), w_branch[1].astype(BF16), w_out.astype(BF16),
                 b_gate.astype(F32), norm_post_w.astype(F32), 5, 6)
    return out.reshape(b, s, d)
```

```python
import functools
import math

import jax
import jax.numpy as jnp
from jax import lax
from jax.experimental import pallas as pl
from jax.experimental.pallas import tpu as pltpu

F32 = jnp.float32
BF16 = jnp.bfloat16

LANES = 128
DN_CHUNK = 64
DN_GROUP = 128
DN_CONV_HALO = 8
MOBA_BLOCK = 256
MOBA_TOPK = 3
MOBA_SHIFT_SLACK = 40.0
REL_BUCKETS = 32
REL_MAX_DIST = 2048
NORM_EPS = 1e-6
NEG_INF = -1e30
VMEM_LIMIT = 56 * 1024 * 1024
NT_DIMS = (((1,), (1,)), ((), ()))


def _cparams(sem):
    return pltpu.CompilerParams(dimension_semantics=sem, vmem_limit_bytes=VMEM_LIMIT)


def _sigmoid(x):
    return 0.5 * jnp.tanh(0.5 * x) + 0.5


def _silu(x):
    h = 0.5 * x
    return h * jnp.tanh(h) + h


def _normalise(x, w):
    y = x * lax.rsqrt(jnp.mean(x * x, axis=-1, keepdims=True) + NORM_EPS)
    return (y * w).astype(BF16)


def _in_proj_kernel(x_ref, xh_ref, nw_ref, wc_ref, wr_ref, cs_ref, ws_ref, cw_ref, oc_ref, or_ref, sm_ref,
                    h_ref, hh_ref, *, tiles_per_seq, row_chunks):
    i = pl.program_id(0)
    tm = oc_ref.shape[0]
    halo = hh_ref.shape[0]

    @pl.when(pl.program_id(1) == 0)
    def _():
        h = _normalise(x_ref[...], nw_ref[...])
        h_ref[...] = h
        sm_ref[...] = jnp.dot(h, ws_ref[...], preferred_element_type=F32)
        hh = _normalise(xh_ref[...], nw_ref[...])
        hh_ref[...] = jnp.where(i % tiles_per_seq == 0, jnp.zeros_like(hh), hh)

    wc = wc_ref[...]
    wr = wr_ref[...]
    cw = cw_ref[...]
    kw = cw.shape[0]
    base = halo - (kw - 1)
    rm = tm // row_chunks
    tail = jnp.dot(hh_ref[...], wc, preferred_element_type=F32)
    for r in range(row_chunks):
        rows = slice(r * rm, (r + 1) * rm)
        hr = h_ref[rows, :]
        acc_c = jnp.dot(hr, wc, preferred_element_type=F32)
        acc_r = jnp.dot(hr, wr, preferred_element_type=F32)
        or_ref[rows, :] = (acc_r * cs_ref[...]).astype(or_ref.dtype)
        ext = jnp.concatenate([tail, acc_c], axis=0)
        y = ext[base:base + rm] * cw[0:1]
        for kk in range(1, kw):
            y = y + ext[base + kk:base + kk + rm] * cw[kk:kk + 1]
        oc_ref[rows, :] = _silu(y).astype(oc_ref.dtype)
        tail = acc_c[rm - halo:rm]


def _in_proj(x2d, norm_w, w_conv, w_rest, col_scale, w_small, conv_w, seq_len, tm=1024, steps=4, halo=16,
             row_chunks=4):
    t, d = x2d.shape
    nc, nr, ns = w_conv.shape[1], w_rest.shape[1], w_small.shape[1]
    kw = conv_w.shape[0]
    assert nc % (steps * LANES) == 0 and nr % (steps * LANES) == 0
    assert seq_len % tm == 0 and tm % halo == 0 and kw - 1 <= halo and conv_w.shape[1] == nc
    tc, tr = nc // steps, nr // steps
    row = lambda w: pl.BlockSpec((tm, w), lambda i, j: (i, j))
    col = lambda r, w: pl.BlockSpec((r, w), lambda i, j: (0, j))
    fixed = lambda r, w: pl.BlockSpec((r, w), lambda i, j: (0, 0))
    return pl.pallas_call(
        functools.partial(_in_proj_kernel, tiles_per_seq=seq_len // tm, row_chunks=row_chunks),
        grid=(t // tm, steps),
        in_specs=[pl.BlockSpec((tm, d), lambda i, j: (i, 0)),
                  pl.BlockSpec((halo, d), lambda i, j: (jnp.maximum(i * (tm // halo) - 1, 0), 0)),
                  fixed(1, d), col(d, tc), col(d, tr), col(1, tr), fixed(d, ns), col(kw, tc)],
        out_specs=[row(tc), row(tr), pl.BlockSpec((tm, ns), lambda i, j: (i, 0))],
        out_shape=[jax.ShapeDtypeStruct((t, nc), BF16), jax.ShapeDtypeStruct((t, nr), BF16),
                   jax.ShapeDtypeStruct((t, ns), F32)],
        scratch_shapes=[pltpu.VMEM((tm, d), BF16), pltpu.VMEM((halo, d), BF16)],
        compiler_params=_cparams(("parallel", "arbitrary")),
        name="in_proj",
    )(x2d, x2d, norm_w.reshape(1, d), w_conv, w_rest, col_scale, w_small, conv_w)


def _deltanet_kernel(q_ref, k_ref, v_ref, z_ref, sm_ref, alog_ref, dtb_ref, onw_ref,
                     o_ref, state_ref, vnew_ref, gct_ref, *, n_heads, heads_per_step):
    hg = pl.program_id(1)
    g = pl.program_id(2)
    G, C, D, NH = DN_GROUP, DN_CHUNK, LANES, heads_per_step
    lc = C.bit_length() - 1

    @pl.when(g == 0)
    def _():
        state_ref[...] = jnp.zeros_like(state_ref)

    def l2n(x):
        return x * lax.rsqrt(jnp.sum(x * x, axis=-1, keepdims=True) + NORM_EPS)

    NB = q_ref.shape[0]
    qf = [q_ref[bi].astype(F32) for bi in range(NB)]
    kf = [k_ref[bi].astype(F32) for bi in range(NB)]
    vf = [v_ref[bi].astype(F32) for bi in range(NB)]

    lane = lax.broadcasted_iota(jnp.int32, (G, LANES), 1)
    row = lax.broadcasted_iota(jnp.int32, (G, LANES), 0)
    pos = row & (C - 1)
    beta_all, gc_all, gl_all = [], [], []
    for bi in range(NB):
        cs = sm_ref[bi]
        beta_all.append(_sigmoid(cs))
        xg = cs + dtb_ref[...]
        softplus = jnp.maximum(xg, 0.0) + jnp.log1p(jnp.exp(-jnp.abs(xg)))
        gc = -jnp.exp(alog_ref[...]) * softplus
        shift = 1
        while shift < C:
            gc = gc + jnp.where(pos >= shift, pltpu.roll(gc, shift, 0), 0.0)
            shift *= 2
        gc_all.append(gc)
        gl_all.append(jnp.concatenate([jnp.broadcast_to(gc[c * C + C - 1:c * C + C], (C, LANES))
                                       for c in range(G // C)], axis=0))
        gct_ref[bi] = gc.T

    ri = lax.broadcasted_iota(jnp.int32, (G, G), 0)
    ci = lax.broadcasted_iota(jnp.int32, (G, G), 1)
    same = (ri >> lc) == (ci >> lc)
    tri_incl = same & (ri >= ci)
    tri_strict = same & (ri > ci)
    eye = jnp.where(ri == ci, 1.0, 0.0)
    ct = lax.broadcasted_iota(jnp.int32, (D, G), 1) >> lc

    def pair_mask(s):
        ls = s.bit_length() - 1
        return ((ri >> (ls + 1)) == (ci >> (ls + 1))) & (((ri >> ls) & 1) == 1) & (((ci >> ls) & 1) == 0)

    pair_masks = {}
    s = 1
    while s < C:
        pair_masks[s] = pair_mask(s)
        s *= 2

    heads = range(NB * NH)
    bis = [a // NH for a in heads]
    sls = [slice((a % NH) * D, (a % NH + 1) * D) for a in heads]
    hidx = [hg * NH + a % NH for a in heads]

    def col_of(x, lane_idx):
        return jnp.sum(jnp.where(lane == lane_idx, x, 0.0), axis=1, keepdims=True)

    q = [l2n(qf[bis[a]][:, sls[a]]) * (D ** -0.5) for a in heads]
    k = [l2n(kf[bis[a]][:, sls[a]]) for a in heads]
    beta_col = [col_of(beta_all[bis[a]], hidx[a]) for a in heads]
    gc_col = [col_of(gc_all[bis[a]], hidx[a] + n_heads) for a in heads]
    gl_col = [col_of(gl_all[bis[a]], hidx[a] + n_heads) for a in heads]
    gc_row = [gct_ref[bis[a], pl.ds(hidx[a] + n_heads, 1), :] for a in heads]
    decay = [jnp.where(tri_incl, jnp.exp(jnp.where(tri_incl, gc_col[a] - gc_row[a], 0.0)), 0.0) for a in heads]
    kb = [k[a] * beta_col[a] for a in heads]
    vb = [vf[bis[a]][:, sls[a]] * beta_col[a] for a in heads]
    k16 = [x.astype(BF16) for x in k]
    kk = [lax.dot_general(kb[a].astype(BF16), k16[a], NT_DIMS, preferred_element_type=F32) for a in heads]
    qk = [lax.dot_general(q[a].astype(BF16), k16[a], NT_DIMS, preferred_element_type=F32) for a in heads]
    lower = [jnp.where(tri_strict, kk[a] * decay[a], 0.0) for a in heads]
    attn = [jnp.where(tri_incl, qk[a] * decay[a], 0.0).astype(BF16) for a in heads]

    tinv = [eye - jnp.where(pair_masks[1], lower[a], 0.0) for a in heads]
    s = 2
    while s < C:
        t16 = [x.astype(BF16) for x in tinv]
        md = [jnp.dot(jnp.where(pair_masks[s], lower[a], 0.0).astype(BF16), t16[a], preferred_element_type=F32)
              for a in heads]
        tinv = [tinv[a] - jnp.dot(t16[a], md[a].astype(BF16), preferred_element_type=F32) for a in heads]
        s *= 2

    egc = [jnp.exp(x) for x in gc_col]
    uw = [jnp.dot(tinv[a].astype(BF16), jnp.concatenate([vb[a], kb[a] * egc[a]], axis=1).astype(BF16),
                  preferred_element_type=F32) for a in heads]
    q_dec = [q[a] * egc[a] for a in heads]
    k_dec_t = [(k[a] * jnp.exp(gl_col[a] - gc_col[a])).T for a in heads]
    egl = [jnp.exp(x) for x in gl_col]

    vnew_ref[...] = jnp.zeros_like(vnew_ref)
    outs = [[] for _ in heads]
    for c in range(G // C):
        r0 = c * C
        st = [state_ref[a] for a in heads]
        ws_qs = [jnp.dot(jnp.concatenate([uw[a][r0:r0 + C, D:], q_dec[a][r0:r0 + C]], axis=0).astype(BF16),
                         st[a].astype(BF16), preferred_element_type=F32) for a in heads]
        for a in heads:
            vnew_ref[a, r0:r0 + C, :] = (uw[a][r0:r0 + C, :D] - ws_qs[a][:C]).astype(BF16)
        vn = [vnew_ref[a] for a in heads]
        for a in heads:
            outs[a].append(ws_qs[a][C:] + jnp.dot(attn[a][r0:r0 + C], vn[a], preferred_element_type=F32))
        for a in heads:
            kt_c = jnp.where(ct == c, k_dec_t[a], 0.0).astype(BF16)
            e_c = jnp.concatenate([egl[a][r0:r0 + C]] * (D // C), axis=0)
            state_ref[a] = st[a] * e_c + jnp.dot(kt_c, vn[a], preferred_element_type=F32)

    for a in heads:
        o = jnp.concatenate(outs[a], axis=0)
        o = o * lax.rsqrt(jnp.mean(o * o, axis=-1, keepdims=True) + NORM_EPS) * onw_ref[...]
        z = z_ref[bis[a], :, sls[a]].astype(F32)
        o_ref[bis[a], :, sls[a]] = (o * _silu(z)).astype(o_ref.dtype)


def _deltanet(qkv3, z3, small3, a_log_row, dt_row, onorm_w, n_heads, col_q, col_k, col_v, col_z, heads_per_step=8,
              batch_rows_per_step=2):
    b, s, _ = qkv3.shape
    G, D, NH = DN_GROUP, LANES, heads_per_step
    NB = batch_rows_per_step if b % batch_rows_per_step == 0 else 1
    assert n_heads % NH == 0 and all(c % NH == 0 for c in (col_q, col_k, col_v, col_z))
    blk = lambda off: pl.BlockSpec((NB, G, NH * D), lambda bi, hi, gi: (bi, gi, off // NH + hi))
    row_spec = pl.BlockSpec((1, LANES), lambda bi, hi, gi: (0, 0))
    return pl.pallas_call(
        functools.partial(_deltanet_kernel, n_heads=n_heads, heads_per_step=NH),
        grid=(b // NB, n_heads // NH, s // G),
        in_specs=[blk(col_q), blk(col_k), blk(col_v), blk(col_z),
                  pl.BlockSpec((NB, G, LANES), lambda bi, hi, gi: (bi, gi, 0)),
                  row_spec, row_spec, row_spec],
        out_specs=pl.BlockSpec((NB, G, NH * D), lambda bi, hi, gi: (bi, gi, hi)),
        out_shape=jax.ShapeDtypeStruct((b, s, n_heads * D), BF16),
        scratch_shapes=[pltpu.VMEM((NB * NH, D, D), F32),
                        pltpu.VMEM((NB * NH, G, D), BF16), pltpu.VMEM((NB, LANES, G), F32)],
        compiler_params=_cparams(("parallel", "parallel", "arbitrary")),
        name="gated_deltanet",
    )(qkv3, qkv3, qkv3, z3, small3, a_log_row, dt_row, onorm_w)


def _t5_bucket_f32(dist):
    dist = jnp.maximum(dist, 0)
    max_exact = REL_BUCKETS // 2
    d = jnp.maximum(dist, 1).astype(F32)
    large = max_exact + (jnp.log(d / max_exact) / math.log(REL_MAX_DIST / max_exact)
                         * (REL_BUCKETS - max_exact)).astype(jnp.int32)
    large = jnp.minimum(large, REL_BUCKETS - 1)
    return jnp.where(dist < max_exact, dist, large)


def _n_near_tiles():
    return -(-(REL_MAX_DIST + MOBA_BLOCK - 1) // MOBA_BLOCK)


def _moba_kernel(rb_ref, q_ref, k_ref, v_ref, z_ref, o_ref, kmean_ref, bias_ref, kt_ref, xt_ref, qa_ref,
                 s0_ref, s1_ref, wide_ref, *, n_near, nbp, group, qblocks):
    hh = pl.program_id(0)
    bb = pl.program_id(1)
    BLK, D, U, QB = MOBA_BLOCK, LANES, group, qblocks
    R = QB * BLK
    nb = k_ref.shape[1] // BLK
    n_steps = nb // QB
    col_shift = nbp

    @pl.when(bb == 0)
    def _():
        qr = lax.broadcasted_iota(jnp.int32, (BLK, BLK), 0)
        kc = lax.broadcasted_iota(jnp.int32, (BLK, BLK), 1)
        for t in range(n_near):
            dist = t * BLK + qr - kc
            bucket = _t5_bucket_f32(dist)
            val = jnp.full((BLK, BLK), rb_ref[0, hh], F32)
            for bk in range(1, REL_BUCKETS):
                val = jnp.where(bucket == bk, rb_ref[bk, hh], val)
            if t == 0:
                val = jnp.where(dist >= 0, val, NEG_INF)
            bias_ref[t] = val
        bias_ref[n_near] = jnp.full((BLK, BLK), rb_ref[REL_BUCKETS - 1, hh], F32)
        ri = lax.broadcasted_iota(jnp.int32, (D, U * BLK), 0)
        ci = lax.broadcasted_iota(jnp.int32, (D, U * BLK), 1)
        for g in range(nb // U):
            hot = (ri == g * U + ci // BLK) | (ri == col_shift)
            kt_ref[g, D:2 * D, :] = jnp.where(hot, 1.0, 0.0).astype(BF16)

    ones_dd = jnp.ones((D, D), BF16)

    def norms(it, carry):
        k2, q2 = carry
        for u in range(U):
            j = it * U + u
            rows = pl.ds(pl.multiple_of(j * BLK, BLK), BLK)
            kj = k_ref[0, rows, :].astype(F32)
            qj = q_ref[0, rows, :].astype(F32)
            kmean_ref[pl.ds(j, 1), :] = jnp.mean(kj, axis=0, keepdims=True)
            kt_ref[it, 0:D, u * BLK:(u + 1) * BLK] = kj.T.astype(BF16)
            k2 = jnp.maximum(k2, jnp.dot((kj * kj).astype(BF16), ones_dd, preferred_element_type=F32))
            q2 = jnp.maximum(q2, jnp.dot((qj * qj).astype(BF16), ones_dd, preferred_element_type=F32))
        return k2, q2
    k2, q2 = lax.fori_loop(0, nb // U, norms, (jnp.zeros((BLK, D), F32), jnp.zeros((BLK, D), F32)))
    rb_abs = jnp.abs(rb_ref[0, hh])
    for bk in range(1, REL_BUCKETS):
        rb_abs = jnp.maximum(rb_abs, jnp.abs(rb_ref[bk, hh]))
    bound = jnp.sqrt(jnp.max(q2) * jnp.max(k2)) * 1.02 + rb_abs + 1e-3
    wide_ref[0] = jnp.where(bound > MOBA_SHIFT_SLACK, 1, 0).astype(jnp.int32)

    kmean16 = kmean_ref[...].astype(BF16)
    jrow = lax.broadcasted_iota(jnp.int32, (nb, BLK), 0)
    jrow_f = jrow.astype(F32)

    def visibility(it, carry):
        for u in range(U):
            ib = it * U + u
            rows = pl.ds(pl.multiple_of(ib * BLK, BLK), BLK)
            gate = lax.dot_general(kmean16, q_ref[0, rows, :], NT_DIMS, preferred_element_type=F32)
            past = jrow < ib
            sc = jnp.where(past, gate, NEG_INF)
            visible = jrow == ib
            for _ in range(MOBA_TOPK):
                top = jnp.max(sc, axis=0, keepdims=True)
                first = jnp.min(jnp.where(sc == top, jrow_f, float(nb)), axis=0, keepdims=True)
                pick = jrow_f == first
                visible = visible | (pick & past)
                sc = jnp.where(pick, jnp.finfo(F32).min, sc)
            cols = jnp.concatenate([jnp.where(visible, 0.0, NEG_INF), jnp.zeros((D - nb, BLK), F32)], axis=0)
            xt_ref[rows, :] = cols.T.astype(BF16)
        return carry
    lax.fori_loop(0, nb // U, visibility, 0)

    def set_query_operand(s, slot):
        rows = pl.ds(pl.multiple_of(s * R, R), R)
        qa_ref[slot, :, 0:D] = q_ref[0, rows, :]
        qa_ref[slot, :, D:2 * D] = xt_ref[rows, :]

    def pair_scores(s, g, slot):
        sc = jnp.dot(qa_ref[slot], kt_ref[g], preferred_element_type=F32)
        bias = jnp.concatenate(
            [jnp.concatenate([bias_ref[jnp.clip(s * QB + a - (g * U + u), 0, n_near)] for u in range(U)], axis=1)
             for a in range(QB)], axis=0)
        return sc + bias

    def lane_chunks(x):
        return [x[:, c * D:(c + 1) * D] for c in range(x.shape[1] // D)]

    @pl.when(wide_ref[0] == 1)
    def _():
        def step_max(s, carry):
            set_query_operand(s, 0)

            def max_step(g, mm):
                for c in lane_chunks(pair_scores(s, g, 0)):
                    mm = jnp.maximum(mm, c)
                return mm
            mm = lax.fori_loop(0, s + 1, max_step, jnp.full((R, D), jnp.finfo(F32).min, F32))
            m = jnp.max(mm, axis=1, keepdims=True)
            rows = pl.ds(pl.multiple_of(s * R, R), R)
            lane_q = lax.broadcasted_iota(jnp.int32, (R, D), 1)
            xt_ref[rows, :] = jnp.where(lane_q == col_shift, -m, xt_ref[rows, :].astype(F32)).astype(BF16)
            return carry
        lax.fori_loop(0, n_steps, step_max, 0)

    def value_operand(g):
        return v_ref[0, pl.ds(pl.multiple_of(g * (U * BLK), U * BLK), U * BLK), :]

    def consume(vg, sc, acc, lsum):
        p = jnp.exp(sc)
        for c in lane_chunks(p):
            lsum = lsum + c
        return acc + jnp.dot(p.astype(BF16), vg, preferred_element_type=F32), lsum

    def finish_step(s, g, acc, lsum):
        @pl.when(g == s)
        def _():
            rows = pl.ds(pl.multiple_of(s * R, R), R)
            out = acc / jnp.sum(lsum, axis=1, keepdims=True)
            z = z_ref[0, rows, :].astype(F32)
            o_ref[0, rows, :] = (out * _silu(z)).astype(o_ref.dtype)
        keep = jnp.where(g == s, 0.0, 1.0).astype(F32)
        return acc * keep, lsum * keep

    def stage(cur_ref, nxt_ref, carry):
        s, g, acc, lsum = carry
        wrap = g == s
        s2 = jnp.where(wrap, s + 1, s)
        g2 = jnp.where(wrap, 0, g + 1)

        @pl.when(wrap)
        def _():
            set_query_operand(s2, s2 & 1)

        nxt_ref[...] = pair_scores(s2, g2, s2 & 1)
        acc, lsum = consume(value_operand(g), cur_ref[...], acc, lsum)
        acc, lsum = finish_step(s, g, acc, lsum)
        return s2, g2, acc, lsum

    def two_stages(t, carry):
        return stage(s1_ref, s0_ref, stage(s0_ref, s1_ref, carry))

    n_pairs = n_steps * (n_steps + 1) // 2
    set_query_operand(0, 0)
    s0_ref[...] = pair_scores(0, 0, 0)
    zero = jnp.zeros((R, D), F32)
    carry = (jnp.int32(0), jnp.int32(0), zero, zero)
    carry = lax.fori_loop(0, (n_pairs - 1) // 2, two_stages, carry)
    last_ref = s0_ref
    if (n_pairs - 1) % 2:
        carry = stage(s0_ref, s1_ref, carry)
        last_ref = s1_ref
    s, g, acc, lsum = carry
    acc, lsum = consume(value_operand(g), last_ref[...], acc, lsum)
    finish_step(s, g, acc, lsum)


def _moba(rel_bias, p3, n_heads, col_q, col_k, col_v, col_z, group=4, qblocks=4):
    b, s, _ = p3.shape
    BLK, D = MOBA_BLOCK, LANES
    nb = s // BLK
    nbp = -(-nb // 16) * 16
    assert nbp + 16 <= D, "block-visibility columns must fit in the spare contraction columns"
    assert group == qblocks and nb % group == 0, "step s visits key groups 0..s"
    n_near = _n_near_tiles()
    rows = qblocks * BLK
    seq = lambda off: pl.BlockSpec((1, s, D), lambda hi, bi: (bi, 0, off + hi))
    return pl.pallas_call(
        functools.partial(_moba_kernel, n_near=n_near, nbp=nbp, group=group, qblocks=qblocks),
        grid=(n_heads, b),
        in_specs=[pl.BlockSpec(memory_space=pltpu.SMEM), seq(col_q), seq(col_k), seq(col_v), seq(col_z)],
        out_specs=seq(0),
        out_shape=jax.ShapeDtypeStruct((b, s, n_heads * D), BF16),
        scratch_shapes=[pltpu.VMEM((nb, D), F32), pltpu.VMEM((n_near + 1, BLK, BLK), F32),
                        pltpu.VMEM((nb // group, 2 * D, group * BLK), BF16), pltpu.VMEM((s, D), BF16),
                        pltpu.VMEM((2, rows, 2 * D), BF16),
                        pltpu.VMEM((rows, group * BLK), F32), pltpu.VMEM((rows, group * BLK), F32),
                        pltpu.SMEM((1,), jnp.int32)],
        compiler_params=_cparams(("arbitrary", "arbitrary")),
        name="moba_attention",
    )(rel_bias, p3, p3, p3, p3)


def _merge_kernel(ya_ref, yb_ref, g0_ref, g1_ref, x_ref, w0_ref, w1_ref, wo_ref, b0_ref, b1_ref, nw_ref, o_ref):
    pa = jnp.dot(ya_ref[...], w0_ref[...], preferred_element_type=F32)
    pb = jnp.dot(yb_ref[...], w1_ref[...], preferred_element_type=F32)
    g0 = _sigmoid(g0_ref[...].astype(F32) + b0_ref[...])
    g1 = _sigmoid(g1_ref[...].astype(F32) + b1_ref[...])
    merged = (g0 * pa + g1 * pb).astype(BF16)
    out = jnp.dot(merged, wo_ref[...], preferred_element_type=F32)
    y = out * lax.rsqrt(jnp.mean(out * out, axis=-1, keepdims=True) + NORM_EPS) * nw_ref[...]
    o_ref[...] = x_ref[...] + y


def _merge(ya, yb, p2, x2d, w0, w1, wo, b_gate, norm_w, col_g0, col_g1, tm=512):
    t, d = x2d.shape
    row = lambda c: pl.BlockSpec((tm, d), lambda i: (i, c))
    full = pl.BlockSpec((d, d), lambda i: (0, 0))
    vec = lambda c: pl.BlockSpec((1, d), lambda i: (0, c))
    return pl.pallas_call(
        _merge_kernel,
        grid=(t // tm,),
        in_specs=[row(0), row(0), row(col_g0), row(col_g1), row(0), full, full, full, vec(0), vec(1), vec(0)],
        out_specs=row(0),
        out_shape=jax.ShapeDtypeStruct((t, d), F32),
        compiler_params=_cparams(("parallel",)),
        name="merge_out",
    )(ya, yb, p2, p2, x2d, w0, w1, wo, b_gate.reshape(1, -1), b_gate.reshape(1, -1), norm_w.reshape(1, d))


def kernel(x, norm_pre_w, w_in, b_gate, conv_w, dn_a_log, dn_dt_bias, dn_onorm_w, rel_bias, w_branch, w_out,
           norm_post_w):
    b, s, d = x.shape
    n_heads = dn_a_log.shape[0]
    hd = dn_onorm_w.shape[0]
    width = n_heads * hd
    assert hd == LANES and rel_bias.shape[1] == n_heads and width == d
    assert s % MOBA_BLOCK == 0 and s % DN_GROUP == 0 and 2 * n_heads <= LANES
    t = b * s

    c_za = 3 * width
    c_beta = 4 * width
    c_qb = c_beta + 2 * n_heads
    assert conv_w.shape[1] == c_za
    w16 = w_in.astype(BF16)
    w_qkv_a = w16[:, :c_za]
    w_rest = jnp.concatenate([w16[:, c_za:c_beta], w16[:, c_qb:]], axis=1)
    w_small = jnp.pad(w16[:, c_beta:c_qb], ((0, 0), (0, LANES - 2 * n_heads)))
    nblk = width // LANES
    col = lambda k: k * nblk

    x2d = x.reshape(t, d)
    norm_w = norm_pre_w.astype(F32)
    col_scale = jnp.concatenate([jnp.ones((1, width), F32), jnp.full((1, width), hd ** -0.5, F32),
                                 jnp.ones((1, w_rest.shape[1] - 2 * width), F32)], axis=1)
    qkv_a, rest, small = _in_proj(x2d, norm_w, w_qkv_a, w_rest, col_scale, w_small, conv_w.astype(F32), s)
    qkv_a3 = qkv_a.reshape(b, s, -1)
    rest3 = rest.reshape(b, s, -1)

    pad = (0, LANES - 2 * n_heads)
    a_log_row = jnp.pad(jnp.concatenate([jnp.zeros_like(dn_a_log), dn_a_log]), pad).reshape(1, LANES).astype(F32)
    dt_row = jnp.pad(jnp.concatenate([jnp.zeros_like(dn_dt_bias), dn_dt_bias]), pad).reshape(1, LANES).astype(F32)
    ya = _deltanet(qkv_a3, rest3, small.reshape(b, s, LANES), a_log_row, dt_row,
                   dn_onorm_w.reshape(1, hd).astype(F32), n_heads, col(0), col(1), col(2), col(0))
    yb = _moba(rel_bias.astype(F32), rest3, n_heads, col(1), col(2), col(3), col(4))

    out = _merge(ya.reshape(t, d), yb.reshape(t, d), rest, x2d,
                 w_branch[0].astype(BF16), w_branch[1].astype(BF16), w_out.astype(BF16),
                 b_gate.astype(F32), norm_post_w.astype(F32), 5, 6)
    return out.reshape(b, s, d)
```

```python
import functools
import math

import jax
import jax.numpy as jnp
from jax import lax
from jax.experimental import pallas as pl
from jax.experimental.pallas import tpu as pltpu

F32 = jnp.float32
BF16 = jnp.bfloat16

LANES = 128
DN_CHUNK = 64
DN_GROUP = 128
DN_CONV_HALO = 8
MOBA_BLOCK = 256
MOBA_TOPK = 3
MOBA_SHIFT_SLACK = 40.0
REL_BUCKETS = 32
REL_MAX_DIST = 2048
NORM_EPS = 1e-6
NEG_INF = -1e30
VMEM_LIMIT = 56 * 1024 * 1024
NT_DIMS = (((1,), (1,)), ((), ()))


def _cparams(sem):
    return pltpu.CompilerParams(dimension_semantics=sem, vmem_limit_bytes=VMEM_LIMIT)


def _sigmoid(x):
    return 0.5 * jnp.tanh(0.5 * x) + 0.5


def _silu(x):
    h = 0.5 * x
    return h * jnp.tanh(h) + h


def _normalise(x, w):
    y = x * lax.rsqrt(jnp.mean(x * x, axis=-1, keepdims=True) + NORM_EPS)
    return (y * w).astype(BF16)


def _in_proj_kernel(x_ref, xh_ref, nw_ref, wc_ref, wr_ref, cs_ref, ws_ref, cw_ref, oc_ref, or_ref, sm_ref,
                    h_ref, hh_ref, *, tiles_per_seq, row_chunks):
    i = pl.program_id(0)
    tm = oc_ref.shape[0]
    halo = hh_ref.shape[0]

    @pl.when(pl.program_id(1) == 0)
    def _():
        h = _normalise(x_ref[...], nw_ref[...])
        h_ref[...] = h
        sm_ref[...] = jnp.dot(h, ws_ref[...], preferred_element_type=F32)
        hh = _normalise(xh_ref[...], nw_ref[...])
        hh_ref[...] = jnp.where(i % tiles_per_seq == 0, jnp.zeros_like(hh), hh)

    wc = wc_ref[...]
    wr = wr_ref[...]
    cw = cw_ref[...]
    kw = cw.shape[0]
    base = halo - (kw - 1)
    rm = tm // row_chunks
    tail = jnp.dot(hh_ref[...], wc, preferred_element_type=F32)
    for r in range(row_chunks):
        rows = slice(r * rm, (r + 1) * rm)
        hr = h_ref[rows, :]
        acc_c = jnp.dot(hr, wc, preferred_element_type=F32)
        acc_r = jnp.dot(hr, wr, preferred_element_type=F32)
        or_ref[rows, :] = (acc_r * cs_ref[...]).astype(or_ref.dtype)
        ext = jnp.concatenate([tail, acc_c], axis=0)
        y = ext[base:base + rm] * cw[0:1]
        for kk in range(1, kw):
            y = y + ext[base + kk:base + kk + rm] * cw[kk:kk + 1]
        oc_ref[rows, :] = _silu(y).astype(oc_ref.dtype)
        tail = acc_c[rm - halo:rm]


def _in_proj(x2d, norm_w, w_conv, w_rest, col_scale, w_small, conv_w, seq_len, tm=1024, steps=4, halo=16,
             row_chunks=4):
    t, d = x2d.shape
    nc, nr, ns = w_conv.shape[1], w_rest.shape[1], w_small.shape[1]
    kw = conv_w.shape[0]
    assert nc % (steps * LANES) == 0 and nr % (steps * LANES) == 0
    assert seq_len % tm == 0 and tm % halo == 0 and kw - 1 <= halo and conv_w.shape[1] == nc
    tc, tr = nc // steps, nr // steps
    row = lambda w: pl.BlockSpec((tm, w), lambda i, j: (i, j))
    col = lambda r, w: pl.BlockSpec((r, w), lambda i, j: (0, j))
    fixed = lambda r, w: pl.BlockSpec((r, w), lambda i, j: (0, 0))
    return pl.pallas_call(
        functools.partial(_in_proj_kernel, tiles_per_seq=seq_len // tm, row_chunks=row_chunks),
        grid=(t // tm, steps),
        in_specs=[pl.BlockSpec((tm, d), lambda i, j: (i, 0)),
                  pl.BlockSpec((halo, d), lambda i, j: (jnp.maximum(i * (tm // halo) - 1, 0), 0)),
                  fixed(1, d), col(d, tc), col(d, tr), col(1, tr), fixed(d, ns), col(kw, tc)],
        out_specs=[row(tc), row(tr), pl.BlockSpec((tm, ns), lambda i, j: (i, 0))],
        out_shape=[jax.ShapeDtypeStruct((t, nc), BF16), jax.ShapeDtypeStruct((t, nr), BF16),
                   jax.ShapeDtypeStruct((t, ns), F32)],
        scratch_shapes=[pltpu.VMEM((tm, d), BF16), pltpu.VMEM((halo, d), BF16)],
        compiler_params=_cparams(("parallel", "arbitrary")),
        name="in_proj",
    )(x2d, x2d, norm_w.reshape(1, d), w_conv, w_rest, col_scale, w_small, conv_w)


def _deltanet_kernel(q_ref, k_ref, v_ref, z_ref, sm_ref, alog_ref, dtb_ref, onw_ref,
                     o_ref, state_ref, vnew_ref, gct_ref, *, n_heads, heads_per_step):
    hg = pl.program_id(1)
    g = pl.program_id(2)
    G, C, D, NH = DN_GROUP, DN_CHUNK, LANES, heads_per_step
    lc = C.bit_length() - 1

    @pl.when(g == 0)
    def _():
        state_ref[...] = jnp.zeros_like(state_ref)

    def l2n(x):
        return x * lax.rsqrt(jnp.sum(x * x, axis=-1, keepdims=True) + NORM_EPS)

    NB = q_ref.shape[0]
    qf = [q_ref[bi].astype(F32) for bi in range(NB)]
    kf = [k_ref[bi].astype(F32) for bi in range(NB)]
    vf = [v_ref[bi].astype(F32) for bi in range(NB)]

    lane = lax.broadcasted_iota(jnp.int32, (G, LANES), 1)
    row = lax.broadcasted_iota(jnp.int32, (G, LANES), 0)
    pos = row & (C - 1)
    beta_all, gc_all, gl_all = [], [], []
    for bi in range(NB):
        cs = sm_ref[bi]
        beta_all.append(_sigmoid(cs))
        xg = cs + dtb_ref[...]
        softplus = jnp.maximum(xg, 0.0) + jnp.log1p(jnp.exp(-jnp.abs(xg)))
        gc = -jnp.exp(alog_ref[...]) * softplus
        shift = 1
        while shift < C:
            gc = gc + jnp.where(pos >= shift, pltpu.roll(gc, shift, 0), 0.0)
            shift *= 2
        gc_all.append(gc)
        gl_all.append(jnp.concatenate([jnp.broadcast_to(gc[c * C + C - 1:c * C + C], (C, LANES))
                                       for c in range(G // C)], axis=0))
        gct_ref[bi] = gc.T

    ri = lax.broadcasted_iota(jnp.int32, (G, G), 0)
    ci = lax.broadcasted_iota(jnp.int32, (G, G), 1)
    same = (ri >> lc) == (ci >> lc)
    tri_incl = same & (ri >= ci)
    tri_strict = same & (ri > ci)
    eye = jnp.where(ri == ci, 1.0, 0.0)
    ct = lax.broadcasted_iota(jnp.int32, (D, G), 1) >> lc

    def pair_mask(s):
        ls = s.bit_length() - 1
        return ((ri >> (ls + 1)) == (ci >> (ls + 1))) & (((ri >> ls) & 1) == 1) & (((ci >> ls) & 1) == 0)

    pair_masks = {}
    s = 1
    while s < C:
        pair_masks[s] = pair_mask(s)
        s *= 2

    heads = range(NB * NH)
    bis = [a // NH for a in heads]
    sls = [slice((a % NH) * D, (a % NH + 1) * D) for a in heads]
    hidx = [hg * NH + a % NH for a in heads]

    def col_of(x, lane_idx):
        return jnp.sum(jnp.where(lane == lane_idx, x, 0.0), axis=1, keepdims=True)

    q = [l2n(qf[bis[a]][:, sls[a]]) * (D ** -0.5) for a in heads]
    k = [l2n(kf[bis[a]][:, sls[a]]) for a in heads]
    beta_col = [col_of(beta_all[bis[a]], hidx[a]) for a in heads]
    gc_col = [col_of(gc_all[bis[a]], hidx[a] + n_heads) for a in heads]
    gl_col = [col_of(gl_all[bis[a]], hidx[a] + n_heads) for a in heads]
    gc_row = [gct_ref[bis[a], pl.ds(hidx[a] + n_heads, 1), :] for a in heads]
    decay = [jnp.where(tri_incl, jnp.exp(jnp.where(tri_incl, gc_col[a] - gc_row[a], 0.0)), 0.0) for a in heads]
    kb = [k[a] * beta_col[a] for a in heads]
    vb = [vf[bis[a]][:, sls[a]] * beta_col[a] for a in heads]
    k_t16 = [x.T.astype(BF16) for x in k]
    kk = [jnp.dot(kb[a].astype(BF16), k_t16[a], preferred_element_type=F32) for a in heads]
    qk = [jnp.dot(q[a].astype(BF16), k_t16[a], preferred_element_type=F32) for a in heads]
    lower = [jnp.where(tri_strict, kk[a] * decay[a], 0.0) for a in heads]
    attn = [jnp.where(tri_incl, qk[a] * decay[a], 0.0).astype(BF16) for a in heads]

    tinv = [eye - jnp.where(pair_masks[1], lower[a], 0.0) for a in heads]
    s = 2
    while s < C:
        t16 = [x.astype(BF16) for x in tinv]
        md = [jnp.dot(jnp.where(pair_masks[s], lower[a], 0.0).astype(BF16), t16[a], preferred_element_type=F32)
              for a in heads]
        tinv = [tinv[a] - jnp.dot(t16[a], md[a].astype(BF16), preferred_element_type=F32) for a in heads]
        s *= 2

    egc = [jnp.exp(x) for x in gc_col]
    uw = [jnp.dot(tinv[a].astype(BF16), jnp.concatenate([vb[a], kb[a] * egc[a]], axis=1).astype(BF16),
                  preferred_element_type=F32) for a in heads]
    q_dec = [q[a] * egc[a] for a in heads]
    k_dec_t = [(k[a] * jnp.exp(gl_col[a] - gc_col[a])).T for a in heads]
    egl = [jnp.exp(x) for x in gl_col]

    vnew_ref[...] = jnp.zeros_like(vnew_ref)
    outs = [[] for _ in heads]
    for c in range(G // C):
        r0 = c * C
        st = [state_ref[a] for a in heads]
        ws_qs = [jnp.dot(jnp.concatenate([uw[a][r0:r0 + C, D:], q_dec[a][r0:r0 + C]], axis=0).astype(BF16),
                         st[a].astype(BF16), preferred_element_type=F32) for a in heads]
        for a in heads:
            vnew_ref[a, r0:r0 + C, :] = (uw[a][r0:r0 + C, :D] - ws_qs[a][:C]).astype(BF16)
        vn = [vnew_ref[a] for a in heads]
        for a in heads:
            outs[a].append(ws_qs[a][C:] + jnp.dot(attn[a][r0:r0 + C], vn[a], preferred_element_type=F32))
        for a in heads:
            kt_c = jnp.where(ct == c, k_dec_t[a], 0.0).astype(BF16)
            e_c = jnp.concatenate([egl[a][r0:r0 + C]] * (D // C), axis=0)
            state_ref[a] = st[a] * e_c + jnp.dot(kt_c, vn[a], preferred_element_type=F32)

    for a in heads:
        o = jnp.concatenate(outs[a], axis=0)
        o = o * lax.rsqrt(jnp.mean(o * o, axis=-1, keepdims=True) + NORM_EPS) * onw_ref[...]
        z = z_ref[bis[a], :, sls[a]].astype(F32)
        o_ref[bis[a], :, sls[a]] = (o * _silu(z)).astype(o_ref.dtype)


def _deltanet(qkv3, z3, small3, a_log_row, dt_row, onorm_w, n_heads, col_q, col_k, col_v, col_z, heads_per_step=8,
              batch_rows_per_step=2):
    b, s, _ = qkv3.shape
    G, D, NH = DN_GROUP, LANES, heads_per_step
    NB = batch_rows_per_step if b % batch_rows_per_step == 0 else 1
    assert n_heads % NH == 0 and all(c % NH == 0 for c in (col_q, col_k, col_v, col_z))
    blk = lambda off: pl.BlockSpec((NB, G, NH * D), lambda bi, hi, gi: (bi, gi, off // NH + hi))
    row_spec = pl.BlockSpec((1, LANES), lambda bi, hi, gi: (0, 0))
    return pl.pallas_call(
        functools.partial(_deltanet_kernel, n_heads=n_heads, heads_per_step=NH),
        grid=(b // NB, n_heads // NH, s // G),
        in_specs=[blk(col_q), blk(col_k), blk(col_v), blk(col_z),
                  pl.BlockSpec((NB, G, LANES), lambda bi, hi, gi: (bi, gi, 0)),
                  row_spec, row_spec, row_spec],
        out_specs=pl.BlockSpec((NB, G, NH * D), lambda bi, hi, gi: (bi, gi, hi)),
        out_shape=jax.ShapeDtypeStruct((b, s, n_heads * D), BF16),
        scratch_shapes=[pltpu.VMEM((NB * NH, D, D), F32),
                        pltpu.VMEM((NB * NH, G, D), BF16), pltpu.VMEM((NB, LANES, G), F32)],
        compiler_params=_cparams(("parallel", "parallel", "arbitrary")),
        name="gated_deltanet",
    )(qkv3, qkv3, qkv3, z3, small3, a_log_row, dt_row, onorm_w)


def _t5_bucket_f32(dist):
    dist = jnp.maximum(dist, 0)
    max_exact = REL_BUCKETS // 2
    d = jnp.maximum(dist, 1).astype(F32)
    large = max_exact + (jnp.log(d / max_exact) / math.log(REL_MAX_DIST / max_exact)
                         * (REL_BUCKETS - max_exact)).astype(jnp.int32)
    large = jnp.minimum(large, REL_BUCKETS - 1)
    return jnp.where(dist < max_exact, dist, large)


def _n_near_tiles():
    return -(-(REL_MAX_DIST + MOBA_BLOCK - 1) // MOBA_BLOCK)


def _moba_kernel(rb_ref, q_ref, k_ref, v_ref, z_ref, o_ref, kmean_ref, bias_ref, kt_ref, xt_ref, qa_ref,
                 s0_ref, s1_ref, wide_ref, *, n_near, nbp, group, qblocks):
    hh = pl.program_id(0)
    bb = pl.program_id(1)
    BLK, D, U, QB = MOBA_BLOCK, LANES, group, qblocks
    R = QB * BLK
    nb = k_ref.shape[1] // BLK
    n_steps = nb // QB
    col_shift = nbp

    @pl.when(bb == 0)
    def _():
        qr = lax.broadcasted_iota(jnp.int32, (BLK, BLK), 0)
        kc = lax.broadcasted_iota(jnp.int32, (BLK, BLK), 1)
        for t in range(n_near):
            dist = t * BLK + qr - kc
            bucket = _t5_bucket_f32(dist)
            val = jnp.full((BLK, BLK), rb_ref[0, hh], F32)
            for bk in range(1, REL_BUCKETS):
                val = jnp.where(bucket == bk, rb_ref[bk, hh], val)
            if t == 0:
                val = jnp.where(dist >= 0, val, NEG_INF)
            bias_ref[t] = val
        bias_ref[n_near] = jnp.full((BLK, BLK), rb_ref[REL_BUCKETS - 1, hh], F32)
        ri = lax.broadcasted_iota(jnp.int32, (D, U * BLK), 0)
        ci = lax.broadcasted_iota(jnp.int32, (D, U * BLK), 1)
        for g in range(nb // U):
            hot = (ri == g * U + ci // BLK) | (ri == col_shift)
            kt_ref[g, D:2 * D, :] = jnp.where(hot, 1.0, 0.0).astype(BF16)

    ones_dd = jnp.ones((D, D), BF16)

    def norms(it, carry):
        k2, q2 = carry
        for u in range(U):
            j = it * U + u
            rows = pl.ds(pl.multiple_of(j * BLK, BLK), BLK)
            kj = k_ref[0, rows, :].astype(F32)
            qj = q_ref[0, rows, :].astype(F32)
            kmean_ref[pl.ds(j, 1), :] = jnp.mean(kj, axis=0, keepdims=True)
            kt_ref[it, 0:D, u * BLK:(u + 1) * BLK] = kj.T.astype(BF16)
            k2 = jnp.maximum(k2, jnp.dot((kj * kj).astype(BF16), ones_dd, preferred_element_type=F32))
            q2 = jnp.maximum(q2, jnp.dot((qj * qj).astype(BF16), ones_dd, preferred_element_type=F32))
        return k2, q2
    k2, q2 = lax.fori_loop(0, nb // U, norms, (jnp.zeros((BLK, D), F32), jnp.zeros((BLK, D), F32)))
    rb_abs = jnp.abs(rb_ref[0, hh])
    for bk in range(1, REL_BUCKETS):
        rb_abs = jnp.maximum(rb_abs, jnp.abs(rb_ref[bk, hh]))
    bound = jnp.sqrt(jnp.max(q2) * jnp.max(k2)) * 1.02 + rb_abs + 1e-3
    wide_ref[0] = jnp.where(bound > MOBA_SHIFT_SLACK, 1, 0).astype(jnp.int32)

    kmean16 = kmean_ref[...].astype(BF16)
    jrow = lax.broadcasted_iota(jnp.int32, (nb, BLK), 0)
    jrow_f = jrow.astype(F32)

    def visibility(it, carry):
        for u in range(U):
            ib = it * U + u
            rows = pl.ds(pl.multiple_of(ib * BLK, BLK), BLK)
            gate = lax.dot_general(kmean16, q_ref[0, rows, :], NT_DIMS, preferred_element_type=F32)
            past = jrow < ib
            sc = jnp.where(past, gate, NEG_INF)
            visible = jrow == ib
            for _ in range(MOBA_TOPK):
                top = jnp.max(sc, axis=0, keepdims=True)
                first = jnp.min(jnp.where(sc == top, jrow_f, float(nb)), axis=0, keepdims=True)
                pick = jrow_f == first
                visible = visible | (pick & past)
                sc = jnp.where(pick, jnp.finfo(F32).min, sc)
            cols = jnp.concatenate([jnp.where(visible, 0.0, NEG_INF), jnp.zeros((D - nb, BLK), F32)], axis=0)
            xt_ref[rows, :] = cols.T.astype(BF16)
        return carry
    lax.fori_loop(0, nb // U, visibility, 0)

    def set_query_operand(s, slot):
        rows = pl.ds(pl.multiple_of(s * R, R), R)
        qa_ref[slot, :, 0:D] = q_ref[0, rows, :]
        qa_ref[slot, :, D:2 * D] = xt_ref[rows, :]

    def pair_scores(s, g, slot):
        sc = jnp.dot(qa_ref[slot], kt_ref[g], preferred_element_type=F32)
        bias = jnp.concatenate(
            [jnp.concatenate([bias_ref[jnp.clip(s * QB + a - (g * U + u), 0, n_near)] for u in range(U)], axis=1)
             for a in range(QB)], axis=0)
        return sc + bias

    def lane_chunks(x):
        return [x[:, c * D:(c + 1) * D] for c in range(x.shape[1] // D)]

    @pl.when(wide_ref[0] == 1)
    def _():
        def step_max(s, carry):
            set_query_operand(s, 0)

            def max_step(g, mm):
                for c in lane_chunks(pair_scores(s, g, 0)):
                    mm = jnp.maximum(mm, c)
                return mm
            mm = lax.fori_loop(0, s + 1, max_step, jnp.full((R, D), jnp.finfo(F32).min, F32))
            m = jnp.max(mm, axis=1, keepdims=True)
            rows = pl.ds(pl.multiple_of(s * R, R), R)
            lane_q = lax.broadcasted_iota(jnp.int32, (R, D), 1)
            xt_ref[rows, :] = jnp.where(lane_q == col_shift, -m, xt_ref[rows, :].astype(F32)).astype(BF16)
            return carry
        lax.fori_loop(0, n_steps, step_max, 0)

    ones_cols = jnp.ones((U * BLK, D), BF16)

    def value_operand(g):
        vg = v_ref[0, pl.ds(pl.multiple_of(g * (U * BLK), U * BLK), U * BLK), :]
        return jnp.concatenate([vg, ones_cols], axis=1)

    def consume(vg, sc, acc):
        return acc + jnp.dot(jnp.exp(sc).astype(BF16), vg, preferred_element_type=F32)

    def finish_step(s, g, acc):
        @pl.when(g == s)
        def _():
            rows = pl.ds(pl.multiple_of(s * R, R), R)
            out = acc[:, 0:D] / acc[:, D:D + 1]
            z = z_ref[0, rows, :].astype(F32)
            o_ref[0, rows, :] = (out * _silu(z)).astype(o_ref.dtype)
        return acc * jnp.where(g == s, 0.0, 1.0).astype(F32)

    def stage(cur_ref, nxt_ref, carry):
        s, g, acc = carry
        wrap = g == s
        s2 = jnp.where(wrap, s + 1, s)
        g2 = jnp.where(wrap, 0, g + 1)

        @pl.when(wrap)
        def _():
            set_query_operand(s2, s2 & 1)

        nxt_ref[...] = pair_scores(s2, g2, s2 & 1)
        acc = consume(value_operand(g), cur_ref[...], acc)
        return s2, g2, finish_step(s, g, acc)

    def two_stages(t, carry):
        return stage(s1_ref, s0_ref, stage(s0_ref, s1_ref, carry))

    n_pairs = n_steps * (n_steps + 1) // 2
    set_query_operand(0, 0)
    s0_ref[...] = pair_scores(0, 0, 0)
    carry = (jnp.int32(0), jnp.int32(0), jnp.zeros((R, 2 * D), F32))
    carry = lax.fori_loop(0, (n_pairs - 1) // 2, two_stages, carry)
    last_ref = s0_ref
    if (n_pairs - 1) % 2:
        carry = stage(s0_ref, s1_ref, carry)
        last_ref = s1_ref
    s, g, acc = carry
    finish_step(s, g, consume(value_operand(g), last_ref[...], acc))


def _moba(rel_bias, p3, n_heads, col_q, col_k, col_v, col_z, group=4, qblocks=4):
    b, s, _ = p3.shape
    BLK, D = MOBA_BLOCK, LANES
    nb = s // BLK
    nbp = -(-nb // 16) * 16
    assert nbp + 16 <= D, "block-visibility columns must fit in the spare contraction columns"
    assert group == qblocks and nb % group == 0, "step s visits key groups 0..s"
    n_near = _n_near_tiles()
    rows = qblocks * BLK
    seq = lambda off: pl.BlockSpec((1, s, D), lambda hi, bi: (bi, 0, off + hi))
    return pl.pallas_call(
        functools.partial(_moba_kernel, n_near=n_near, nbp=nbp, group=group, qblocks=qblocks),
        grid=(n_heads, b),
        in_specs=[pl.BlockSpec(memory_space=pltpu.SMEM), seq(col_q), seq(col_k), seq(col_v), seq(col_z)],
        out_specs=seq(0),
        out_shape=jax.ShapeDtypeStruct((b, s, n_heads * D), BF16),
        scratch_shapes=[pltpu.VMEM((nb, D), F32), pltpu.VMEM((n_near + 1, BLK, BLK), F32),
                        pltpu.VMEM((nb // group, 2 * D, group * BLK), BF16), pltpu.VMEM((s, D), BF16),
                        pltpu.VMEM((2, rows, 2 * D), BF16),
                        pltpu.VMEM((rows, group * BLK), F32), pltpu.VMEM((rows, group * BLK), F32),
                        pltpu.SMEM((1,), jnp.int32)],
        compiler_params=_cparams(("arbitrary", "arbitrary")),
        name="moba_attention",
    )(rel_bias, p3, p3, p3, p3)


def _merge_kernel(ya_ref, yb_ref, g0_ref, g1_ref, x_ref, w0_ref, w1_ref, wo_ref, b0_ref, b1_ref, nw_ref, o_ref):
    pa = jnp.dot(ya_ref[...], w0_ref[...], preferred_element_type=F32)
    pb = jnp.dot(yb_ref[...], w1_ref[...], preferred_element_type=F32)
    g0 = _sigmoid(g0_ref[...].astype(F32) + b0_ref[...])
    g1 = _sigmoid(g1_ref[...].astype(F32) + b1_ref[...])
    merged = (g0 * pa + g1 * pb).astype(BF16)
    out = jnp.dot(merged, wo_ref[...], preferred_element_type=F32)
    y = out * lax.rsqrt(jnp.mean(out * out, axis=-1, keepdims=True) + NORM_EPS) * nw_ref[...]
    o_ref[...] = x_ref[...] + y


def _merge(ya, yb, p2, x2d, w0, w1, wo, b_gate, norm_w, col_g0, col_g1, tm=512):
    t, d = x2d.shape
    row = lambda c: pl.BlockSpec((tm, d), lambda i: (i, c))
    full = pl.BlockSpec((d, d), lambda i: (0, 0))
    vec = lambda c: pl.BlockSpec((1, d), lambda i: (0, c))
    return pl.pallas_call(
        _merge_kernel,
        grid=(t // tm,),
        in_specs=[row(0), row(0), row(col_g0), row(col_g1), row(0), full, full, full, vec(0), vec(1), vec(0)],
        out_specs=row(0),
        out_shape=jax.ShapeDtypeStruct((t, d), F32),
        compiler_params=_cparams(("parallel",)),
        name="merge_out",
    )(ya, yb, p2, p2, x2d, w0, w1, wo, b_gate.reshape(1, -1), b_gate.reshape(1, -1), norm_w.reshape(1, d))


def kernel(x, norm_pre_w, w_in, b_gate, conv_w, dn_a_log, dn_dt_bias, dn_onorm_w, rel_bias, w_branch, w_out,
           norm_post_w):
    b, s, d = x.shape
    n_heads = dn_a_log.shape[0]
    hd = dn_onorm_w.shape[0]
    width = n_heads * hd
    assert hd == LANES and rel_bias.shape[1] == n_heads and width == d
    assert s % MOBA_BLOCK == 0 and s % DN_GROUP == 0 and 2 * n_heads <= LANES
    t = b * s

    c_za = 3 * width
    c_beta = 4 * width
    c_qb = c_beta + 2 * n_heads
    assert conv_w.shape[1] == c_za
    w16 = w_in.astype(BF16)
    w_qkv_a = w16[:, :c_za]
    w_rest = jnp.concatenate([w16[:, c_za:c_beta], w16[:, c_qb:]], axis=1)
    w_small = jnp.pad(w16[:, c_beta:c_qb], ((0, 0), (0, LANES - 2 * n_heads)))
    nblk = width // LANES
    col = lambda k: k * nblk

    x2d = x.reshape(t, d)
    norm_w = norm_pre_w.astype(F32)
    col_scale = jnp.concatenate([jnp.ones((1, width), F32), jnp.full((1, width), hd ** -0.5, F32),
                                 jnp.ones((1, w_rest.shape[1] - 2 * width), F32)], axis=1)
    qkv_a, rest, small = _in_proj(x2d, norm_w, w_qkv_a, w_rest, col_scale, w_small, conv_w.astype(F32), s)
    qkv_a3 = qkv_a.reshape(b, s, -1)
    rest3 = rest.reshape(b, s, -1)

    pad = (0, LANES - 2 * n_heads)
    a_log_row = jnp.pad(jnp.concatenate([jnp.zeros_like(dn_a_log), dn_a_log]), pad).reshape(1, LANES).astype(F32)
    dt_row = jnp.pad(jnp.concatenate([jnp.zeros_like(dn_dt_bias), dn_dt_bias]), pad).reshape(1, LANES).astype(F32)
    ya = _deltanet(qkv_a3, rest3, small.reshape(b, s, LANES), a_log_row, dt_row,
                   dn_onorm_w.reshape(1, hd).astype(F32), n_heads, col(0), col(1), col(2), col(0))
    yb = _moba(rel_bias.astype(F32), rest3, n_heads, col(1), col(2), col(3), col(4))

    out = _merge(ya.reshape(t, d), yb.reshape(t, d), rest, x2d,
                 w_branch[0].astype(BF16), w_branch[1].astype(BF16), w_out.astype(BF16),
                 b_gate.astype(F32), norm_post_w.astype(F32), 5, 6)
    return out.reshape(b, s, d)
```

```python
import functools
import math

import jax
import jax.numpy as jnp
from jax import lax
from jax.experimental import pallas as pl
from jax.experimental.pallas import tpu as pltpu

F32 = jnp.float32
BF16 = jnp.bfloat16

LANES = 128
DN_CHUNK = 64
DN_GROUP = 128
DN_CONV_HALO = 8
MOBA_BLOCK = 256
MOBA_TOPK = 3
MOBA_SHIFT_SLACK = 40.0
REL_BUCKETS = 32
REL_MAX_DIST = 2048
NORM_EPS = 1e-6
NEG_INF = -1e30
VMEM_LIMIT = 56 * 1024 * 1024
NT_DIMS = (((1,), (1,)), ((), ()))


def _cparams(sem):
    return pltpu.CompilerParams(dimension_semantics=sem, vmem_limit_bytes=VMEM_LIMIT)


def _sigmoid(x):
    return 0.5 * jnp.tanh(0.5 * x) + 0.5


def _silu(x):
    h = 0.5 * x
    return h * jnp.tanh(h) + h


def _normalise(x, w):
    y = x * lax.rsqrt(jnp.mean(x * x, axis=-1, keepdims=True) + NORM_EPS)
    return (y * w).astype(BF16)


def _in_proj_kernel(x_ref, xh_ref, nw_ref, wc_ref, wr_ref, cs_ref, ws_ref, cw_ref, oc_ref, or_ref, sm_ref,
                    h_ref, hh_ref, *, tiles_per_seq, row_chunks):
    i = pl.program_id(0)
    tm = oc_ref.shape[0]
    halo = hh_ref.shape[0]

    @pl.when(pl.program_id(1) == 0)
    def _():
        h = _normalise(x_ref[...], nw_ref[...])
        h_ref[...] = h
        sm_ref[...] = jnp.dot(h, ws_ref[...], preferred_element_type=F32)
        hh = _normalise(xh_ref[...], nw_ref[...])
        hh_ref[...] = jnp.where(i % tiles_per_seq == 0, jnp.zeros_like(hh), hh)

    wc = wc_ref[...]
    wr = wr_ref[...]
    cw = cw_ref[...]
    kw = cw.shape[0]
    base = halo - (kw - 1)
    rm = tm // row_chunks
    tail = jnp.dot(hh_ref[...], wc, preferred_element_type=F32)
    for r in range(row_chunks):
        rows = slice(r * rm, (r + 1) * rm)
        hr = h_ref[rows, :]
        acc_c = jnp.dot(hr, wc, preferred_element_type=F32)
        acc_r = jnp.dot(hr, wr, preferred_element_type=F32)
        or_ref[rows, :] = (acc_r * cs_ref[...]).astype(or_ref.dtype)
        ext = jnp.concatenate([tail, acc_c], axis=0)
        y = ext[base:base + rm] * cw[0:1]
        for kk in range(1, kw):
            y = y + ext[base + kk:base + kk + rm] * cw[kk:kk + 1]
        oc_ref[rows, :] = _silu(y).astype(oc_ref.dtype)
        tail = acc_c[rm - halo:rm]


def _in_proj(x2d, norm_w, w_conv, w_rest, col_scale, w_small, conv_w, seq_len, tm=1024, steps=2, halo=16,
             row_chunks=4):
    t, d = x2d.shape
    nc, nr, ns = w_conv.shape[1], w_rest.shape[1], w_small.shape[1]
    kw = conv_w.shape[0]
    assert nc % (steps * LANES) == 0 and nr % (steps * LANES) == 0
    assert seq_len % tm == 0 and tm % halo == 0 and kw - 1 <= halo and conv_w.shape[1] == nc
    tc, tr = nc // steps, nr // steps
    row = lambda w: pl.BlockSpec((tm, w), lambda i, j: (i, j))
    col = lambda r, w: pl.BlockSpec((r, w), lambda i, j: (0, j))
    fixed = lambda r, w: pl.BlockSpec((r, w), lambda i, j: (0, 0))
    return pl.pallas_call(
        functools.partial(_in_proj_kernel, tiles_per_seq=seq_len // tm, row_chunks=row_chunks),
        grid=(t // tm, steps),
        in_specs=[pl.BlockSpec((tm, d), lambda i, j: (i, 0)),
                  pl.BlockSpec((halo, d), lambda i, j: (jnp.maximum(i * (tm // halo) - 1, 0), 0)),
                  fixed(1, d), col(d, tc), col(d, tr), col(1, tr), fixed(d, ns), col(kw, tc)],
        out_specs=[row(tc), row(tr), pl.BlockSpec((tm, ns), lambda i, j: (i, 0))],
        out_shape=[jax.ShapeDtypeStruct((t, nc), BF16), jax.ShapeDtypeStruct((t, nr), BF16),
                   jax.ShapeDtypeStruct((t, ns), F32)],
        scratch_shapes=[pltpu.VMEM((tm, d), BF16), pltpu.VMEM((halo, d), BF16)],
        compiler_params=_cparams(("parallel", "arbitrary")),
        name="in_proj",
    )(x2d, x2d, norm_w.reshape(1, d), w_conv, w_rest, col_scale, w_small, conv_w)


def _deltanet_kernel(q_ref, k_ref, v_ref, z_ref, sm_ref, alog_ref, dtb_ref, onw_ref,
                     o_ref, state_ref, vnew_ref, gct_ref, *, n_heads, heads_per_step):
    hg = pl.program_id(1)
    g = pl.program_id(2)
    G, C, D, NH = DN_GROUP, DN_CHUNK, LANES, heads_per_step
    lc = C.bit_length() - 1

    @pl.when(g == 0)
    def _():
        state_ref[...] = jnp.zeros_like(state_ref)

    def l2n(x):
        return x * lax.rsqrt(jnp.sum(x * x, axis=-1, keepdims=True) + NORM_EPS)

    NB = q_ref.shape[0]
    qf = [q_ref[bi].astype(F32) for bi in range(NB)]
    kf = [k_ref[bi].astype(F32) for bi in range(NB)]
    vf = [v_ref[bi].astype(F32) for bi in range(NB)]

    lane = lax.broadcasted_iota(jnp.int32, (G, LANES), 1)
    row = lax.broadcasted_iota(jnp.int32, (G, LANES), 0)
    pos = row & (C - 1)
    beta_all, gc_all, gl_all = [], [], []
    for bi in range(NB):
        cs = sm_ref[bi]
        beta_all.append(_sigmoid(cs))
        xg = cs + dtb_ref[...]
        softplus = jnp.maximum(xg, 0.0) + jnp.log1p(jnp.exp(-jnp.abs(xg)))
        gc = -jnp.exp(alog_ref[...]) * softplus
        shift = 1
        while shift < C:
            gc = gc + jnp.where(pos >= shift, pltpu.roll(gc, shift, 0), 0.0)
            shift *= 2
        gc_all.append(gc)
        gl_all.append(jnp.concatenate([jnp.broadcast_to(gc[c * C + C - 1:c * C + C], (C, LANES))
                                       for c in range(G // C)], axis=0))
        gct_ref[bi] = gc.T

    ri = lax.broadcasted_iota(jnp.int32, (G, G), 0)
    ci = lax.broadcasted_iota(jnp.int32, (G, G), 1)
    same = (ri >> lc) == (ci >> lc)
    tri_incl = same & (ri >= ci)
    tri_strict = same & (ri > ci)
    eye = jnp.where(ri == ci, 1.0, 0.0)
    ct = lax.broadcasted_iota(jnp.int32, (D, G), 1) >> lc

    def pair_mask(s):
        ls = s.bit_length() - 1
        return ((ri >> (ls + 1)) == (ci >> (ls + 1))) & (((ri >> ls) & 1) == 1) & (((ci >> ls) & 1) == 0)

    pair_masks = {}
    s = 1
    while s < C:
        pair_masks[s] = pair_mask(s)
        s *= 2

    heads = range(NB * NH)
    bis = [a // NH for a in heads]
    sls = [slice((a % NH) * D, (a % NH + 1) * D) for a in heads]
    hidx = [hg * NH + a % NH for a in heads]

    def col_of(x, lane_idx):
        return jnp.sum(jnp.where(lane == lane_idx, x, 0.0), axis=1, keepdims=True)

    q = [l2n(qf[bis[a]][:, sls[a]]) * (D ** -0.5) for a in heads]
    k = [l2n(kf[bis[a]][:, sls[a]]) for a in heads]
    beta_col = [col_of(beta_all[bis[a]], hidx[a]) for a in heads]
    gc_col = [col_of(gc_all[bis[a]], hidx[a] + n_heads) for a in heads]
    gl_col = [col_of(gl_all[bis[a]], hidx[a] + n_heads) for a in heads]
    gc_row = [gct_ref[bis[a], pl.ds(hidx[a] + n_heads, 1), :] for a in heads]
    decay = [jnp.where(tri_incl, jnp.exp(jnp.where(tri_incl, gc_col[a] - gc_row[a], 0.0)), 0.0) for a in heads]
    kb = [k[a] * beta_col[a] for a in heads]
    vb = [vf[bis[a]][:, sls[a]] * beta_col[a] for a in heads]
    k16 = [x.astype(BF16) for x in k]
    kk = [lax.dot_general(kb[a].astype(BF16), k16[a], NT_DIMS, preferred_element_type=F32) for a in heads]
    qk = [lax.dot_general(q[a].astype(BF16), k16[a], NT_DIMS, preferred_element_type=F32) for a in heads]
    lower = [jnp.where(tri_strict, kk[a] * decay[a], 0.0) for a in heads]
    attn = [jnp.where(tri_incl, qk[a] * decay[a], 0.0).astype(BF16) for a in heads]

    tinv = [eye - jnp.where(pair_masks[1], lower[a], 0.0) for a in heads]
    s = 2
    while s < C:
        t16 = [x.astype(BF16) for x in tinv]
        md = [jnp.dot(jnp.where(pair_masks[s], lower[a], 0.0).astype(BF16), t16[a], preferred_element_type=F32)
              for a in heads]
        tinv = [tinv[a] - jnp.dot(t16[a], md[a].astype(BF16), preferred_element_type=F32) for a in heads]
        s *= 2

    egc = [jnp.exp(x) for x in gc_col]
    uw = [jnp.dot(tinv[a].astype(BF16), jnp.concatenate([vb[a], kb[a] * egc[a]], axis=1).astype(BF16),
                  preferred_element_type=F32) for a in heads]
    q_dec = [q[a] * egc[a] for a in heads]
    k_dec_t = [(k[a] * jnp.exp(gl_col[a] - gc_col[a])).T for a in heads]
    egl = [jnp.exp(x) for x in gl_col]

    vnew_ref[...] = jnp.zeros_like(vnew_ref)
    outs = [[] for _ in heads]
    for c in range(G // C):
        r0 = c * C
        st = [state_ref[a] for a in heads]
        ws_qs = [jnp.dot(jnp.concatenate([uw[a][r0:r0 + C, D:], q_dec[a][r0:r0 + C]], axis=0).astype(BF16),
                         st[a].astype(BF16), preferred_element_type=F32) for a in heads]
        for a in heads:
            vnew_ref[a, r0:r0 + C, :] = (uw[a][r0:r0 + C, :D] - ws_qs[a][:C]).astype(BF16)
        vn = [vnew_ref[a] for a in heads]
        for a in heads:
            outs[a].append(ws_qs[a][C:] + jnp.dot(attn[a][r0:r0 + C], vn[a], preferred_element_type=F32))
        for a in heads:
            kt_c = jnp.where(ct == c, k_dec_t[a], 0.0).astype(BF16)
            e_c = jnp.concatenate([egl[a][r0:r0 + C]] * (D // C), axis=0)
            state_ref[a] = st[a] * e_c + jnp.dot(kt_c, vn[a], preferred_element_type=F32)

    for a in heads:
        o = jnp.concatenate(outs[a], axis=0)
        o = o * lax.rsqrt(jnp.mean(o * o, axis=-1, keepdims=True) + NORM_EPS) * onw_ref[...]
        z = z_ref[bis[a], :, sls[a]].astype(F32)
        o_ref[bis[a], :, sls[a]] = (o * _silu(z)).astype(o_ref.dtype)


def _deltanet(qkv3, z3, small3, a_log_row, dt_row, onorm_w, n_heads, col_q, col_k, col_v, col_z, heads_per_step=8,
              batch_rows_per_step=2):
    b, s, _ = qkv3.shape
    G, D, NH = DN_GROUP, LANES, heads_per_step
    NB = batch_rows_per_step if b % batch_rows_per_step == 0 else 1
    assert n_heads % NH == 0 and all(c % NH == 0 for c in (col_q, col_k, col_v, col_z))
    blk = lambda off: pl.BlockSpec((NB, G, NH * D), lambda bi, hi, gi: (bi, gi, off // NH + hi))
    row_spec = pl.BlockSpec((1, LANES), lambda bi, hi, gi: (0, 0))
    return pl.pallas_call(
        functools.partial(_deltanet_kernel, n_heads=n_heads, heads_per_step=NH),
        grid=(b // NB, n_heads // NH, s // G),
        in_specs=[blk(col_q), blk(col_k), blk(col_v), blk(col_z),
                  pl.BlockSpec((NB, G, LANES), lambda bi, hi, gi: (bi, gi, 0)),
                  row_spec, row_spec, row_spec],
        out_specs=pl.BlockSpec((NB, G, NH * D), lambda bi, hi, gi: (bi, gi, hi)),
        out_shape=jax.ShapeDtypeStruct((b, s, n_heads * D), BF16),
        scratch_shapes=[pltpu.VMEM((NB * NH, D, D), F32),
                        pltpu.VMEM((NB * NH, G, D), BF16), pltpu.VMEM((NB, LANES, G), F32)],
        compiler_params=_cparams(("parallel", "parallel", "arbitrary")),
        name="gated_deltanet",
    )(qkv3, qkv3, qkv3, z3, small3, a_log_row, dt_row, onorm_w)


def _t5_bucket_f32(dist):
    dist = jnp.maximum(dist, 0)
    max_exact = REL_BUCKETS // 2
    d = jnp.maximum(dist, 1).astype(F32)
    large = max_exact + (jnp.log(d / max_exact) / math.log(REL_MAX_DIST / max_exact)
                         * (REL_BUCKETS - max_exact)).astype(jnp.int32)
    large = jnp.minimum(large, REL_BUCKETS - 1)
    return jnp.where(dist < max_exact, dist, large)


def _n_near_tiles():
    return -(-(REL_MAX_DIST + MOBA_BLOCK - 1) // MOBA_BLOCK)


def _moba_kernel(rb_ref, q_ref, k_ref, v_ref, z_ref, o_ref, kmean_ref, bias_ref, kt_ref, xt_ref, qa_ref,
                 s0_ref, s1_ref, wide_ref, *, n_near, nbp, group, qblocks):
    hh = pl.program_id(0)
    bb = pl.program_id(1)
    BLK, D, U, QB = MOBA_BLOCK, LANES, group, qblocks
    R = QB * BLK
    nb = k_ref.shape[1] // BLK
    n_steps = nb // QB
    col_shift = nbp

    @pl.when(bb == 0)
    def _():
        qr = lax.broadcasted_iota(jnp.int32, (BLK, BLK), 0)
        kc = lax.broadcasted_iota(jnp.int32, (BLK, BLK), 1)
        for t in range(n_near):
            dist = t * BLK + qr - kc
            bucket = _t5_bucket_f32(dist)
            val = jnp.full((BLK, BLK), rb_ref[0, hh], F32)
            for bk in range(1, REL_BUCKETS):
                val = jnp.where(bucket == bk, rb_ref[bk, hh], val)
            if t == 0:
                val = jnp.where(dist >= 0, val, NEG_INF)
            bias_ref[t] = val
        bias_ref[n_near] = jnp.full((BLK, BLK), rb_ref[REL_BUCKETS - 1, hh], F32)
        ri = lax.broadcasted_iota(jnp.int32, (D, U * BLK), 0)
        ci = lax.broadcasted_iota(jnp.int32, (D, U * BLK), 1)
        for g in range(nb // U):
            hot = (ri == g * U + ci // BLK) | (ri == col_shift)
            kt_ref[g, D:2 * D, :] = jnp.where(hot, 1.0, 0.0).astype(BF16)

    ones_dd = jnp.ones((D, D), BF16)

    def norms(it, carry):
        k2, q2 = carry
        for u in range(U):
            j = it * U + u
            rows = pl.ds(pl.multiple_of(j * BLK, BLK), BLK)
            kj = k_ref[0, rows, :].astype(F32)
            qj = q_ref[0, rows, :].astype(F32)
            kmean_ref[pl.ds(j, 1), :] = jnp.mean(kj, axis=0, keepdims=True)
            kt_ref[it, 0:D, u * BLK:(u + 1) * BLK] = kj.T.astype(BF16)
            k2 = jnp.maximum(k2, jnp.dot((kj * kj).astype(BF16), ones_dd, preferred_element_type=F32))
            q2 = jnp.maximum(q2, jnp.dot((qj * qj).astype(BF16), ones_dd, preferred_element_type=F32))
        return k2, q2
    k2, q2 = lax.fori_loop(0, nb // U, norms, (jnp.zeros((BLK, D), F32), jnp.zeros((BLK, D), F32)))
    rb_abs = jnp.abs(rb_ref[0, hh])
    for bk in range(1, REL_BUCKETS):
        rb_abs = jnp.maximum(rb_abs, jnp.abs(rb_ref[bk, hh]))
    bound = jnp.sqrt(jnp.max(q2) * jnp.max(k2)) * 1.02 + rb_abs + 1e-3
    wide_ref[0] = jnp.where(bound > MOBA_SHIFT_SLACK, 1, 0).astype(jnp.int32)

    kmean16 = kmean_ref[...].astype(BF16)
    jrow = lax.broadcasted_iota(jnp.int32, (nb, BLK), 0)
    jrow_f = jrow.astype(F32)

    def visibility(it, carry):
        for u in range(U):
            ib = it * U + u
            rows = pl.ds(pl.multiple_of(ib * BLK, BLK), BLK)
            gate = lax.dot_general(kmean16, q_ref[0, rows, :], NT_DIMS, preferred_element_type=F32)
            past = jrow < ib
            sc = jnp.where(past, gate, NEG_INF)
            visible = jrow == ib
            for _ in range(MOBA_TOPK):
                top = jnp.max(sc, axis=0, keepdims=True)
                first = jnp.min(jnp.where(sc == top, jrow_f, float(nb)), axis=0, keepdims=True)
                pick = jrow_f == first
                visible = visible | (pick & past)
                sc = jnp.where(pick, jnp.finfo(F32).min, sc)
            cols = jnp.concatenate([jnp.where(visible, 0.0, NEG_INF), jnp.zeros((D - nb, BLK), F32)], axis=0)
            xt_ref[rows, :] = cols.T.astype(BF16)
        return carry
    lax.fori_loop(0, nb // U, visibility, 0)

    def set_query_operand(s, slot):
        rows = pl.ds(pl.multiple_of(s * R, R), R)
        qa_ref[slot, :, 0:D] = q_ref[0, rows, :]
        qa_ref[slot, :, D:2 * D] = xt_ref[rows, :]

    def pair_scores(s, g, slot):
        sc = jnp.dot(qa_ref[slot], kt_ref[g], preferred_element_type=F32)
        bias = jnp.concatenate(
            [jnp.concatenate([bias_ref[jnp.clip(s * QB + a - (g * U + u), 0, n_near)] for u in range(U)], axis=1)
             for a in range(QB)], axis=0)
        return sc + bias

    def lane_chunks(x):
        return [x[:, c * D:(c + 1) * D] for c in range(x.shape[1] // D)]

    @pl.when(wide_ref[0] == 1)
    def _():
        def step_max(s, carry):
            set_query_operand(s, 0)

            def max_step(g, mm):
                for c in lane_chunks(pair_scores(s, g, 0)):
                    mm = jnp.maximum(mm, c)
                return mm
            mm = lax.fori_loop(0, s + 1, max_step, jnp.full((R, D), jnp.finfo(F32).min, F32))
            m = jnp.max(mm, axis=1, keepdims=True)
            rows = pl.ds(pl.multiple_of(s * R, R), R)
            lane_q = lax.broadcasted_iota(jnp.int32, (R, D), 1)
            xt_ref[rows, :] = jnp.where(lane_q == col_shift, -m, xt_ref[rows, :].astype(F32)).astype(BF16)
            return carry
        lax.fori_loop(0, n_steps, step_max, 0)

    ones_cols = jnp.ones((U * BLK, D), BF16)

    def value_operand(g):
        vg = v_ref[0, pl.ds(pl.multiple_of(g * (U * BLK), U * BLK), U * BLK), :]
        return jnp.concatenate([vg, ones_cols], axis=1)

    def consume(vg, sc, acc):
        return acc + jnp.dot(jnp.exp(sc).astype(BF16), vg, preferred_element_type=F32)

    def finish_step(s, g, acc):
        @pl.when(g == s)
        def _():
            rows = pl.ds(pl.multiple_of(s * R, R), R)
            out = acc[:, 0:D] / acc[:, D:D + 1]
            z = z_ref[0, rows, :].astype(F32)
            o_ref[0, rows, :] = (out * _silu(z)).astype(o_ref.dtype)
        return acc * jnp.where(g == s, 0.0, 1.0).astype(F32)

    def stage(cur_ref, nxt_ref, carry):
        s, g, acc = carry
        wrap = g == s
        s2 = jnp.where(wrap, s + 1, s)
        g2 = jnp.where(wrap, 0, g + 1)

        @pl.when(wrap)
        def _():
            set_query_operand(s2, s2 & 1)

        nxt_ref[...] = pair_scores(s2, g2, s2 & 1)
        acc = consume(value_operand(g), cur_ref[...], acc)
        return s2, g2, finish_step(s, g, acc)

    def two_stages(t, carry):
        return stage(s1_ref, s0_ref, stage(s0_ref, s1_ref, carry))

    n_pairs = n_steps * (n_steps + 1) // 2
    set_query_operand(0, 0)
    s0_ref[...] = pair_scores(0, 0, 0)
    carry = (jnp.int32(0), jnp.int32(0), jnp.zeros((R, 2 * D), F32))
    carry = lax.fori_loop(0, (n_pairs - 1) // 2, two_stages, carry)
    last_ref = s0_ref
    if (n_pairs - 1) % 2:
        carry = stage(s0_ref, s1_ref, carry)
        last_ref = s1_ref
    s, g, acc = carry
    finish_step(s, g, consume(value_operand(g), last_ref[...], acc))


def _moba(rel_bias, p3, n_heads, col_q, col_k, col_v, col_z, group=4, qblocks=4):
    b, s, _ = p3.shape
    BLK, D = MOBA_BLOCK, LANES
    nb = s // BLK
    nbp = -(-nb // 16) * 16
    assert nbp + 16 <= D, "block-visibility columns must fit in the spare contraction columns"
    assert group == qblocks and nb % group == 0, "step s visits key groups 0..s"
    n_near = _n_near_tiles()
    rows = qblocks * BLK
    seq = lambda off: pl.BlockSpec((1, s, D), lambda hi, bi: (bi, 0, off + hi))
    return pl.pallas_call(
        functools.partial(_moba_kernel, n_near=n_near, nbp=nbp, group=group, qblocks=qblocks),
        grid=(n_heads, b),
        in_specs=[pl.BlockSpec(memory_space=pltpu.SMEM), seq(col_q), seq(col_k), seq(col_v), seq(col_z)],
        out_specs=seq(0),
        out_shape=jax.ShapeDtypeStruct((b, s, n_heads * D), BF16),
        scratch_shapes=[pltpu.VMEM((nb, D), F32), pltpu.VMEM((n_near + 1, BLK, BLK), F32),
                        pltpu.VMEM((nb // group, 2 * D, group * BLK), BF16), pltpu.VMEM((s, D), BF16),
                        pltpu.VMEM((2, rows, 2 * D), BF16),
                        pltpu.VMEM((rows, group * BLK), F32), pltpu.VMEM((rows, group * BLK), F32),
                        pltpu.SMEM((1,), jnp.int32)],
        compiler_params=_cparams(("arbitrary", "arbitrary")),
        name="moba_attention",
    )(rel_bias, p3, p3, p3, p3)


def _merge_kernel(ya_ref, yb_ref, g0_ref, g1_ref, x_ref, w0_ref, w1_ref, wo_ref, b0_ref, b1_ref, nw_ref, o_ref):
    pa = jnp.dot(ya_ref[...], w0_ref[...], preferred_element_type=F32)
    pb = jnp.dot(yb_ref[...], w1_ref[...], preferred_element_type=F32)
    g0 = _sigmoid(g0_ref[...].astype(F32) + b0_ref[...])
    g1 = _sigmoid(g1_ref[...].astype(F32) + b1_ref[...])
    merged = (g0 * pa + g1 * pb).astype(BF16)
    out = jnp.dot(merged, wo_ref[...], preferred_element_type=F32)
    y = out * lax.rsqrt(jnp.mean(out * out, axis=-1, keepdims=True) + NORM_EPS) * nw_ref[...]
    o_ref[...] = x_ref[...] + y


def _merge(ya, yb, p2, x2d, w0, w1, wo, b_gate, norm_w, col_g0, col_g1, tm=512):
    t, d = x2d.shape
    row = lambda c: pl.BlockSpec((tm, d), lambda i: (i, c))
    full = pl.BlockSpec((d, d), lambda i: (0, 0))
    vec = lambda c: pl.BlockSpec((1, d), lambda i: (0, c))
    return pl.pallas_call(
        _merge_kernel,
        grid=(t // tm,),
        in_specs=[row(0), row(0), row(col_g0), row(col_g1), row(0), full, full, full, vec(0), vec(1), vec(0)],
        out_specs=row(0),
        out_shape=jax.ShapeDtypeStruct((t, d), F32),
        compiler_params=_cparams(("parallel",)),
        name="merge_out",
    )(ya, yb, p2, p2, x2d, w0, w1, wo, b_gate.reshape(1, -1), b_gate.reshape(1, -1), norm_w.reshape(1, d))


def kernel(x, norm_pre_w, w_in, b_gate, conv_w, dn_a_log, dn_dt_bias, dn_onorm_w, rel_bias, w_branch, w_out,
           norm_post_w):
    b, s, d = x.shape
    n_heads = dn_a_log.shape[0]
    hd = dn_onorm_w.shape[0]
    width = n_heads * hd
    assert hd == LANES and rel_bias.shape[1] == n_heads and width == d
    assert s % MOBA_BLOCK == 0 and s % DN_GROUP == 0 and 2 * n_heads <= LANES
    t = b * s

    c_za = 3 * width
    c_beta = 4 * width
    c_qb = c_beta + 2 * n_heads
    assert conv_w.shape[1] == c_za
    w16 = w_in.astype(BF16)
    w_qkv_a = w16[:, :c_za]
    w_rest = jnp.concatenate([w16[:, c_za:c_beta], w16[:, c_qb:]], axis=1)
    w_small = jnp.pad(w16[:, c_beta:c_qb], ((0, 0), (0, LANES - 2 * n_heads)))
    nblk = width // LANES
    col = lambda k: k * nblk

    x2d = x.reshape(t, d)
    norm_w = norm_pre_w.astype(F32)
    col_scale = jnp.concatenate([jnp.ones((1, width), F32), jnp.full((1, width), hd ** -0.5, F32),
                                 jnp.ones((1, w_rest.shape[1] - 2 * width), F32)], axis=1)
    qkv_a, rest, small = _in_proj(x2d, norm_w, w_qkv_a, w_rest, col_scale, w_small, conv_w.astype(F32), s)
    qkv_a3 = qkv_a.reshape(b, s, -1)
    rest3 = rest.reshape(b, s, -1)

    pad = (0, LANES - 2 * n_heads)
    a_log_row = jnp.pad(jnp.concatenate([jnp.zeros_like(dn_a_log), dn_a_log]), pad).reshape(1, LANES).astype(F32)
    dt_row = jnp.pad(jnp.concatenate([jnp.zeros_like(dn_dt_bias), dn_dt_bias]), pad).reshape(1, LANES).astype(F32)
    ya = _deltanet(qkv_a3, rest3, small.reshape(b, s, LANES), a_log_row, dt_row,
                   dn_onorm_w.reshape(1, hd).astype(F32), n_heads, col(0), col(1), col(2), col(0))
    yb = _moba(rel_bias.astype(F32), rest3, n_heads, col(1), col(2), col(3), col(4))

    out = _merge(ya.reshape(t, d), yb.reshape(t, d), rest, x2d,
                 w_branch[0].astype(BF16), w_branch[1].astype(BF16), w_out.astype(BF16),
                 b_gate.astype(F32), norm_post_w.astype(F32), 5, 6)
    return out.reshape(b, s, d)
```

```python
import functools
import math

import jax
import jax.numpy as jnp
from jax import lax
from jax.experimental import pallas as pl
from jax.experimental.pallas import tpu as pltpu

F32 = jnp.float32
BF16 = jnp.bfloat16

LANES = 128
DN_CHUNK = 64
DN_GROUP = 128
DN_CONV_HALO = 8
MOBA_BLOCK = 256
MOBA_TOPK = 3
MOBA_SHIFT_SLACK = 40.0
REL_BUCKETS = 32
REL_MAX_DIST = 2048
NORM_EPS = 1e-6
NEG_INF = -1e30
VMEM_LIMIT = 56 * 1024 * 1024
NT_DIMS = (((1,), (1,)), ((), ()))


def _cparams(sem):
    return pltpu.CompilerParams(dimension_semantics=sem, vmem_limit_bytes=VMEM_LIMIT)


def _sigmoid(x):
    return 0.5 * jnp.tanh(0.5 * x) + 0.5


def _silu(x):
    h = 0.5 * x
    return h * jnp.tanh(h) + h


def _normalise(x, w):
    y = x * lax.rsqrt(jnp.mean(x * x, axis=-1, keepdims=True) + NORM_EPS)
    return (y * w).astype(BF16)


def _in_proj_kernel(x_ref, xh_ref, nw_ref, wc_ref, wr_ref, cs_ref, ws_ref, cw_ref, oc_ref, or_ref, sm_ref,
                    h_ref, hh_ref, *, tiles_per_seq, row_chunks):
    i = pl.program_id(0)
    tm = oc_ref.shape[0]
    halo = hh_ref.shape[0]

    @pl.when(pl.program_id(1) == 0)
    def _():
        h = _normalise(x_ref[...], nw_ref[...])
        h_ref[...] = h
        sm_ref[...] = jnp.dot(h, ws_ref[...], preferred_element_type=F32)
        hh = _normalise(xh_ref[...], nw_ref[...])
        hh_ref[...] = jnp.where(i % tiles_per_seq == 0, jnp.zeros_like(hh), hh)

    wc = wc_ref[...]
    wr = wr_ref[...]
    cw = cw_ref[...]
    kw = cw.shape[0]
    base = halo - (kw - 1)
    rm = tm // row_chunks
    tail = jnp.dot(hh_ref[...], wc, preferred_element_type=F32)
    for r in range(row_chunks):
        rows = slice(r * rm, (r + 1) * rm)
        hr = h_ref[rows, :]
        acc_c = jnp.dot(hr, wc, preferred_element_type=F32)
        acc_r = jnp.dot(hr, wr, preferred_element_type=F32)
        or_ref[rows, :] = (acc_r * cs_ref[...]).astype(or_ref.dtype)
        ext = jnp.concatenate([tail, acc_c], axis=0)
        y = ext[base:base + rm] * cw[0:1]
        for kk in range(1, kw):
            y = y + ext[base + kk:base + kk + rm] * cw[kk:kk + 1]
        oc_ref[rows, :] = _silu(y).astype(oc_ref.dtype)
        tail = acc_c[rm - halo:rm]


def _in_proj(x2d, norm_w, w_conv, w_rest, col_scale, w_small, conv_w, seq_len, tm=1024, steps=4, halo=16,
             row_chunks=4):
    t, d = x2d.shape
    nc, nr, ns = w_conv.shape[1], w_rest.shape[1], w_small.shape[1]
    kw = conv_w.shape[0]
    assert nc % (steps * LANES) == 0 and nr % (steps * LANES) == 0
    assert seq_len % tm == 0 and tm % halo == 0 and kw - 1 <= halo and conv_w.shape[1] == nc
    tc, tr = nc // steps, nr // steps
    row = lambda w: pl.BlockSpec((tm, w), lambda i, j: (i, j))
    col = lambda r, w: pl.BlockSpec((r, w), lambda i, j: (0, j))
    fixed = lambda r, w: pl.BlockSpec((r, w), lambda i, j: (0, 0))
    return pl.pallas_call(
        functools.partial(_in_proj_kernel, tiles_per_seq=seq_len // tm, row_chunks=row_chunks),
        grid=(t // tm, steps),
        in_specs=[pl.BlockSpec((tm, d), lambda i, j: (i, 0)),
                  pl.BlockSpec((halo, d), lambda i, j: (jnp.maximum(i * (tm // halo) - 1, 0), 0)),
                  fixed(1, d), col(d, tc), col(d, tr), col(1, tr), fixed(d, ns), col(kw, tc)],
        out_specs=[row(tc), row(tr), pl.BlockSpec((tm, ns), lambda i, j: (i, 0))],
        out_shape=[jax.ShapeDtypeStruct((t, nc), BF16), jax.ShapeDtypeStruct((t, nr), BF16),
                   jax.ShapeDtypeStruct((t, ns), F32)],
        scratch_shapes=[pltpu.VMEM((tm, d), BF16), pltpu.VMEM((halo, d), BF16)],
        compiler_params=_cparams(("parallel", "arbitrary")),
        name="in_proj",
    )(x2d, x2d, norm_w.reshape(1, d), w_conv, w_rest, col_scale, w_small, conv_w)


def _deltanet_kernel(q_ref, k_ref, v_ref, z_ref, sm_ref, alog_ref, dtb_ref, onw_ref, wb_ref,
                     o_ref, state_ref, vnew_ref, gct_ref, *, n_heads, heads_per_step):
    hg = pl.program_id(1)
    g = pl.program_id(2)
    G, C, D, NH = DN_GROUP, DN_CHUNK, LANES, heads_per_step
    lc = C.bit_length() - 1

    @pl.when(g == 0)
    def _():
        state_ref[...] = jnp.zeros_like(state_ref)

    def l2n(x):
        return x * lax.rsqrt(jnp.sum(x * x, axis=-1, keepdims=True) + NORM_EPS)

    NB = q_ref.shape[0]
    qf = [q_ref[bi].astype(F32) for bi in range(NB)]
    kf = [k_ref[bi].astype(F32) for bi in range(NB)]
    vf = [v_ref[bi].astype(F32) for bi in range(NB)]

    lane = lax.broadcasted_iota(jnp.int32, (G, LANES), 1)
    row = lax.broadcasted_iota(jnp.int32, (G, LANES), 0)
    pos = row & (C - 1)
    beta_all, gc_all, gl_all = [], [], []
    for bi in range(NB):
        cs = sm_ref[bi]
        beta_all.append(_sigmoid(cs))
        xg = cs + dtb_ref[...]
        softplus = jnp.maximum(xg, 0.0) + jnp.log1p(jnp.exp(-jnp.abs(xg)))
        gc = -jnp.exp(alog_ref[...]) * softplus
        shift = 1
        while shift < C:
            gc = gc + jnp.where(pos >= shift, pltpu.roll(gc, shift, 0), 0.0)
            shift *= 2
        gc_all.append(gc)
        gl_all.append(jnp.concatenate([jnp.broadcast_to(gc[c * C + C - 1:c * C + C], (C, LANES))
                                       for c in range(G // C)], axis=0))
        gct_ref[bi] = gc.T

    ri = lax.broadcasted_iota(jnp.int32, (G, G), 0)
    ci = lax.broadcasted_iota(jnp.int32, (G, G), 1)
    same = (ri >> lc) == (ci >> lc)
    tri_incl = same & (ri >= ci)
    tri_strict = same & (ri > ci)
    eye = jnp.where(ri == ci, 1.0, 0.0)
    ct = lax.broadcasted_iota(jnp.int32, (D, G), 1) >> lc

    def pair_mask(s):
        ls = s.bit_length() - 1
        return ((ri >> (ls + 1)) == (ci >> (ls + 1))) & (((ri >> ls) & 1) == 1) & (((ci >> ls) & 1) == 0)

    pair_masks = {}
    s = 1
    while s < C:
        pair_masks[s] = pair_mask(s)
        s *= 2

    heads = range(NB * NH)
    bis = [a // NH for a in heads]
    sls = [slice((a % NH) * D, (a % NH + 1) * D) for a in heads]
    hidx = [hg * NH + a % NH for a in heads]

    def col_of(x, lane_idx):
        return jnp.sum(jnp.where(lane == lane_idx, x, 0.0), axis=1, keepdims=True)

    q = [l2n(qf[bis[a]][:, sls[a]]) * (D ** -0.5) for a in heads]
    k = [l2n(kf[bis[a]][:, sls[a]]) for a in heads]
    beta_col = [col_of(beta_all[bis[a]], hidx[a]) for a in heads]
    gc_col = [col_of(gc_all[bis[a]], hidx[a] + n_heads) for a in heads]
    gl_col = [col_of(gl_all[bis[a]], hidx[a] + n_heads) for a in heads]
    gc_row = [gct_ref[bis[a], pl.ds(hidx[a] + n_heads, 1), :] for a in heads]
    decay = [jnp.where(tri_incl, jnp.exp(jnp.where(tri_incl, gc_col[a] - gc_row[a], 0.0)), 0.0) for a in heads]
    kb = [k[a] * beta_col[a] for a in heads]
    vb = [vf[bis[a]][:, sls[a]] * beta_col[a] for a in heads]
    k16 = [x.astype(BF16) for x in k]
    kk = [lax.dot_general(kb[a].astype(BF16), k16[a], NT_DIMS, preferred_element_type=F32) for a in heads]
    qk = [lax.dot_general(q[a].astype(BF16), k16[a], NT_DIMS, preferred_element_type=F32) for a in heads]
    lower = [jnp.where(tri_strict, kk[a] * decay[a], 0.0) for a in heads]
    attn = [jnp.where(tri_incl, qk[a] * decay[a], 0.0).astype(BF16) for a in heads]

    tinv = [eye - jnp.where(pair_masks[1], lower[a], 0.0) for a in heads]
    s = 2
    while s < C:
        t16 = [x.astype(BF16) for x in tinv]
        md = [jnp.dot(jnp.where(pair_masks[s], lower[a], 0.0).astype(BF16), t16[a], preferred_element_type=F32)
              for a in heads]
        tinv = [tinv[a] - jnp.dot(t16[a], md[a].astype(BF16), preferred_element_type=F32) for a in heads]
        s *= 2

    egc = [jnp.exp(x) for x in gc_col]
    uw = [jnp.dot(tinv[a].astype(BF16), jnp.concatenate([vb[a], kb[a] * egc[a]], axis=1).astype(BF16),
                  preferred_element_type=F32) for a in heads]
    q_dec = [q[a] * egc[a] for a in heads]
    k_dec_t = [(k[a] * jnp.exp(gl_col[a] - gc_col[a])).T for a in heads]
    egl = [jnp.exp(x) for x in gl_col]

    vnew_ref[...] = jnp.zeros_like(vnew_ref)
    outs = [[] for _ in heads]
    for c in range(G // C):
        r0 = c * C
        st = [state_ref[a] for a in heads]
        ws_qs = [jnp.dot(jnp.concatenate([uw[a][r0:r0 + C, D:], q_dec[a][r0:r0 + C]], axis=0).astype(BF16),
                         st[a].astype(BF16), preferred_element_type=F32) for a in heads]
        for a in heads:
            vnew_ref[a, r0:r0 + C, :] = (uw[a][r0:r0 + C, :D] - ws_qs[a][:C]).astype(BF16)
        vn = [vnew_ref[a] for a in heads]
        for a in heads:
            outs[a].append(ws_qs[a][C:] + jnp.dot(attn[a][r0:r0 + C], vn[a], preferred_element_type=F32))
        for a in heads:
            kt_c = jnp.where(ct == c, k_dec_t[a], 0.0).astype(BF16)
            e_c = jnp.concatenate([egl[a][r0:r0 + C]] * (D // C), axis=0)
            state_ref[a] = st[a] * e_c + jnp.dot(kt_c, vn[a], preferred_element_type=F32)

    ys = []
    for a in heads:
        o = jnp.concatenate(outs[a], axis=0)
        o = o * lax.rsqrt(jnp.mean(o * o, axis=-1, keepdims=True) + NORM_EPS) * onw_ref[...]
        z = z_ref[bis[a], :, sls[a]].astype(F32)
        ys.append((o * _silu(z)).astype(BF16))
    for bi in range(NB):
        y = jnp.concatenate(ys[bi * NH:(bi + 1) * NH], axis=1)
        o_ref[bi] = jnp.dot(y, wb_ref[...], preferred_element_type=F32).astype(o_ref.dtype)


def _deltanet(qkv3, z3, small3, a_log_row, dt_row, onorm_w, w_branch, n_heads, col_q, col_k, col_v, col_z,
              heads_per_step=8, batch_rows_per_step=2):
    b, s, _ = qkv3.shape
    G, D, NH = DN_GROUP, LANES, heads_per_step
    NB = batch_rows_per_step if b % batch_rows_per_step == 0 else 1
    assert n_heads == NH and all(c % NH == 0 for c in (col_q, col_k, col_v, col_z))
    assert w_branch.shape == (NH * D, NH * D)
    blk = lambda off: pl.BlockSpec((NB, G, NH * D), lambda bi, hi, gi: (bi, gi, off // NH + hi))
    row_spec = pl.BlockSpec((1, LANES), lambda bi, hi, gi: (0, 0))
    return pl.pallas_call(
        functools.partial(_deltanet_kernel, n_heads=n_heads, heads_per_step=NH),
        grid=(b // NB, n_heads // NH, s // G),
        in_specs=[blk(col_q), blk(col_k), blk(col_v), blk(col_z),
                  pl.BlockSpec((NB, G, LANES), lambda bi, hi, gi: (bi, gi, 0)),
                  row_spec, row_spec, row_spec, pl.BlockSpec((NH * D, NH * D), lambda bi, hi, gi: (0, 0))],
        out_specs=pl.BlockSpec((NB, G, NH * D), lambda bi, hi, gi: (bi, gi, hi)),
        out_shape=jax.ShapeDtypeStruct((b, s, n_heads * D), BF16),
        scratch_shapes=[pltpu.VMEM((NB * NH, D, D), F32),
                        pltpu.VMEM((NB * NH, G, D), BF16), pltpu.VMEM((NB, LANES, G), F32)],
        compiler_params=_cparams(("parallel", "parallel", "arbitrary")),
        name="gated_deltanet",
    )(qkv3, qkv3, qkv3, z3, small3, a_log_row, dt_row, onorm_w, w_branch)


def _t5_bucket_f32(dist):
    dist = jnp.maximum(dist, 0)
    max_exact = REL_BUCKETS // 2
    d = jnp.maximum(dist, 1).astype(F32)
    large = max_exact + (jnp.log(d / max_exact) / math.log(REL_MAX_DIST / max_exact)
                         * (REL_BUCKETS - max_exact)).astype(jnp.int32)
    large = jnp.minimum(large, REL_BUCKETS - 1)
    return jnp.where(dist < max_exact, dist, large)


def _n_near_tiles():
    return -(-(REL_MAX_DIST + MOBA_BLOCK - 1) // MOBA_BLOCK)


def _moba_kernel(rb_ref, q_ref, k_ref, v_ref, z_ref, o_ref, kmean_ref, bias_ref, kt_ref, xt_ref, qa_ref,
                 s0_ref, s1_ref, wide_ref, *, n_near, nbp, group, qblocks):
    hh = pl.program_id(0)
    bb = pl.program_id(1)
    BLK, D, U, QB = MOBA_BLOCK, LANES, group, qblocks
    R = QB * BLK
    nb = k_ref.shape[1] // BLK
    n_steps = nb // QB
    col_shift = nbp

    @pl.when(bb == 0)
    def _():
        qr = lax.broadcasted_iota(jnp.int32, (BLK, BLK), 0)
        kc = lax.broadcasted_iota(jnp.int32, (BLK, BLK), 1)
        for t in range(n_near):
            dist = t * BLK + qr - kc
            bucket = _t5_bucket_f32(dist)
            val = jnp.full((BLK, BLK), rb_ref[0, hh], F32)
            for bk in range(1, REL_BUCKETS):
                val = jnp.where(bucket == bk, rb_ref[bk, hh], val)
            if t == 0:
                val = jnp.where(dist >= 0, val, NEG_INF)
            bias_ref[t] = val
        bias_ref[n_near] = jnp.full((BLK, BLK), rb_ref[REL_BUCKETS - 1, hh], F32)
        ri = lax.broadcasted_iota(jnp.int32, (D, U * BLK), 0)
        ci = lax.broadcasted_iota(jnp.int32, (D, U * BLK), 1)
        for g in range(nb // U):
            hot = (ri == g * U + ci // BLK) | (ri == col_shift)
            kt_ref[g, D:2 * D, :] = jnp.where(hot, 1.0, 0.0).astype(BF16)

    ones_dd = jnp.ones((D, D), BF16)

    def norms(it, carry):
        k2, q2 = carry
        for u in range(U):
            j = it * U + u
            rows = pl.ds(pl.multiple_of(j * BLK, BLK), BLK)
            kj = k_ref[0, rows, :].astype(F32)
            qj = q_ref[0, rows, :].astype(F32)
            kmean_ref[pl.ds(j, 1), :] = jnp.mean(kj, axis=0, keepdims=True)
            kt_ref[it, 0:D, u * BLK:(u + 1) * BLK] = kj.T.astype(BF16)
            k2 = jnp.maximum(k2, jnp.dot((kj * kj).astype(BF16), ones_dd, preferred_element_type=F32))
            q2 = jnp.maximum(q2, jnp.dot((qj * qj).astype(BF16), ones_dd, preferred_element_type=F32))
        return k2, q2
    k2, q2 = lax.fori_loop(0, nb // U, norms, (jnp.zeros((BLK, D), F32), jnp.zeros((BLK, D), F32)))
    rb_abs = jnp.abs(rb_ref[0, hh])
    for bk in range(1, REL_BUCKETS):
        rb_abs = jnp.maximum(rb_abs, jnp.abs(rb_ref[bk, hh]))
    bound = jnp.sqrt(jnp.max(q2) * jnp.max(k2)) * 1.02 + rb_abs + 1e-3
    wide_ref[0] = jnp.where(bound > MOBA_SHIFT_SLACK, 1, 0).astype(jnp.int32)

    kmean16 = kmean_ref[...].astype(BF16)
    jrow = lax.broadcasted_iota(jnp.int32, (nb, BLK), 0)
    jrow_f = jrow.astype(F32)

    def visibility(it, carry):
        for u in range(U):
            ib = it * U + u
            rows = pl.ds(pl.multiple_of(ib * BLK, BLK), BLK)
            gate = lax.dot_general(kmean16, q_ref[0, rows, :], NT_DIMS, preferred_element_type=F32)
            past = jrow < ib
            sc = jnp.where(past, gate, NEG_INF)
            visible = jrow == ib
            for _ in range(MOBA_TOPK):
                top = jnp.max(sc, axis=0, keepdims=True)
                first = jnp.min(jnp.where(sc == top, jrow_f, float(nb)), axis=0, keepdims=True)
                pick = jrow_f == first
                visible = visible | (pick & past)
                sc = jnp.where(pick, jnp.finfo(F32).min, sc)
            cols = jnp.concatenate([jnp.where(visible, 0.0, NEG_INF), jnp.zeros((D - nb, BLK), F32)], axis=0)
            xt_ref[rows, :] = cols.T.astype(BF16)
        return carry
    lax.fori_loop(0, nb // U, visibility, 0)

    def set_query_operand(s, slot):
        rows = pl.ds(pl.multiple_of(s * R, R), R)
        qa_ref[slot, :, 0:D] = q_ref[0, rows, :]
        qa_ref[slot, :, D:2 * D] = xt_ref[rows, :]

    def pair_scores(s, g, slot):
        sc = jnp.dot(qa_ref[slot], kt_ref[g], preferred_element_type=F32)
        bias = jnp.concatenate(
            [jnp.concatenate([bias_ref[jnp.clip(s * QB + a - (g * U + u), 0, n_near)] for u in range(U)], axis=1)
             for a in range(QB)], axis=0)
        return sc + bias

    def lane_chunks(x):
        return [x[:, c * D:(c + 1) * D] for c in range(x.shape[1] // D)]

    @pl.when(wide_ref[0] == 1)
    def _():
        def step_max(s, carry):
            set_query_operand(s, 0)

            def max_step(g, mm):
                for c in lane_chunks(pair_scores(s, g, 0)):
                    mm = jnp.maximum(mm, c)
                return mm
            mm = lax.fori_loop(0, s + 1, max_step, jnp.full((R, D), jnp.finfo(F32).min, F32))
            m = jnp.max(mm, axis=1, keepdims=True)
            rows = pl.ds(pl.multiple_of(s * R, R), R)
            lane_q = lax.broadcasted_iota(jnp.int32, (R, D), 1)
            xt_ref[rows, :] = jnp.where(lane_q == col_shift, -m, xt_ref[rows, :].astype(F32)).astype(BF16)
            return carry
        lax.fori_loop(0, n_steps, step_max, 0)

    ones_cols = jnp.ones((U * BLK, D), BF16)

    def value_operand(g):
        vg = v_ref[0, pl.ds(pl.multiple_of(g * (U * BLK), U * BLK), U * BLK), :]
        return jnp.concatenate([vg, ones_cols], axis=1)

    def consume(vg, sc, acc):
        return acc + jnp.dot(jnp.exp(sc).astype(BF16), vg, preferred_element_type=F32)

    def finish_step(s, g, acc):
        @pl.when(g == s)
        def _():
            rows = pl.ds(pl.multiple_of(s * R, R), R)
            out = acc[:, 0:D] / acc[:, D:D + 1]
            z = z_ref[0, rows, :].astype(F32)
            o_ref[0, rows, :] = (out * _silu(z)).astype(o_ref.dtype)
        return acc * jnp.where(g == s, 0.0, 1.0).astype(F32)

    def stage(cur_ref, nxt_ref, carry):
        s, g, acc = carry
        wrap = g == s
        s2 = jnp.where(wrap, s + 1, s)
        g2 = jnp.where(wrap, 0, g + 1)

        @pl.when(wrap)
        def _():
            set_query_operand(s2, s2 & 1)

        nxt_ref[...] = pair_scores(s2, g2, s2 & 1)
        acc = consume(value_operand(g), cur_ref[...], acc)
        return s2, g2, finish_step(s, g, acc)

    def two_stages(t, carry):
        return stage(s1_ref, s0_ref, stage(s0_ref, s1_ref, carry))

    n_pairs = n_steps * (n_steps + 1) // 2
    set_query_operand(0, 0)
    s0_ref[...] = pair_scores(0, 0, 0)
    carry = (jnp.int32(0), jnp.int32(0), jnp.zeros((R, 2 * D), F32))
    carry = lax.fori_loop(0, (n_pairs - 1) // 2, two_stages, carry)
    last_ref = s0_ref
    if (n_pairs - 1) % 2:
        carry = stage(s0_ref, s1_ref, carry)
        last_ref = s1_ref
    s, g, acc = carry
    finish_step(s, g, consume(value_operand(g), last_ref[...], acc))


def _moba(rel_bias, p3, n_heads, col_q, col_k, col_v, col_z, group=4, qblocks=4):
    b, s, _ = p3.shape
    BLK, D = MOBA_BLOCK, LANES
    nb = s // BLK
    nbp = -(-nb // 16) * 16
    assert nbp + 16 <= D, "block-visibility columns must fit in the spare contraction columns"
    assert group == qblocks and nb % group == 0, "step s visits key groups 0..s"
    n_near = _n_near_tiles()
    rows = qblocks * BLK
    seq = lambda off: pl.BlockSpec((1, s, D), lambda hi, bi: (bi, 0, off + hi))
    return pl.pallas_call(
        functools.partial(_moba_kernel, n_near=n_near, nbp=nbp, group=group, qblocks=qblocks),
        grid=(n_heads, b),
        in_specs=[pl.BlockSpec(memory_space=pltpu.SMEM), seq(col_q), seq(col_k), seq(col_v), seq(col_z)],
        out_specs=seq(0),
        out_shape=jax.ShapeDtypeStruct((b, s, n_heads * D), BF16),
        scratch_shapes=[pltpu.VMEM((nb, D), F32), pltpu.VMEM((n_near + 1, BLK, BLK), F32),
                        pltpu.VMEM((nb // group, 2 * D, group * BLK), BF16), pltpu.VMEM((s, D), BF16),
                        pltpu.VMEM((2, rows, 2 * D), BF16),
                        pltpu.VMEM((rows, group * BLK), F32), pltpu.VMEM((rows, group * BLK), F32),
                        pltpu.SMEM((1,), jnp.int32)],
        compiler_params=_cparams(("arbitrary", "arbitrary")),
        name="moba_attention",
    )(rel_bias, p3, p3, p3, p3)


def _merge_kernel(pa_ref, yb_ref, g0_ref, g1_ref, x_ref, w1_ref, wo_ref, b0_ref, b1_ref, nw_ref, o_ref):
    pa = pa_ref[...].astype(F32)
    pb = jnp.dot(yb_ref[...], w1_ref[...], preferred_element_type=F32)
    g0 = _sigmoid(g0_ref[...].astype(F32) + b0_ref[...])
    g1 = _sigmoid(g1_ref[...].astype(F32) + b1_ref[...])
    merged = (g0 * pa + g1 * pb).astype(BF16)
    out = jnp.dot(merged, wo_ref[...], preferred_element_type=F32)
    y = out * lax.rsqrt(jnp.mean(out * out, axis=-1, keepdims=True) + NORM_EPS) * nw_ref[...]
    o_ref[...] = x_ref[...] + y


def _merge(pa, yb, p2, x2d, w1, wo, b_gate, norm_w, col_g0, col_g1, tm=512):
    t, d = x2d.shape
    row = lambda c: pl.BlockSpec((tm, d), lambda i: (i, c))
    full = pl.BlockSpec((d, d), lambda i: (0, 0))
    vec = lambda c: pl.BlockSpec((1, d), lambda i: (0, c))
    return pl.pallas_call(
        _merge_kernel,
        grid=(t // tm,),
        in_specs=[row(0), row(0), row(col_g0), row(col_g1), row(0), full, full, vec(0), vec(1), vec(0)],
        out_specs=row(0),
        out_shape=jax.ShapeDtypeStruct((t, d), F32),
        compiler_params=_cparams(("parallel",)),
        name="merge_out",
    )(pa, yb, p2, p2, x2d, w1, wo, b_gate.reshape(1, -1), b_gate.reshape(1, -1), norm_w.reshape(1, d))


def kernel(x, norm_pre_w, w_in, b_gate, conv_w, dn_a_log, dn_dt_bias, dn_onorm_w, rel_bias, w_branch, w_out,
           norm_post_w):
    b, s, d = x.shape
    n_heads = dn_a_log.shape[0]
    hd = dn_onorm_w.shape[0]
    width = n_heads * hd
    assert hd == LANES and rel_bias.shape[1] == n_heads and width == d
    assert s % MOBA_BLOCK == 0 and s % DN_GROUP == 0 and 2 * n_heads <= LANES
    t = b * s

    c_za = 3 * width
    c_beta = 4 * width
    c_qb = c_beta + 2 * n_heads
    assert conv_w.shape[1] == c_za
    w16 = w_in.astype(BF16)
    w_qkv_a = w16[:, :c_za]
    w_rest = jnp.concatenate([w16[:, c_za:c_beta], w16[:, c_qb:]], axis=1)
    w_small = jnp.pad(w16[:, c_beta:c_qb], ((0, 0), (0, LANES - 2 * n_heads)))
    nblk = width // LANES
    col = lambda k: k * nblk

    x2d = x.reshape(t, d)
    norm_w = norm_pre_w.astype(F32)
    col_scale = jnp.concatenate([jnp.ones((1, width), F32), jnp.full((1, width), hd ** -0.5, F32),
                                 jnp.ones((1, w_rest.shape[1] - 2 * width), F32)], axis=1)
    qkv_a, rest, small = _in_proj(x2d, norm_w, w_qkv_a, w_rest, col_scale, w_small, conv_w.astype(F32), s)
    qkv_a3 = qkv_a.reshape(b, s, -1)
    rest3 = rest.reshape(b, s, -1)

    pad = (0, LANES - 2 * n_heads)
    a_log_row = jnp.pad(jnp.concatenate([jnp.zeros_like(dn_a_log), dn_a_log]), pad).reshape(1, LANES).astype(F32)
    dt_row = jnp.pad(jnp.concatenate([jnp.zeros_like(dn_dt_bias), dn_dt_bias]), pad).reshape(1, LANES).astype(F32)
    pa = _deltanet(qkv_a3, rest3, small.reshape(b, s, LANES), a_log_row, dt_row,
                   dn_onorm_w.reshape(1, hd).astype(F32), w_branch[0].astype(BF16), n_heads,
                   col(0), col(1), col(2), col(0))
    yb = _moba(rel_bias.astype(F32), rest3, n_heads, col(1), col(2), col(3), col(4))

    out = _merge(pa.reshape(t, d), yb.reshape(t, d), rest, x2d,
                 w_branch[1].astype(BF16), w_out.astype(BF16),
                 b_gate.astype(F32), norm_post_w.astype(F32), 5, 6)
    return out.reshape(b, s, d)
```
